```python
import math
import jax, jax.numpy as jnp
from jax import lax
import numpy as np

D_MODEL = 2048
BATCH = 4
SEQ = 2048
DEPTH = 4

MEM_LEN = 256
EPS = 1e-5

SG_CHUNK = 128
N_SG = 4
D_SG = 512
SG_GROUP = D_SG // N_SG

N_MLA = 4
MLA_Q_RANK = 384
MLA_KV_RANK = 256
MLA_NOPE = 128
MLA_ROPE = 64
MLA_V = 128
ROPE_BASE = 10000.0
Q_BLOCK = 128

N_GLA = 4
GLA_DK = 64
GLA_DV = 128
GLA_GATE_RANK = 16
GLA_TAU = 16.0
GLA_CHUNK = 64
GLA_QK = N_GLA * GLA_DK
GLA_VW = N_GLA * GLA_DV

N_ML = 4
ML_DK = 64
ML_DV = 128
ML_CONV = 4
ML_CHUNK = 64
ML_QK = N_ML * ML_DK
ML_VW = N_ML * ML_DV

N_BRANCH = 4
BRANCH_W = 512

N_X = 4
X_HEAD = 128

N_EXPERTS = 16
N_GROUPS = 4
TOP_K = 2
D_EXPERT = 512

SG_IN = 2 * D_SG
MLA_IN = MLA_Q_RANK + MLA_KV_RANK + MLA_ROPE
GLA_IN = 2 * GLA_QK + 2 * GLA_VW + GLA_GATE_RANK
ML_IN = 2 * ML_QK + 2 * ML_VW + 2 * N_ML
GATE_IN = N_BRANCH * D_MODEL
IN_SPLITS = (SG_IN, MLA_IN, GLA_IN, ML_IN, GATE_IN)
D_IN = SG_IN + MLA_IN + GLA_IN + ML_IN + GATE_IN

kernel_name = "hybrid_gated_mixer_deepnorm_moe"


def _split(x, sizes):
    out, off = [], 0
    for s in sizes:
        out.append(x[..., off:off + s])
        off += s
    return out


def layer_norm(x, g, b):
    xf = x.astype(jnp.float32)
    mu = jnp.mean(xf, -1, keepdims=True)
    var = jnp.mean(jnp.square(xf - mu), -1, keepdims=True)
    y = (xf - mu) * lax.rsqrt(var + EPS) * g.astype(jnp.float32) + b.astype(jnp.float32)
    return y.astype(x.dtype)


def rms_norm(x, g):
    xf = x.astype(jnp.float32)
    y = xf * lax.rsqrt(jnp.mean(xf * xf, -1, keepdims=True) + EPS) * g.astype(jnp.float32)
    return y.astype(x.dtype)


def rope(x, pos):
    half = x.shape[-1] // 2
    freqs = ROPE_BASE ** (-jnp.arange(half, dtype=jnp.float32) / half)
    ang = pos.astype(jnp.float32)[..., None] * freqs
    cos, sin = jnp.cos(ang)[:, :, None, :], jnp.sin(ang)[:, :, None, :]
    xf = x.astype(jnp.float32)
    x1, x2 = xf[..., :half], xf[..., half:]
    return jnp.concatenate([x1 * cos - x2 * sin, x1 * sin + x2 * cos], -1).astype(x.dtype)


def spatial_gating(z, v_g, v_b, w_s, b_s):
    B, S, _ = z.shape
    u, v = z[..., :D_SG], z[..., D_SG:]
    v = layer_norm(v, v_g, v_b).reshape(B, S // SG_CHUNK, SG_CHUNK, N_SG, SG_GROUP)
    causal = jnp.tril(jnp.ones((SG_CHUNK, SG_CHUNK), dtype=bool))
    w = jnp.where(causal, w_s, 0.0).astype(v.dtype)
    mixed = jnp.einsum('gts,bcsgd->bctgd', w, v) + b_s.T[:, :, None]
    return u * mixed.reshape(B, S, D_SG)


def causal_block_attention(q, k, v, scale):
    B, S, H, Dq = q.shape
    nb = S // Q_BLOCK
    qb = q.reshape(B, nb, Q_BLOCK, H, Dq).transpose(1, 0, 2, 3, 4)
    k_idx = jnp.arange(S)

    def one_block(args):
        q_blk, i = args
        s = jnp.einsum('bthd,bshd->bhts', q_blk, k).astype(jnp.float32) * scale
        q_idx = i * Q_BLOCK + jnp.arange(Q_BLOCK)
        s = jnp.where(k_idx[None, :] <= q_idx[:, None], s, -jnp.inf)
        p = jax.nn.softmax(s, axis=-1).astype(v.dtype)
        return jnp.einsum('bhts,bshd->bthd', p, v)

    out = lax.map(one_block, (qb, jnp.arange(nb)))
    return out.transpose(1, 0, 2, 3, 4).reshape(B, S, H * v.shape[-1])


def mla_attention(c_q, c_kv, k_rope, pos, q_g, kv_g, w_uq, w_ukv):
    B, S, _ = c_q.shape
    q = (rms_norm(c_q, q_g) @ w_uq).reshape(B, S, N_MLA, MLA_NOPE + MLA_ROPE)
    kv = (rms_norm(c_kv, kv_g) @ w_ukv).reshape(B, S, N_MLA, MLA_NOPE + MLA_V)
    q = jnp.concatenate([q[..., :MLA_NOPE], rope(q[..., MLA_NOPE:], pos)], -1)
    k_r = jnp.broadcast_to(rope(k_rope[:, :, None, :], pos), (B, S, N_MLA, MLA_ROPE))
    k = jnp.concatenate([kv[..., :MLA_NOPE], k_r], -1)
    v = kv[..., MLA_NOPE:]
    return causal_block_attention(q, k, v, (MLA_NOPE + MLA_ROPE) ** -0.5)


def gla(q, k, v, o_gate, gate_lr, w_gate, b_gate, norm_g):
    B, S, _ = q.shape
    L, nc = GLA_CHUNK, S // GLA_CHUNK
    log_a = jax.nn.log_sigmoid((gate_lr @ w_gate + b_gate).astype(jnp.float32)) / GLA_TAU

    def chunks(t, d):
        return t.reshape(B, nc, L, N_GLA, d).transpose(1, 0, 3, 2, 4).astype(jnp.float32)

    qc = chunks(q, GLA_DK) * (GLA_DK ** -0.5)
    kc, vc, ac = chunks(k, GLA_DK), chunks(v, GLA_DV), chunks(log_a, GLA_DK)
    causal = jnp.tril(jnp.ones((L, L), dtype=bool))

    def step(state, inp):
        qi, ki, vi, ai = inp
        b = jnp.cumsum(ai, axis=2)
        b_last = b[:, :, -1:, :]
        q_t = qi * jnp.exp(b)
        att = jnp.where(causal, jnp.einsum('bhtd,bhsd->bhts', q_t, ki * jnp.exp(-b)), 0.0)
        o = jnp.einsum('bhts,bhsv->bhtv', att, vi) + jnp.einsum('bhtd,bhdv->bhtv', q_t, state)
        state = state * jnp.exp(b_last[:, :, 0, :])[..., None] + \
            jnp.einsum('bhsd,bhsv->bhdv', ki * jnp.exp(b_last - b), vi)
        return state, o

    init = jnp.zeros((B, N_GLA, GLA_DK, GLA_DV), jnp.float32)
    _, o = lax.scan(step, init, (qc, kc, vc, ac))
    o = rms_norm(o.transpose(1, 0, 3, 2, 4).reshape(B, S, N_GLA, GLA_DV), norm_g)
    return (o.reshape(B, S, GLA_VW) * jax.nn.silu(o_gate.astype(jnp.float32))).astype(v.dtype)


def causal_depthwise_conv(x, w, b):
    K, C = w.shape
    y = lax.conv_general_dilated(x, w[:, None, :], window_strides=(1,), padding=[(K - 1, 0)],
                                 dimension_numbers=('NWC', 'WIO', 'NWC'), feature_group_count=C)
    return y + b


def mlstm(q, k, v, o_pre, i_pre, f_pre, norm_g):
    B, S, _ = q.shape
    L, nc = ML_CHUNK, S // ML_CHUNK

    def chunks(t, d):
        return t.reshape(B, nc, L, N_ML, d).transpose(1, 0, 3, 2, 4).astype(jnp.float32)

    def gchunks(t):
        return t.reshape(B, nc, L, N_ML).transpose(1, 0, 3, 2).astype(jnp.float32)

    qc = chunks(q, ML_DK) * (ML_DK ** -0.5)
    kc, vc = chunks(k, ML_DK), chunks(v, ML_DV)
    fc = gchunks(jax.nn.log_sigmoid(f_pre.astype(jnp.float32)))
    ic = gchunks(i_pre)
    causal = jnp.tril(jnp.ones((L, L), dtype=bool))

    def step(carry, inp):
        C, n, m = carry
        qi, ki, vi, fi, ii = inp
        b = jnp.cumsum(fi, axis=-1)
        log_inter = b + m[..., None]
        log_d = jnp.where(causal, b[..., :, None] - b[..., None, :] + ii[..., None, :], -jnp.inf)
        m_t = jnp.maximum(log_inter, jnp.max(log_d, -1))
        w_inter = jnp.exp(log_inter - m_t)
        s = jnp.einsum('bhtd,bhsd->bhts', qi, ki) * jnp.exp(log_d - m_t[..., None])
        num = jnp.einsum('bhts,bhsv->bhtv', s, vi) + \
            w_inter[..., None] * jnp.einsum('bhtd,bhdv->bhtv', qi, C)
        den = jnp.sum(s, -1) + w_inter * jnp.einsum('bhtd,bhd->bht', qi, n)
        h = num / jnp.maximum(jnp.abs(den), jnp.exp(-m_t))[..., None]
        b_last = b[..., -1]
        log_w = b_last[..., None] - b + ii
        m_new = jnp.maximum(b_last + m, jnp.max(log_w, -1))
        decay = jnp.exp(b_last + m - m_new)
        w = jnp.exp(log_w - m_new[..., None])
        C = decay[..., None, None] * C + jnp.einsum('bhs,bhsd,bhsv->bhdv', w, ki, vi)
        n = decay[..., None] * n + jnp.einsum('bhs,bhsd->bhd', w, ki)
        return (C, n, m_new), h

    init = (jnp.zeros((B, N_ML, ML_DK, ML_DV), jnp.float32),
            jnp.zeros((B, N_ML, ML_DK), jnp.float32),
            jnp.zeros((B, N_ML), jnp.float32))
    _, h = lax.scan(step, init, (qc, kc, vc, fc, ic))
    h = rms_norm(h.transpose(1, 0, 3, 2, 4).reshape(B, S, N_ML, ML_DV), norm_g)
    return (h.reshape(B, S, ML_VW) * jax.nn.sigmoid(o_pre.astype(jnp.float32))).astype(v.dtype)


def hybrid_mixer(h, positions, w_in, sg_vnorm_g, sg_vnorm_b, sg_w_s, sg_b_s,
                 mla_qnorm_g, mla_kvnorm_g, mla_w_uq, mla_w_ukv,
                 gla_w_gate, gla_b_gate, gla_norm_g,
                 ml_conv_w, ml_conv_b, ml_gate_b, ml_norm_g, w_branch, w_out):
    B, S, D = h.shape
    z_sg, z_mla, z_gla, z_ml, z_gate = _split(h @ w_in, IN_SPLITS)
    y_a = spatial_gating(jax.nn.gelu(z_sg), sg_vnorm_g, sg_vnorm_b, sg_w_s, sg_b_s)
    c_q, c_kv, k_rope = _split(z_mla, (MLA_Q_RANK, MLA_KV_RANK, MLA_ROPE))
    y_b = mla_attention(c_q, c_kv, k_rope, positions, mla_qnorm_g, mla_kvnorm_g, mla_w_uq, mla_w_ukv)
    g_q, g_k, g_v, g_o, g_lr = _split(z_gla, (GLA_QK, GLA_QK, GLA_VW, GLA_VW, GLA_GATE_RANK))
    y_c = gla(g_q, g_k, g_v, g_o, g_lr, gla_w_gate, gla_b_gate, gla_norm_g)
    m_qk, m_v, m_o, m_if = _split(z_ml, (2 * ML_QK, ML_VW, ML_VW, 2 * N_ML))
    m_q, m_k = _split(jax.nn.silu(causal_depthwise_conv(m_qk, ml_conv_w, ml_conv_b)), (ML_QK, ML_QK))
    m_i, m_f = _split(m_if + ml_gate_b, (N_ML, N_ML))
    y_d = mlstm(m_q, m_k, m_v, m_o, m_i, m_f, ml_norm_g)
    branches = jnp.stack([y_a, y_b, y_c, y_d], axis=2)
    gates = jax.nn.sigmoid(z_gate).reshape(B, S, N_BRANCH, D)
    merged = jnp.einsum('bsnd,bsnd->bsd', gates, jnp.einsum('bsnw,nwd->bsnd', branches, w_branch))
    return merged @ w_out


def memory_cross_attention(h, mem, w_q, w_kv, w_o):
    B, S, _ = h.shape
    M = mem.shape[1]
    q = (h @ w_q).reshape(B, S, N_X, X_HEAD)
    kv = (mem @ w_kv).reshape(B, M, 2, N_X, X_HEAD)
    s = jnp.einsum('bthd,bshd->bhts', q, kv[:, :, 0]).astype(jnp.float32) * (X_HEAD ** -0.5)
    p = jax.nn.softmax(s, axis=-1).astype(h.dtype)
    o = jnp.einsum('bhts,bshd->bthd', p, kv[:, :, 1]).reshape(B, S, N_X * X_HEAD)
    return o @ w_o


def grouped_moe(h, w_router, router_bias, w_gate, w_up, w_down):
    B, S, D = h.shape
    t = h.reshape(B * S, D)
    aff = jax.nn.sigmoid((t @ w_router).astype(jnp.float32))
    per_group = N_EXPERTS // N_GROUPS
    biased = (aff + router_bias.astype(jnp.float32)).reshape(-1, N_GROUPS, per_group)
    group_score = lax.top_k(biased, TOP_K)[0].sum(-1)
    g_idx = jnp.argmax(group_score, axis=-1)
    in_group = jnp.take_along_axis(biased, g_idx[:, None, None], axis=1)[:, 0]
    _, local = lax.top_k(in_group, TOP_K)
    exp_idx = g_idx[:, None] * per_group + local
    sel = jnp.take_along_axis(aff, exp_idx, axis=1)
    wts = sel / jnp.sum(sel, -1, keepdims=True)
    gates = jnp.sum(jax.nn.one_hot(exp_idx, N_EXPERTS, dtype=jnp.float32) * wts[..., None], axis=1)
    hid = jax.nn.silu(jnp.einsum('nd,edf->nef', t, w_gate)) * jnp.einsum('nd,edf->nef', t, w_up)
    hid = hid * gates.astype(hid.dtype)[..., None]
    return jnp.einsum('nef,efd->nd', hid, w_down).reshape(B, S, D)


def setup_inputs(seed: int = 0) -> dict:
    key = jax.random.key(seed)
    ks = iter(jax.random.split(key, 48))
    f32 = jnp.float32

    def nrm(shape, scale):
        return jax.random.normal(next(ks), shape, f32) * scale

    Ld, D = DEPTH, D_MODEL
    beta = (8.0 * DEPTH) ** -0.25
    x = nrm((BATCH, SEQ, D), 1.0)
    mem = nrm((BATCH, MEM_LEN, D), 1.0)
    positions = jax.random.randint(next(ks), (BATCH, 1), 0, 4096, dtype=jnp.int32) + \
        jnp.arange(SEQ, dtype=jnp.int32)[None, :]
    return {
        "x": x,
        "mem": mem,
        "positions": positions,
        "ln_in_g": 1.0 + nrm((D,), 0.05),
        "ln_in_b": nrm((D,), 0.02),
        "w_in": nrm((Ld, D, D_IN), D ** -0.5),
        "sg_vnorm_g": 1.0 + nrm((Ld, D_SG), 0.05),
        "sg_vnorm_b": nrm((Ld, D_SG), 0.02),
        "sg_w_s": nrm((Ld, N_SG, SG_CHUNK, SG_CHUNK), 0.5 * SG_CHUNK ** -0.5),
        "sg_b_s": 1.0 + nrm((Ld, N_SG, SG_CHUNK), 0.1),
        "mla_qnorm_g": 1.0 + nrm((Ld, MLA_Q_RANK), 0.05),
        "mla_kvnorm_g": 1.0 + nrm((Ld, MLA_KV_RANK), 0.05),
        "mla_w_uq": nrm((Ld, MLA_Q_RANK, N_MLA * (MLA_NOPE + MLA_ROPE)), MLA_Q_RANK ** -0.5),
        "mla_w_ukv": nrm((Ld, MLA_KV_RANK, N_MLA * (MLA_NOPE + MLA_V)), MLA_KV_RANK ** -0.5),
        "gla_w_gate": nrm((Ld, GLA_GATE_RANK, GLA_QK), GLA_GATE_RANK ** -0.5),
        "gla_b_gate": nrm((Ld, GLA_QK), 0.1),
        "gla_norm_g": 1.0 + nrm((Ld, GLA_DV), 0.05),
        "ml_conv_w": nrm((Ld, ML_CONV, 2 * ML_QK), ML_CONV ** -0.5),
        "ml_conv_b": nrm((Ld, 2 * ML_QK), 0.02),
        "ml_gate_b": jnp.concatenate(
            [nrm((Ld, N_ML), 0.1),
             jnp.broadcast_to(jnp.linspace(3.0, 6.0, N_ML, dtype=f32), (Ld, N_ML)) + nrm((Ld, N_ML), 0.1)],
            axis=-1),
        "ml_norm_g": 1.0 + nrm((Ld, ML_DV), 0.05),
        "w_branch": nrm((Ld, N_BRANCH, BRANCH_W, D), beta * BRANCH_W ** -0.5),
        "w_out": nrm((Ld, D, D), beta * D ** -0.5),
        "ln1_g": 1.0 + nrm((Ld, D), 0.05),
        "ln1_b": nrm((Ld, D), 0.02),
        "x_w_q": nrm((Ld, D, N_X * X_HEAD), D ** -0.5),
        "x_w_kv": nrm((Ld, D, 2 * N_X * X_HEAD), D ** -0.5),
        "x_w_o": nrm((Ld, N_X * X_HEAD, D), beta * (N_X * X_HEAD) ** -0.5),
        "ln2_g": 1.0 + nrm((Ld, D), 0.05),
        "ln2_b": nrm((Ld, D), 0.02),
        "w_router": nrm((D, N_EXPERTS), D ** -0.5),
        "router_bias": nrm((N_EXPERTS,), 0.01),
        "moe_w_gate": nrm((Ld, N_EXPERTS, D, D_EXPERT), D ** -0.5),
        "moe_w_up": nrm((Ld, N_EXPERTS, D, D_EXPERT), D ** -0.5),
        "moe_w_down": nrm((Ld, N_EXPERTS, D_EXPERT, D), beta * D_EXPERT ** -0.5),
        "ln3_g": 1.0 + nrm((Ld, D), 0.05),
        "ln3_b": nrm((Ld, D), 0.02),
    }


def reference(x, mem, positions, ln_in_g, ln_in_b, w_in, sg_vnorm_g, sg_vnorm_b, sg_w_s, sg_b_s,
              mla_qnorm_g, mla_kvnorm_g, mla_w_uq, mla_w_ukv, gla_w_gate, gla_b_gate, gla_norm_g,
              ml_conv_w, ml_conv_b, ml_gate_b, ml_norm_g, w_branch, w_out, ln1_g, ln1_b,
              x_w_q, x_w_kv, x_w_o, ln2_g, ln2_b, w_router, router_bias,
              moe_w_gate, moe_w_up, moe_w_down, ln3_g, ln3_b):
    alpha = (2.0 * DEPTH) ** 0.25
    h = layer_norm(x, ln_in_g, ln_in_b)
    for l in range(DEPTH):
        y = hybrid_mixer(h, positions, w_in[l], sg_vnorm_g[l], sg_vnorm_b[l], sg_w_s[l], sg_b_s[l],
                         mla_qnorm_g[l], mla_kvnorm_g[l], mla_w_uq[l], mla_w_ukv[l],
                         gla_w_gate[l], gla_b_gate[l], gla_norm_g[l],
                         ml_conv_w[l], ml_conv_b[l], ml_gate_b[l], ml_norm_g[l], w_branch[l], w_out[l])
        h = layer_norm(alpha * h + y, ln1_g[l], ln1_b[l])
        y = memory_cross_attention(h, mem, x_w_q[l], x_w_kv[l], x_w_o[l])
        h = layer_norm(alpha * h + y, ln2_g[l], ln2_b[l])
        y = grouped_moe(h, w_router, router_bias, moe_w_gate[l], moe_w_up[l], moe_w_down[l])
        h = layer_norm(alpha * h + y, ln3_g[l], ln3_b[l])
    return h
```

```python
import functools
import math

import jax
import jax.numpy as jnp
from jax import lax
from jax.experimental import pallas as pl
from jax.experimental.pallas import tpu as pltpu

F32 = jnp.float32
BF16 = jnp.bfloat16

D_MODEL = 2048
DEPTH = 4
EPS = 1e-5
ALPHA = (2.0 * DEPTH) ** 0.25

SG_CHUNK = 128
N_SG = 4
D_SG = 512
N_MLA = 4
MLA_Q_RANK = 384
MLA_KV_RANK = 256
MLA_NOPE = 128
MLA_ROPE = 64
MLA_V = 128
MLA_QK = 256
ROPE_BASE = 10000.0
N_GLA = 4
GLA_DK = 64
GLA_DV = 128
GLA_RANK = 16
GLA_TAU = 16.0
GLA_CHUNK = 64
N_ML = 4
ML_DK = 64
ML_DV = 128
ML_CHUNK = 128
N_X = 4
X_HEAD = 128
N_EXPERTS = 16
D_EXPERT = 512

LANES = 128
VMEM_LIMIT = 48 * 1024 * 1024

_NT = (((1,), (1,)), ((), ()))
_TN = (((0,), (0,)), ((), ()))


def _cparams(*sem):
    return pltpu.CompilerParams(dimension_semantics=sem, vmem_limit_bytes=VMEM_LIMIT)


def _dot(a, b):
    return jnp.dot(a, b, preferred_element_type=F32)


def _dotg(a, b, dims):
    return lax.dot_general(a, b, dims, preferred_element_type=F32)


def _split3(a):
    a1 = a.astype(BF16)
    r1 = a - a1.astype(F32)
    a2 = r1.astype(BF16)
    a3 = (r1 - a2.astype(F32)).astype(BF16)
    return a1, a2, a3


def _dot_exact_rhs(a, ones_bf16):
    a1, a2, a3 = _split3(a)
    return _dot(a1, ones_bf16) + _dot(a2, ones_bf16) + _dot(a3, ones_bf16)


def _dot_exact_lhs(ones_bf16, a):
    a1, a2, a3 = _split3(a)
    return _dot(ones_bf16, a1) + _dot(ones_bf16, a2) + _dot(ones_bf16, a3)


def _dot_hi(a, b):
    a1 = a.astype(BF16)
    a2 = (a - a1.astype(F32)).astype(BF16)
    b1 = b.astype(BF16)
    b2 = (b - b1.astype(F32)).astype(BF16)
    return _dot(a1, b1) + _dot(a2, b1) + _dot(a1, b2)


def _sigmoid(x):
    return 1.0 / (1.0 + jnp.exp(-x))


def _log_sigmoid(x):
    return jnp.minimum(x, 0.0) - jnp.log(1.0 + jnp.exp(-jnp.abs(x)))


def _layer_norm(t, g, b):
    mu = jnp.mean(t, axis=-1, keepdims=True)
    c = t - mu
    var = jnp.mean(c * c, axis=-1, keepdims=True)
    return c * lax.rsqrt(var + EPS) * g + b


def _rms_norm(t, g):
    return t * lax.rsqrt(jnp.mean(t * t, axis=-1, keepdims=True) + EPS) * g


def _ln_kernel(x_ref, g_ref, b_ref, of_ref, ob_ref):
    y = _layer_norm(x_ref[...], g_ref[...], b_ref[...])
    of_ref[...] = y
    ob_ref[...] = y.astype(BF16)


def _ln(x, g, b, tm=256):
    n, d = x.shape
    row = pl.BlockSpec((tm, d), lambda i: (i, 0))
    par = pl.BlockSpec((1, d), lambda i: (0, 0))
    return pl.pallas_call(
        _ln_kernel, grid=(n // tm,), in_specs=[row, par, par], out_specs=[row, row],
        out_shape=[jax.ShapeDtypeStruct((n, d), F32), jax.ShapeDtypeStruct((n, d), BF16)],
        compiler_params=_cparams("parallel"), name="ln_in")(x, g.reshape(1, d), b.reshape(1, d))


def _mm_kernel(x_ref, w_ref, o_ref, *, act):
    acc = _dot(x_ref[...], w_ref[...])
    if act == "sigmoid":
        acc = _sigmoid(acc)
    o_ref[...] = acc.astype(o_ref.dtype)


def _mm(x, w, out_dtype, act=None, tm=512, tn=None, name="mm"):
    n, k = x.shape
    m = w.shape[1]
    tn = m if tn is None else tn
    return pl.pallas_call(
        functools.partial(_mm_kernel, act=act), grid=(n // tm, m // tn),
        in_specs=[pl.BlockSpec((tm, k), lambda i, j: (i, 0)),
                  pl.BlockSpec((k, tn), lambda i, j: (0, j))],
        out_specs=pl.BlockSpec((tm, tn), lambda i, j: (i, j)),
        out_shape=jax.ShapeDtypeStruct((n, m), out_dtype),
        compiler_params=_cparams("parallel", "parallel"), name=name)(x, w)


def _mm_res_ln_kernel(x_ref, w_ref, h_ref, g_ref, b_ref, of_ref, ob_ref, *acc, nk):
    def finish(y):
        t = ALPHA * h_ref[...] + y
        o = _layer_norm(t, g_ref[...], b_ref[...])
        of_ref[...] = o
        ob_ref[...] = o.astype(BF16)

    if nk == 1:
        finish(_dot(x_ref[...], w_ref[...]))
        return
    acc_ref, = acc
    kk = pl.program_id(1)

    @pl.when(kk == 0)
    def _():
        acc_ref[...] = jnp.zeros_like(acc_ref)

    acc_ref[...] += _dot(x_ref[...], w_ref[...])

    @pl.when(kk == nk - 1)
    def _():
        finish(acc_ref[...])


def _mm_res_ln(x, w, h, g, b, tm=256, tk=None, name="mm_res_ln"):
    n, k = x.shape
    d = w.shape[1]
    tk = k if tk is None else tk
    nk = k // tk
    row = pl.BlockSpec((tm, d), lambda i, j: (i, 0))
    par = pl.BlockSpec((1, d), lambda i, j: (0, 0))
    scratch = [] if nk == 1 else [pltpu.VMEM((tm, d), F32)]
    return pl.pallas_call(
        functools.partial(_mm_res_ln_kernel, nk=nk), grid=(n // tm, nk),
        in_specs=[pl.BlockSpec((tm, tk), lambda i, j: (i, j)),
                  pl.BlockSpec((tk, d), lambda i, j: (j, 0)), row, par, par],
        out_specs=[row, row],
        out_shape=[jax.ShapeDtypeStruct((n, d), F32), jax.ShapeDtypeStruct((n, d), BF16)],
        scratch_shapes=scratch,
        compiler_params=_cparams("parallel", "arbitrary"), name=name)(
            x, w, h, g.reshape(1, d), b.reshape(1, d))


def _sg_kernel(z_ref, vg_ref, vb_ref, ws_ref, bst_ref, o_ref, *, nchunk):
    z = z_ref[...]
    z = 0.5 * z * (1.0 + jnp.tanh(math.sqrt(2.0 / math.pi) * (z + 0.044715 * (z * z * z))))
    u = z[:, :D_SG]
    vn = _layer_norm(z[:, D_SG:], vg_ref[...], vb_ref[...]).astype(BF16)
    r = lax.broadcasted_iota(jnp.int32, (SG_CHUNK, SG_CHUNK), 0)
    c = lax.broadcasted_iota(jnp.int32, (SG_CHUNK, SG_CHUNK), 1)
    causal = c <= r
    gw = SG_CHUNK
    for g in range(N_SG):
        w = jnp.where(causal, ws_ref[g], 0.0).astype(BF16)
        bias = bst_ref[:, g:g + 1]
        for ci in range(nchunk):
            rs = slice(ci * SG_CHUNK, (ci + 1) * SG_CHUNK)
            cs = slice(g * gw, (g + 1) * gw)
            mixed = _dot(w, vn[rs, cs]) + bias
            o_ref[rs, cs] = (u[rs, cs] * mixed).astype(BF16)


def _sg(z, vg, vb, ws, bs, tm=512):
    n = z.shape[0]
    return pl.pallas_call(
        functools.partial(_sg_kernel, nchunk=tm // SG_CHUNK), grid=(n // tm,),
        in_specs=[pl.BlockSpec((tm, 2 * D_SG), lambda i: (i, 0)),
                  pl.BlockSpec((1, D_SG), lambda i: (0, 0)),
                  pl.BlockSpec((1, D_SG), lambda i: (0, 0)),
                  pl.BlockSpec((N_SG, SG_CHUNK, SG_CHUNK), lambda i: (0, 0, 0)),
                  pl.BlockSpec((SG_CHUNK, N_SG), lambda i: (0, 0))],
        out_specs=pl.BlockSpec((tm, D_SG), lambda i: (i, 0)),
        out_shape=jax.ShapeDtypeStruct((n, D_SG), BF16),
        compiler_params=_cparams("parallel"), name="sg")(
            z, vg.reshape(1, D_SG), vb.reshape(1, D_SG), ws, bs.T)


def _rope_kernel(pos_ref, cos_ref, sin_ref):
    half = MLA_ROPE // 2
    lane = lax.broadcasted_iota(jnp.int32, (1, LANES), 1)
    idx = jnp.bitwise_and(lane, half - 1).astype(F32)
    freq = jnp.exp(idx * (-math.log(ROPE_BASE) / half))
    ang = pos_ref[...] * freq
    c = jnp.cos(ang)
    s = jnp.sin(ang)
    cos_ref[...] = jnp.where(lane < MLA_ROPE, c, 0.0)
    sin_ref[...] = jnp.where(lane < half, -s, jnp.where(lane < MLA_ROPE, s, 0.0))


def _rope_tables(posb, tm=512):
    n = posb.shape[0]
    row = pl.BlockSpec((tm, LANES), lambda i: (i, 0))
    return pl.pallas_call(
        _rope_kernel, grid=(n // tm,), in_specs=[row], out_specs=[row, row],
        out_shape=[jax.ShapeDtypeStruct((n, LANES), F32)] * 2,
        compiler_params=_cparams("parallel"), name="rope_tables")(posb)


def _mla_prep_kernel(z_ref, cos_ref, sin_ref, qg_ref, kvg_ref, wuq_ref, wuk_ref, wuv_ref,
                     q_ref, k_ref, v_ref):
    z = z_ref[...]
    cq = _rms_norm(z[:, :MLA_Q_RANK], qg_ref[...]).astype(BF16)
    o1 = MLA_Q_RANK + MLA_KV_RANK
    ckv = _rms_norm(z[:, MLA_Q_RANK:o1], kvg_ref[...]).astype(BF16)
    cos_p = cos_ref[...]
    sin_p = sin_ref[...]
    k_tail = (z[:, o1:o1 + LANES] * cos_p + z[:, o1 + LANES:o1 + 2 * LANES] * sin_p).astype(BF16)
    qa = _dot(cq, wuq_ref[...])
    kn = _dot(ckv, wuk_ref[...])
    v_ref[...] = _dot(ckv, wuv_ref[...]).astype(BF16)
    scale = (MLA_NOPE + MLA_ROPE) ** -0.5
    for h in range(N_MLA):
        b0 = h * 3 * LANES
        q_tail = qa[:, b0 + LANES:b0 + 2 * LANES] * cos_p + qa[:, b0 + 2 * LANES:b0 + 3 * LANES] * sin_p
        q_ref[:, h * MLA_QK:h * MLA_QK + LANES] = (qa[:, b0:b0 + LANES] * scale).astype(BF16)
        q_ref[:, h * MLA_QK + LANES:(h + 1) * MLA_QK] = (q_tail * scale).astype(BF16)
        k_ref[:, h * MLA_QK:h * MLA_QK + LANES] = kn[:, h * LANES:(h + 1) * LANES].astype(BF16)
        k_ref[:, h * MLA_QK + LANES:(h + 1) * MLA_QK] = k_tail


def _mla_prep(z, cos_p, sin_p, qg, kvg, wuq, wuk, wuv, tm=512):
    n, zw = z.shape
    row = lambda w: pl.BlockSpec((tm, w), lambda i: (i, 0))
    full = lambda a: pl.BlockSpec(a.shape, lambda i: (0,) * a.ndim)
    qg = qg.reshape(1, -1)
    kvg = kvg.reshape(1, -1)
    return pl.pallas_call(
        _mla_prep_kernel, grid=(n // tm,),
        in_specs=[row(zw), row(LANES), row(LANES), full(qg), full(kvg), full(wuq), full(wuk), full(wuv)],
        out_specs=[row(N_MLA * MLA_QK), row(N_MLA * MLA_QK), row(N_MLA * MLA_V)],
        out_shape=[jax.ShapeDtypeStruct((n, N_MLA * MLA_QK), BF16),
                   jax.ShapeDtypeStruct((n, N_MLA * MLA_QK), BF16),
                   jax.ShapeDtypeStruct((n, N_MLA * MLA_V), BF16)],
        compiler_params=_cparams("parallel"), name="mla_prep")(z, cos_p, sin_p, qg, kvg, wuq, wuk, wuv)


def _mla_attn_kernel(q_ref, k_ref, v_ref, o_ref, *, tq, tk):
    i = pl.program_id(2)
    q = q_ref[...]
    qpos = i * tq + lax.broadcasted_iota(jnp.int32, (tq, tk), 0)
    kofs = lax.broadcasted_iota(jnp.int32, (tq, tk), 1)

    def body(j, carry):
        m, l, acc = carry
        start = pl.multiple_of(j * tk, tk)
        s = _dotg(q, k_ref[pl.ds(start, tk), :], _NT)
        s = jnp.where(kofs + j * tk <= qpos, s, -jnp.inf)
        m_new = jnp.maximum(m, jnp.max(s, axis=-1, keepdims=True))
        p = jnp.exp(s - m_new)
        a = jnp.exp(m - m_new)
        l = a * l + jnp.sum(p, axis=-1, keepdims=True)
        acc = a * acc + _dot(p.astype(BF16), v_ref[pl.ds(start, tk), :])
        return m_new, l, acc

    init = (jnp.full((tq, 1), -jnp.inf, F32), jnp.zeros((tq, 1), F32), jnp.zeros((tq, MLA_V), F32))
    nblk = (i * tq + tq + tk - 1) // tk
    _, l, acc = lax.fori_loop(0, nblk, body, init)
    o_ref[...] = (acc / l).astype(BF16)


def _mla_attn(q, k, v, batch, seq, tq=512, tk=512):
    n = q.shape[0]
    nq = seq // tq
    return pl.pallas_call(
        functools.partial(_mla_attn_kernel, tq=tq, tk=tk), grid=(batch, N_MLA, nq),
        in_specs=[pl.BlockSpec((tq, MLA_QK), lambda b, h, i: (b * nq + i, h)),
                  pl.BlockSpec((seq, MLA_QK), lambda b, h, i: (b, h)),
                  pl.BlockSpec((seq, MLA_V), lambda b, h, i: (b, h))],
        out_specs=pl.BlockSpec((tq, MLA_V), lambda b, h, i: (b * nq + i, h)),
        out_shape=jax.ShapeDtypeStruct((n, N_MLA * MLA_V), BF16),
        compiler_params=_cparams("parallel", "parallel", "arbitrary"), name="mla_attn")(q, k, v)


def _gla_kernel(q_ref, k_ref, v_ref, og_ref, lr_ref, wg_ref, bg_ref, ng_ref, o_ref, st_ref, *, nchunk):
    L = GLA_CHUNK
    qkw = N_GLA * GLA_DK

    @pl.when(pl.program_id(1) == 0)
    def _():
        st_ref[...] = jnp.zeros_like(st_ref)

    logits = _dot_hi(lr_ref[...], wg_ref[...]) + bg_ref[...]
    log_a = _log_sigmoid(logits) * (1.0 / GLA_TAU)
    lane = lax.broadcasted_iota(jnp.int32, (1, qkw), 1)
    masks = [((lane >= h * GLA_DK) & (lane < (h + 1) * GLA_DK)).astype(F32) for h in range(N_GLA)]
    r = lax.broadcasted_iota(jnp.int32, (L, L), 0)
    c = lax.broadcasted_iota(jnp.int32, (L, L), 1)
    causal = c <= r
    tril = jnp.where(causal, 1.0, 0.0).astype(BF16)
    ng = ng_ref[...]
    st = st_ref[...]
    for ci in range(nchunk):
        rs = slice(ci * L, (ci + 1) * L)
        b = _dot_exact_lhs(tril, log_a[rs])
        b_last = b[L - 1:L, :]
        q = q_ref[rs, :] * (GLA_DK ** -0.5)
        k = k_ref[rs, :]
        qt = q * jnp.exp(b)
        kt = (k * jnp.exp(-b)).astype(BF16)
        kd = (k * jnp.exp(b_last - b)).astype(BF16)
        qstack = jnp.concatenate([qt * masks[h] for h in range(N_GLA)], axis=0).astype(BF16)
        att = _dotg(qstack, kt, _NT)
        inter = _dotg(qstack, st.astype(BF16), _NT)
        vb = v_ref[rs, :].astype(BF16)
        for h in range(N_GLA):
            hs = slice(h * L, (h + 1) * L)
            vs = slice(h * GLA_DV, (h + 1) * GLA_DV)
            a_h = jnp.where(causal, att[hs], 0.0).astype(BF16)
            o_h = _rms_norm(_dot(a_h, vb[:, vs]) + inter[hs], ng)
            g = og_ref[rs, vs]
            o_ref[rs, vs] = (o_h * (g * _sigmoid(g))).astype(BF16)
        upd = _dotg(vb, kd, _TN)
        new = st * jnp.exp(b_last)
        for h in range(N_GLA):
            new = new + upd[h * GLA_DV:(h + 1) * GLA_DV] * masks[h]
        st = new
    st_ref[...] = st


def _gla(z, wg, bg, ng, batch, seq, tm=256):
    n = z.shape[0]
    ns = seq // tm
    qkw = N_GLA * GLA_DK
    vw = N_GLA * GLA_DV
    col = lambda w, j: pl.BlockSpec((tm, w), lambda b, s: (b * ns + s, j))
    full = lambda a: pl.BlockSpec(a.shape, lambda b, s: (0,) * a.ndim)
    bg = bg.reshape(1, qkw)
    ng = ng.reshape(1, GLA_DV)
    return pl.pallas_call(
        functools.partial(_gla_kernel, nchunk=tm // GLA_CHUNK), grid=(batch, ns),
        in_specs=[col(qkw, 0), col(qkw, 1), col(vw, 1), col(vw, 2), col(LANES, (2 * qkw + 2 * vw) // LANES),
                  full(wg), full(bg), full(ng)],
        out_specs=pl.BlockSpec((tm, vw), lambda b, s: (b * ns + s, 0)),
        out_shape=jax.ShapeDtypeStruct((n, vw), BF16),
        scratch_shapes=[pltpu.VMEM((GLA_DV, qkw), F32)],
        compiler_params=_cparams("parallel", "arbitrary"), name="gla")(z, z, z, z, z, wg, bg, ng)


def _mlstm_kernel(qk_ref, v_ref, op_ref, if_ref, ift_ref, cw_ref, cb_ref, gb_ref, gbt_ref, ng_ref,
                  o_ref, ct_ref, n_ref, m_ref, tail_ref, *, nchunk):
    L = ML_CHUNK
    qkw = N_ML * ML_DK
    tm = nchunk * L

    @pl.when(pl.program_id(1) == 0)
    def _():
        ct_ref[...] = jnp.zeros_like(ct_ref)
        n_ref[...] = jnp.zeros_like(n_ref)
        m_ref[...] = jnp.zeros_like(m_ref)
        tail_ref[...] = jnp.zeros_like(tail_ref)

    x = qk_ref[...]
    tail = tail_ref[...]
    row8 = lax.broadcasted_iota(jnp.int32, (8, 2 * qkw), 0)
    acc = x * cw_ref[3:4, :] + cb_ref[...]
    for j in range(1, 4):
        rx = pltpu.roll(x, j, 0)
        fix = jnp.where(row8 < j, pltpu.roll(tail, j, 0), rx[0:8])
        acc = acc + jnp.concatenate([fix, rx[8:]], axis=0) * cw_ref[3 - j:4 - j, :]
    tail_ref[...] = x[tm - 8:tm]
    y = acc * _sigmoid(acc)
    q = y[:, :qkw] * (ML_DK ** -0.5)
    k = y[:, qkw:]

    gates = if_ref[...] + gb_ref[...]
    fc = _log_sigmoid(gates)
    gt = ift_ref[0] + gbt_ref[...]
    fct = _log_sigmoid(gt)

    lane = lax.broadcasted_iota(jnp.int32, (1, qkw), 1)
    masks = [((lane >= h * ML_DK) & (lane < (h + 1) * ML_DK)).astype(F32) for h in range(N_ML)]
    lane_m = lax.broadcasted_iota(jnp.int32, (1, LANES), 1)
    r = lax.broadcasted_iota(jnp.int32, (L, L), 0)
    c = lax.broadcasted_iota(jnp.int32, (L, L), 1)
    causal = c <= r
    tril = jnp.where(causal, 1.0, 0.0).astype(BF16)
    triu = jnp.where(r <= c, 1.0, 0.0).astype(BF16)
    ng = ng_ref[...]
    ct = ct_ref[...]
    nrow = n_ref[...]
    mrow = m_ref[...]

    for ci in range(nchunk):
        rs = slice(ci * L, (ci + 1) * L)
        bcol = _dot_exact_lhs(tril, fc[rs])
        brow = _dot_exact_rhs(fct[:, rs], triu)
        qc = q[rs]
        kc = k[rs]
        kcb = kc.astype(BF16)
        ctb = ct.astype(BF16)
        decay_row = jnp.zeros((1, qkw), F32)
        ct_upd = jnp.zeros_like(ct)
        n_upd = jnp.zeros_like(nrow)
        m_next = jnp.zeros_like(mrow)
        for h in range(N_ML):
            vs = slice(h * ML_DV, (h + 1) * ML_DV)
            bc = bcol[:, N_ML + h:N_ML + h + 1]
            br = brow[N_ML + h:N_ML + h + 1, :]
            ir = gt[h:h + 1, rs]
            ic = gates[rs, h:h + 1]
            m_h = mrow[:, h:h + 1]
            log_inter = bc + m_h
            log_d = jnp.where(causal, bc - br + ir, -jnp.inf)
            m_t = jnp.maximum(log_inter, jnp.max(log_d, axis=-1, keepdims=True))
            w_inter = jnp.exp(log_inter - m_t)
            qm = qc * masks[h]
            qmb = qm.astype(BF16)
            s_mat = _dotg(qmb, kcb, _NT) * jnp.exp(log_d - m_t)
            vh = v_ref[rs, vs].astype(BF16)
            num = _dot(s_mat.astype(BF16), vh) + w_inter * _dotg(qmb, ctb, _NT)
            den = jnp.sum(s_mat, axis=-1, keepdims=True) + \
                w_inter * jnp.sum(qm * nrow, axis=-1, keepdims=True)
            hh = num / jnp.maximum(jnp.abs(den), jnp.exp(-m_t))
            hh = _rms_norm(hh, ng)
            o_ref[rs, vs] = (hh * _sigmoid(op_ref[rs, vs])).astype(BF16)
            b_last = bc[L - 1:L, :]
            log_w = b_last - bc + ic
            m_new = jnp.maximum(b_last + m_h, jnp.max(log_w, axis=0, keepdims=True))
            decay = jnp.exp(b_last + m_h - m_new)
            kw = kc * masks[h] * jnp.exp(log_w - m_new)
            ct_upd = ct_upd + _dotg(vh, kw.astype(BF16), _TN)
            n_upd = n_upd + jnp.sum(kw, axis=0, keepdims=True)
            decay_row = decay_row + decay * masks[h]
            m_next = m_next + jnp.where(lane_m == h, m_new, 0.0)
        ct = ct * decay_row + ct_upd
        nrow = nrow * decay_row + n_upd
        mrow = m_next

    ct_ref[...] = ct
    n_ref[...] = nrow
    m_ref[...] = mrow


def _mlstm(z, ift, cw, cb, gb, ng, batch, seq, tm=512):
    n = z.shape[0]
    ns = seq // tm
    qkw = N_ML * ML_DK
    vw = N_ML * ML_DV
    col = lambda w, j: pl.BlockSpec((tm, w), lambda b, s: (b * ns + s, j))
    full = lambda a: pl.BlockSpec(a.shape, lambda b, s: (0,) * a.ndim)
    cb = cb.reshape(1, 2 * qkw)
    gbp = jnp.pad(gb, (0, LANES - 2 * N_ML)).reshape(1, LANES)
    gbt = gb.reshape(2 * N_ML, 1)
    ng = ng.reshape(1, ML_DV)
    return pl.pallas_call(
        functools.partial(_mlstm_kernel, nchunk=tm // ML_CHUNK), grid=(batch, ns),
        in_specs=[col(2 * qkw, 0), col(vw, 1), col(vw, 2), col(LANES, (2 * qkw + 2 * vw) // LANES),
                  pl.BlockSpec((1, 2 * N_ML, tm), lambda b, s: (b, 0, s)),
                  full(cw), full(cb), full(gbp), full(gbt), full(ng)],
        out_specs=pl.BlockSpec((tm, vw), lambda b, s: (b * ns + s, 0)),
        out_shape=jax.ShapeDtypeStruct((n, vw), BF16),
        scratch_shapes=[pltpu.VMEM((ML_DV, qkw), F32), pltpu.VMEM((1, qkw), F32),
                        pltpu.VMEM((1, LANES), F32), pltpu.VMEM((8, 2 * qkw), F32)],
        compiler_params=_cparams("parallel", "arbitrary"), name="mlstm")(
            z, z, z, z, ift, cw, cb, gbp, gbt, ng)


def _merge_kernel(ya_ref, yb_ref, yc_ref, yd_ref, g_ref, wb_ref, o_ref):
    acc = None
    for i, y_ref in enumerate((ya_ref, yb_ref, yc_ref, yd_ref)):
        p = _dot(y_ref[...], wb_ref[i])
        t = g_ref[:, i * D_MODEL:(i + 1) * D_MODEL].astype(F32) * p
        acc = t if acc is None else acc + t
    o_ref[...] = acc.astype(BF16)


def _merge(ya, yb, yc, yd, gates, wb, tm=256):
    n, bw = ya.shape
    row = pl.BlockSpec((tm, bw), lambda i: (i, 0))
    return pl.pallas_call(
        _merge_kernel, grid=(n // tm,),
        in_specs=[row, row, row, row, pl.BlockSpec((tm, 4 * D_MODEL), lambda i: (i, 0)),
                  pl.BlockSpec(wb.shape, lambda i: (0, 0, 0))],
        out_specs=pl.BlockSpec((tm, D_MODEL), lambda i: (i, 0)),
        out_shape=jax.ShapeDtypeStruct((n, D_MODEL), BF16),
        compiler_params=_cparams("parallel"), name="merge")(ya, yb, yc, yd, gates, wb)


def _xattn_kernel(hb_ref, hf_ref, wq_ref, k_ref, v_ref, wo_ref, g_ref, b_ref, of_ref, ob_ref):
    q = (_dot(hb_ref[...], wq_ref[...]) * (X_HEAD ** -0.5)).astype(BF16)
    outs = []
    for h in range(N_X):
        hs = slice(h * X_HEAD, (h + 1) * X_HEAD)
        s = _dotg(q[:, hs], k_ref[:, hs], _NT)
        p = jnp.exp(s - jnp.max(s, axis=-1, keepdims=True))
        l = jnp.sum(p, axis=-1, keepdims=True)
        outs.append((_dot(p.astype(BF16), v_ref[:, hs]) / l).astype(BF16))
    y = _dot(jnp.concatenate(outs, axis=-1), wo_ref[...])
    o = _layer_norm(ALPHA * hf_ref[...] + y, g_ref[...], b_ref[...])
    of_ref[...] = o
    ob_ref[...] = o.astype(BF16)


def _xattn(hb, hf, wq, kv, wo, g, b, batch, seq, mem_len, tm=512):
    n, d = hf.shape
    ns = seq // tm
    xw = N_X * X_HEAD
    row = pl.BlockSpec((tm, d), lambda bb, s: (bb * ns + s, 0))
    par = pl.BlockSpec((1, d), lambda bb, s: (0, 0))
    return pl.pallas_call(
        _xattn_kernel, grid=(batch, ns),
        in_specs=[row, row, pl.BlockSpec((d, xw), lambda bb, s: (0, 0)),
                  pl.BlockSpec((mem_len, xw), lambda bb, s: (bb, 0)),
                  pl.BlockSpec((mem_len, xw), lambda bb, s: (bb, 1)),
                  pl.BlockSpec((xw, d), lambda bb, s: (0, 0)), par, par],
        out_specs=[row, row],
        out_shape=[jax.ShapeDtypeStruct((n, d), F32), jax.ShapeDtypeStruct((n, d), BF16)],
        compiler_params=_cparams("parallel", "parallel"), name="xattn")(
            hb, hf, wq, kv, kv, wo, g.reshape(1, d), b.reshape(1, d))


def _router_kernel(h_ref, wr_ref, rb_ref, g_ref):
    per_group = N_EXPERTS // 4
    aff = _sigmoid(_dot_hi(h_ref[...], wr_ref[...]))
    biased = aff + rb_ref[...]
    lane = lax.broadcasted_iota(jnp.int32, (1, LANES), 1).astype(F32)
    big = float(LANES)
    best = e1 = e2 = None
    for g in range(4):
        x = jnp.where((lane >= g * per_group) & (lane < (g + 1) * per_group), biased, -jnp.inf)
        m1 = jnp.max(x, axis=-1, keepdims=True)
        i1 = jnp.min(jnp.where(x == m1, lane, big), axis=-1, keepdims=True)
        x2 = jnp.where(lane == i1, -jnp.inf, x)
        m2 = jnp.max(x2, axis=-1, keepdims=True)
        i2 = jnp.min(jnp.where(x2 == m2, lane, big), axis=-1, keepdims=True)
        score = m1 + m2
        if g == 0:
            best, e1, e2 = score, i1, i2
        else:
            better = score > best
            best = jnp.where(better, score, best)
            e1 = jnp.where(better, i1, e1)
            e2 = jnp.where(better, i2, e2)
    s1 = jnp.sum(jnp.where(lane == e1, aff, 0.0), axis=-1, keepdims=True)
    s2 = jnp.sum(jnp.where(lane == e2, aff, 0.0), axis=-1, keepdims=True)
    tot = s1 + s2
    g_ref[...] = jnp.where(lane == e1, s1 / tot, 0.0) + jnp.where(lane == e2, s2 / tot, 0.0)


def _router(hf, wr, rb, tm=512):
    n, d = hf.shape
    return pl.pallas_call(
        _router_kernel, grid=(n // tm,),
        in_specs=[pl.BlockSpec((tm, d), lambda i: (i, 0)),
                  pl.BlockSpec((d, LANES), lambda i: (0, 0)),
                  pl.BlockSpec((1, LANES), lambda i: (0, 0))],
        out_specs=pl.BlockSpec((tm, LANES), lambda i: (i, 0)),
        out_shape=jax.ShapeDtypeStruct((n, LANES), F32),
        compiler_params=_cparams("parallel"), name="router")(hf, wr, rb)


def _moe_up_kernel(x_ref, wg_ref, wu_ref, g_ref, o_ref):
    e = pl.program_id(1)
    x = x_ref[...]
    a = _dot(x, wg_ref[...])
    u = _dot(x, wu_ref[...])
    lane = lax.broadcasted_iota(jnp.int32, (1, LANES), 1)
    ge = jnp.sum(jnp.where(lane == e, g_ref[...], 0.0), axis=-1, keepdims=True)
    o_ref[...] = (a * _sigmoid(a) * u * ge).astype(BF16)


def _moe_up(hb, wg, wu, gates, tm=1024):
    n, d = hb.shape
    ne, _, de = wg.shape
    wspec = pl.BlockSpec((None, d, de), lambda i, e: (e, 0, 0))
    return pl.pallas_call(
        _moe_up_kernel, grid=(n // tm, ne),
        in_specs=[pl.BlockSpec((tm, d), lambda i, e: (i, 0)), wspec, wspec,
                  pl.BlockSpec((tm, LANES), lambda i, e: (i, 0))],
        out_specs=pl.BlockSpec((tm, de), lambda i, e: (i, e)),
        out_shape=jax.ShapeDtypeStruct((n, ne * de), BF16),
        compiler_params=_cparams("parallel", "arbitrary"), name="moe_up")(hb, wg, wu, gates)


def _prep_params(p):
    L = p["w_in"].shape[0]
    w_in = p["w_in"]
    zeros = lambda w: jnp.zeros((L, D_MODEL, w), F32)
    o_mla = 2 * D_SG
    o_kr = o_mla + MLA_Q_RANK + MLA_KV_RANK
    o_gla = o_kr + MLA_ROPE
    gla_main = 2 * N_GLA * GLA_DK + 2 * N_GLA * GLA_DV
    o_ml = o_gla + gla_main + GLA_RANK
    ml_main = 2 * N_ML * ML_DK + 2 * N_ML * ML_DV
    o_gate = o_ml + ml_main + 2 * N_ML
    half = MLA_ROPE // 2
    kr = w_in[..., o_kr:o_gla]
    pad_r = LANES - MLA_ROPE
    w_mla = jnp.concatenate(
        [w_in[..., o_mla:o_kr], kr, zeros(pad_r), kr[..., half:], kr[..., :half], zeros(pad_r)], axis=-1)
    w_gla = jnp.concatenate([w_in[..., o_gla:o_gla + gla_main + GLA_RANK], zeros(LANES - GLA_RANK)], axis=-1)
    w_ml = jnp.concatenate([w_in[..., o_ml:o_gate], zeros(LANES - 2 * N_ML)], axis=-1)

    wq = p["mla_w_uq"].reshape(L, MLA_Q_RANK, N_MLA, MLA_NOPE + MLA_ROPE)
    rq = wq[..., MLA_NOPE:]
    zq = jnp.zeros((L, MLA_Q_RANK, N_MLA, pad_r), F32)
    wuq = jnp.concatenate([wq[..., :MLA_NOPE], rq, zq, rq[..., half:], rq[..., :half], zq], axis=-1)
    wkv = p["mla_w_ukv"].reshape(L, MLA_KV_RANK, N_MLA, MLA_NOPE + MLA_V)

    bf = lambda a: a.astype(BF16)
    ne = p["moe_w_down"].shape[1]
    return dict(
        w_sg=bf(w_in[..., :o_mla]), w_mla=bf(w_mla), w_gla=bf(w_gla), w_ml=bf(w_ml), w_gate=bf(w_in[..., o_gate:]),
        wuq=bf(wuq.reshape(L, MLA_Q_RANK, N_MLA * 3 * LANES)),
        wuk=bf(wkv[..., :MLA_NOPE].reshape(L, MLA_KV_RANK, N_MLA * MLA_NOPE)),
        wuv=bf(wkv[..., MLA_NOPE:].reshape(L, MLA_KV_RANK, N_MLA * MLA_V)),
        gla_wg=jnp.pad(p["gla_w_gate"], ((0, 0), (0, LANES - GLA_RANK), (0, 0))),
        w_branch=bf(p["w_branch"]), w_out=bf(p["w_out"]),
        x_w_q=bf(p["x_w_q"]), x_w_kv=bf(p["x_w_kv"]), x_w_o=bf(p["x_w_o"]),
        w_router=jnp.pad(p["w_router"], ((0, 0), (0, LANES - N_EXPERTS))),
        router_bias=jnp.pad(p["router_bias"], (0, LANES - N_EXPERTS)).reshape(1, LANES),
        moe_wg=bf(p["moe_w_gate"]), moe_wu=bf(p["moe_w_up"]),
        moe_wd=bf(p["moe_w_down"]).reshape(L, ne * D_EXPERT, D_MODEL),
    )


def _mixer(hf, hb, cos_p, sin_p, p, w, l, batch, seq):
    z_sg = _mm(hb, w["w_sg"][l], F32, name="mm_sg")
    z_mla = _mm(hb, w["w_mla"][l], F32, name="mm_mla")
    z_gla = _mm(hb, w["w_gla"][l], F32, name="mm_gla")
    z_ml = _mm(hb, w["w_ml"][l], F32, name="mm_ml")
    gates = _mm(hb, w["w_gate"][l], BF16, act="sigmoid", tm=1024, tn=1024, name="mm_gate")

    y_a = _sg(z_sg, p["sg_vnorm_g"][l], p["sg_vnorm_b"][l], p["sg_w_s"][l], p["sg_b_s"][l])
    q, k, v = _mla_prep(z_mla, cos_p, sin_p, p["mla_qnorm_g"][l], p["mla_kvnorm_g"][l],
                        w["wuq"][l], w["wuk"][l], w["wuv"][l])
    y_b = _mla_attn(q, k, v, batch, seq)
    y_c = _gla(z_gla, w["gla_wg"][l], p["gla_b_gate"][l], p["gla_norm_g"][l], batch, seq)
    if_cols = z_ml[:, -LANES:-LANES + 2 * N_ML]
    ift = if_cols.reshape(batch, seq, 2 * N_ML).transpose(0, 2, 1)
    y_d = _mlstm(z_ml, ift, p["ml_conv_w"][l], p["ml_conv_b"][l], p["ml_gate_b"][l], p["ml_norm_g"][l],
                 batch, seq)
    merged = _merge(y_a, y_b, y_c, y_d, gates, w["w_branch"][l])
    return _mm_res_ln(merged, w["w_out"][l], hf, p["ln1_g"][l], p["ln1_b"][l], name="out_ln1")


def _forward(p):
    x = p["x"]
    batch, seq, d = x.shape
    n = batch * seq
    mem = p["mem"]
    mem_len = mem.shape[1]
    w = _prep_params(p)
    posb = jnp.broadcast_to(p["positions"].reshape(n, 1).astype(F32), (n, LANES))
    cos_p, sin_p = _rope_tables(posb)
    memb = mem.reshape(batch * mem_len, d).astype(BF16)
    hf, hb = _ln(x.reshape(n, d), p["ln_in_g"], p["ln_in_b"])
    for l in range(p["w_in"].shape[0]):
        hf, hb = _mixer(hf, hb, cos_p, sin_p, p, w, l, batch, seq)
        kv = _mm(memb, w["x_w_kv"][l], BF16, tm=512, tn=1024, name="mm_xkv")
        hf, hb = _xattn(hb, hf, w["x_w_q"][l], kv, w["x_w_o"][l], p["ln2_g"][l], p["ln2_b"][l],
                        batch, seq, mem_len)
        gates = _router(hf, w["w_router"], w["router_bias"])
        hid = _moe_up(hb, w["moe_wg"][l], w["moe_wu"][l], gates)
        hf, hb = _mm_res_ln(hid, w["moe_wd"][l], hf, p["ln3_g"][l], p["ln3_b"][l], tm=512, tk=1024,
                            name="moe_down_ln3")
    return hf.reshape(batch, seq, d)


def kernel(x, mem, positions, ln_in_g, ln_in_b, w_in, sg_vnorm_g, sg_vnorm_b, sg_w_s, sg_b_s, mla_qnorm_g, mla_kvnorm_g, mla_w_uq, mla_w_ukv, gla_w_gate, gla_b_gate, gla_norm_g, ml_conv_w, ml_conv_b, ml_gate_b, ml_norm_g, w_branch, w_out, ln1_g, ln1_b, x_w_q, x_w_kv, x_w_o, ln2_g, ln2_b, w_router, router_bias, moe_w_gate, moe_w_up, moe_w_down, ln3_g, ln3_b):
    return _forward(dict(
        x=x, mem=mem, positions=positions, ln_in_g=ln_in_g, ln_in_b=ln_in_b, w_in=w_in,
        sg_vnorm_g=sg_vnorm_g, sg_vnorm_b=sg_vnorm_b, sg_w_s=sg_w_s, sg_b_s=sg_b_s,
        mla_qnorm_g=mla_qnorm_g, mla_kvnorm_g=mla_kvnorm_g, mla_w_uq=mla_w_uq, mla_w_ukv=mla_w_ukv,
        gla_w_gate=gla_w_gate, gla_b_gate=gla_b_gate, gla_norm_g=gla_norm_g,
        ml_conv_w=ml_conv_w, ml_conv_b=ml_conv_b, ml_gate_b=ml_gate_b, ml_norm_g=ml_norm_g,
        w_branch=w_branch, w_out=w_out, ln1_g=ln1_g, ln1_b=ln1_b,
        x_w_q=x_w_q, x_w_kv=x_w_kv, x_w_o=x_w_o, ln2_g=ln2_g, ln2_b=ln2_b,
        w_router=w_router, router_bias=router_bias,
        moe_w_gate=moe_w_gate, moe_w_up=moe_w_up, moe_w_down=moe_w_down, ln3_g=ln3_g, ln3_b=ln3_b))
```

```python
import functools
import math

import jax
import jax.numpy as jnp
from jax import lax
from jax.experimental import pallas as pl
from jax.experimental.pallas import tpu as pltpu

F32 = jnp.float32
BF16 = jnp.bfloat16

D_MODEL = 2048
DEPTH = 4
EPS = 1e-5
ALPHA = (2.0 * DEPTH) ** 0.25

SG_CHUNK = 128
N_SG = 4
D_SG = 512
N_MLA = 4
MLA_Q_RANK = 384
MLA_KV_RANK = 256
MLA_NOPE = 128
MLA_ROPE = 64
MLA_V = 128
MLA_QK = 256
ROPE_BASE = 10000.0
N_GLA = 4
GLA_DK = 64
GLA_DV = 128
GLA_RANK = 16
GLA_TAU = 16.0
GLA_CHUNK = 64
N_ML = 4
ML_DK = 64
ML_DV = 128
ML_CHUNK = 128
N_X = 4
X_HEAD = 128
N_EXPERTS = 16
D_EXPERT = 512

LANES = 128
VMEM_LIMIT = 48 * 1024 * 1024

_NT = (((1,), (1,)), ((), ()))
_TN = (((0,), (0,)), ((), ()))


def _cparams(*sem):
    return pltpu.CompilerParams(dimension_semantics=sem, vmem_limit_bytes=VMEM_LIMIT)


def _dot(a, b):
    return jnp.dot(a, b, preferred_element_type=F32)


def _dotg(a, b, dims):
    return lax.dot_general(a, b, dims, preferred_element_type=F32)


def _split3(a):
    a1 = a.astype(BF16)
    r1 = a - a1.astype(F32)
    a2 = r1.astype(BF16)
    a3 = (r1 - a2.astype(F32)).astype(BF16)
    return a1, a2, a3


def _dot_exact_rhs(a, ones_bf16):
    a1, a2, a3 = _split3(a)
    return _dot(a1, ones_bf16) + _dot(a2, ones_bf16) + _dot(a3, ones_bf16)


def _dot_exact_lhs(ones_bf16, a):
    a1, a2, a3 = _split3(a)
    return _dot(ones_bf16, a1) + _dot(ones_bf16, a2) + _dot(ones_bf16, a3)


def _dot_hi(a, b):
    a1 = a.astype(BF16)
    a2 = (a - a1.astype(F32)).astype(BF16)
    b1 = b.astype(BF16)
    b2 = (b - b1.astype(F32)).astype(BF16)
    return _dot(a1, b1) + _dot(a2, b1) + _dot(a1, b2)


def _sigmoid(x):
    return 1.0 / (1.0 + jnp.exp(-x))


def _log_sigmoid(x):
    return jnp.minimum(x, 0.0) - jnp.log(1.0 + jnp.exp(-jnp.abs(x)))


def _layer_norm(t, g, b):
    mu = jnp.mean(t, axis=-1, keepdims=True)
    c = t - mu
    var = jnp.mean(c * c, axis=-1, keepdims=True)
    return c * lax.rsqrt(var + EPS) * g + b


def _rms_norm(t, g):
    return t * lax.rsqrt(jnp.mean(t * t, axis=-1, keepdims=True) + EPS) * g


def _ln_kernel(x_ref, g_ref, b_ref, of_ref, ob_ref):
    y = _layer_norm(x_ref[...], g_ref[...], b_ref[...])
    of_ref[...] = y
    ob_ref[...] = y.astype(BF16)


def _ln(x, g, b, tm=256):
    n, d = x.shape
    row = pl.BlockSpec((tm, d), lambda i: (i, 0))
    par = pl.BlockSpec((1, d), lambda i: (0, 0))
    return pl.pallas_call(
        _ln_kernel, grid=(n // tm,), in_specs=[row, par, par], out_specs=[row, row],
        out_shape=[jax.ShapeDtypeStruct((n, d), F32), jax.ShapeDtypeStruct((n, d), BF16)],
        compiler_params=_cparams("parallel"), name="ln_in")(x, g.reshape(1, d), b.reshape(1, d))


def _mm_kernel(x_ref, w_ref, o_ref, *, act):
    acc = _dot(x_ref[...], w_ref[...])
    if act == "sigmoid":
        acc = _sigmoid(acc)
    o_ref[...] = acc.astype(o_ref.dtype)


def _mm(x, w, out_dtype, act=None, tm=512, tn=None, name="mm"):
    n, k = x.shape
    m = w.shape[1]
    tn = m if tn is None else tn
    return pl.pallas_call(
        functools.partial(_mm_kernel, act=act), grid=(n // tm, m // tn),
        in_specs=[pl.BlockSpec((tm, k), lambda i, j: (i, 0)),
                  pl.BlockSpec((k, tn), lambda i, j: (0, j))],
        out_specs=pl.BlockSpec((tm, tn), lambda i, j: (i, j)),
        out_shape=jax.ShapeDtypeStruct((n, m), out_dtype),
        compiler_params=_cparams("parallel", "parallel"), name=name)(x, w)


def _mm_res_ln_kernel(x_ref, w_ref, h_ref, g_ref, b_ref, of_ref, ob_ref, *acc, nk):
    def finish(y):
        t = ALPHA * h_ref[...] + y
        o = _layer_norm(t, g_ref[...], b_ref[...])
        of_ref[...] = o
        ob_ref[...] = o.astype(BF16)

    if nk == 1:
        finish(_dot(x_ref[...], w_ref[...]))
        return
    acc_ref, = acc
    kk = pl.program_id(1)

    @pl.when(kk == 0)
    def _():
        acc_ref[...] = jnp.zeros_like(acc_ref)

    acc_ref[...] += _dot(x_ref[...], w_ref[...])

    @pl.when(kk == nk - 1)
    def _():
        finish(acc_ref[...])


def _mm_res_ln(x, w, h, g, b, tm=256, tk=None, name="mm_res_ln"):
    n, k = x.shape
    d = w.shape[1]
    tk = k if tk is None else tk
    nk = k // tk
    row = pl.BlockSpec((tm, d), lambda i, j: (i, 0))
    par = pl.BlockSpec((1, d), lambda i, j: (0, 0))
    scratch = [] if nk == 1 else [pltpu.VMEM((tm, d), F32)]
    return pl.pallas_call(
        functools.partial(_mm_res_ln_kernel, nk=nk), grid=(n // tm, nk),
        in_specs=[pl.BlockSpec((tm, tk), lambda i, j: (i, j)),
                  pl.BlockSpec((tk, d), lambda i, j: (j, 0)), row, par, par],
        out_specs=[row, row],
        out_shape=[jax.ShapeDtypeStruct((n, d), F32), jax.ShapeDtypeStruct((n, d), BF16)],
        scratch_shapes=scratch,
        compiler_params=_cparams("parallel", "arbitrary"), name=name)(
            x, w, h, g.reshape(1, d), b.reshape(1, d))


def _sg_kernel(z_ref, vg_ref, vb_ref, ws_ref, bst_ref, o_ref, *, nchunk):
    z = z_ref[...]
    z = 0.5 * z * (1.0 + jnp.tanh(math.sqrt(2.0 / math.pi) * (z + 0.044715 * (z * z * z))))
    u = z[:, :D_SG]
    vn = _layer_norm(z[:, D_SG:], vg_ref[...], vb_ref[...]).astype(BF16)
    r = lax.broadcasted_iota(jnp.int32, (SG_CHUNK, SG_CHUNK), 0)
    c = lax.broadcasted_iota(jnp.int32, (SG_CHUNK, SG_CHUNK), 1)
    causal = c <= r
    gw = SG_CHUNK
    for g in range(N_SG):
        w = jnp.where(causal, ws_ref[g], 0.0).astype(BF16)
        bias = bst_ref[:, g:g + 1]
        for ci in range(nchunk):
            rs = slice(ci * SG_CHUNK, (ci + 1) * SG_CHUNK)
            cs = slice(g * gw, (g + 1) * gw)
            mixed = _dot(w, vn[rs, cs]) + bias
            o_ref[rs, cs] = (u[rs, cs] * mixed).astype(BF16)


def _sg(z, vg, vb, ws, bs, tm=512):
    n = z.shape[0]
    return pl.pallas_call(
        functools.partial(_sg_kernel, nchunk=tm // SG_CHUNK), grid=(n // tm,),
        in_specs=[pl.BlockSpec((tm, 2 * D_SG), lambda i: (i, 0)),
                  pl.BlockSpec((1, D_SG), lambda i: (0, 0)),
                  pl.BlockSpec((1, D_SG), lambda i: (0, 0)),
                  pl.BlockSpec((N_SG, SG_CHUNK, SG_CHUNK), lambda i: (0, 0, 0)),
                  pl.BlockSpec((SG_CHUNK, N_SG), lambda i: (0, 0))],
        out_specs=pl.BlockSpec((tm, D_SG), lambda i: (i, 0)),
        out_shape=jax.ShapeDtypeStruct((n, D_SG), BF16),
        compiler_params=_cparams("parallel"), name="sg")(
            z, vg.reshape(1, D_SG), vb.reshape(1, D_SG), ws, bs.T)


def _rope_kernel(pos_ref, cos_ref, sin_ref):
    half = MLA_ROPE // 2
    lane = lax.broadcasted_iota(jnp.int32, (1, LANES), 1)
    idx = jnp.bitwise_and(lane, half - 1).astype(F32)
    freq = jnp.exp(idx * (-math.log(ROPE_BASE) / half))
    ang = pos_ref[...] * freq
    c = jnp.cos(ang)
    s = jnp.sin(ang)
    cos_ref[...] = jnp.where(lane < MLA_ROPE, c, 0.0)
    sin_ref[...] = jnp.where(lane < half, -s, jnp.where(lane < MLA_ROPE, s, 0.0))


def _rope_tables(posb, tm=512):
    n = posb.shape[0]
    row = pl.BlockSpec((tm, LANES), lambda i: (i, 0))
    return pl.pallas_call(
        _rope_kernel, grid=(n // tm,), in_specs=[row], out_specs=[row, row],
        out_shape=[jax.ShapeDtypeStruct((n, LANES), F32)] * 2,
        compiler_params=_cparams("parallel"), name="rope_tables")(posb)


def _mla_prep_kernel(z_ref, cos_ref, sin_ref, qg_ref, kvg_ref, wuq_ref, wuk_ref, wuv_ref,
                     q_ref, k_ref, v_ref):
    z = z_ref[...]
    cq = _rms_norm(z[:, :MLA_Q_RANK], qg_ref[...]).astype(BF16)
    o1 = MLA_Q_RANK + MLA_KV_RANK
    ckv = _rms_norm(z[:, MLA_Q_RANK:o1], kvg_ref[...]).astype(BF16)
    cos_p = cos_ref[...]
    sin_p = sin_ref[...]
    k_tail = (z[:, o1:o1 + LANES] * cos_p + z[:, o1 + LANES:o1 + 2 * LANES] * sin_p).astype(BF16)
    qa = _dot(cq, wuq_ref[...])
    kn = _dot(ckv, wuk_ref[...])
    v_ref[...] = _dot(ckv, wuv_ref[...]).astype(BF16)
    scale = (MLA_NOPE + MLA_ROPE) ** -0.5
    for h in range(N_MLA):
        b0 = h * 3 * LANES
        q_tail = qa[:, b0 + LANES:b0 + 2 * LANES] * cos_p + qa[:, b0 + 2 * LANES:b0 + 3 * LANES] * sin_p
        q_ref[:, h * MLA_QK:h * MLA_QK + LANES] = (qa[:, b0:b0 + LANES] * scale).astype(BF16)
        q_ref[:, h * MLA_QK + LANES:(h + 1) * MLA_QK] = (q_tail * scale).astype(BF16)
        k_ref[:, h * MLA_QK:h * MLA_QK + LANES] = kn[:, h * LANES:(h + 1) * LANES].astype(BF16)
        k_ref[:, h * MLA_QK + LANES:(h + 1) * MLA_QK] = k_tail


def _mla_prep(z, cos_p, sin_p, qg, kvg, wuq, wuk, wuv, tm=512):
    n, zw = z.shape
    row = lambda w: pl.BlockSpec((tm, w), lambda i: (i, 0))
    full = lambda a: pl.BlockSpec(a.shape, lambda i: (0,) * a.ndim)
    qg = qg.reshape(1, -1)
    kvg = kvg.reshape(1, -1)
    return pl.pallas_call(
        _mla_prep_kernel, grid=(n // tm,),
        in_specs=[row(zw), row(LANES), row(LANES), full(qg), full(kvg), full(wuq), full(wuk), full(wuv)],
        out_specs=[row(N_MLA * MLA_QK), row(N_MLA * MLA_QK), row(N_MLA * MLA_V)],
        out_shape=[jax.ShapeDtypeStruct((n, N_MLA * MLA_QK), BF16),
                   jax.ShapeDtypeStruct((n, N_MLA * MLA_QK), BF16),
                   jax.ShapeDtypeStruct((n, N_MLA * MLA_V), BF16)],
        compiler_params=_cparams("parallel"), name="mla_prep")(z, cos_p, sin_p, qg, kvg, wuq, wuk, wuv)


def _mla_attn_kernel(q_ref, k_ref, v_ref, o_ref, *, tq, tk):
    i = pl.program_id(2)
    q = q_ref[...]
    qpos = i * tq + lax.broadcasted_iota(jnp.int32, (tq, tk), 0)
    kofs = lax.broadcasted_iota(jnp.int32, (tq, tk), 1)

    def body(j, carry):
        m, l, acc = carry
        start = pl.multiple_of(j * tk, tk)
        s = _dotg(q, k_ref[pl.ds(start, tk), :], _NT)
        s = jnp.where(kofs + j * tk <= qpos, s, -jnp.inf)
        m_new = jnp.maximum(m, jnp.max(s, axis=-1, keepdims=True))
        p = jnp.exp(s - m_new)
        a = jnp.exp(m - m_new)
        l = a * l + jnp.sum(p, axis=-1, keepdims=True)
        acc = a * acc + _dot(p.astype(BF16), v_ref[pl.ds(start, tk), :])
        return m_new, l, acc

    init = (jnp.full((tq, 1), -jnp.inf, F32), jnp.zeros((tq, 1), F32), jnp.zeros((tq, MLA_V), F32))
    nblk = (i * tq + tq + tk - 1) // tk
    _, l, acc = lax.fori_loop(0, nblk, body, init)
    o_ref[...] = (acc / l).astype(BF16)


def _mla_attn(q, k, v, batch, seq, tq=512, tk=512):
    n = q.shape[0]
    nq = seq // tq
    return pl.pallas_call(
        functools.partial(_mla_attn_kernel, tq=tq, tk=tk), grid=(batch, N_MLA, nq),
        in_specs=[pl.BlockSpec((tq, MLA_QK), lambda b, h, i: (b * nq + i, h)),
                  pl.BlockSpec((seq, MLA_QK), lambda b, h, i: (b, h)),
                  pl.BlockSpec((seq, MLA_V), lambda b, h, i: (b, h))],
        out_specs=pl.BlockSpec((tq, MLA_V), lambda b, h, i: (b * nq + i, h)),
        out_shape=jax.ShapeDtypeStruct((n, N_MLA * MLA_V), BF16),
        compiler_params=_cparams("parallel", "parallel", "arbitrary"), name="mla_attn")(q, k, v)


def _gla_kernel(q_ref, k_ref, v_ref, og_ref, lr_ref, wg_ref, bg_ref, ng_ref, o_ref, st_ref, *, nchunk):
    L = GLA_CHUNK
    qkw = N_GLA * GLA_DK

    @pl.when(pl.program_id(1) == 0)
    def _():
        st_ref[...] = jnp.zeros_like(st_ref)

    logits = _dot_hi(lr_ref[...], wg_ref[...]) + bg_ref[...]
    log_a = _log_sigmoid(logits) * (1.0 / GLA_TAU)
    lane = lax.broadcasted_iota(jnp.int32, (1, qkw), 1)
    masks = [((lane >= h * GLA_DK) & (lane < (h + 1) * GLA_DK)).astype(F32) for h in range(N_GLA)]
    r = lax.broadcasted_iota(jnp.int32, (L, L), 0)
    c = lax.broadcasted_iota(jnp.int32, (L, L), 1)
    causal = c <= r
    tril = jnp.where(causal, 1.0, 0.0).astype(BF16)
    ng = ng_ref[...]
    st = st_ref[...]
    for ci in range(nchunk):
        rs = slice(ci * L, (ci + 1) * L)
        b = _dot_exact_lhs(tril, log_a[rs])
        b_last = b[L - 1:L, :]
        q = q_ref[rs, :] * (GLA_DK ** -0.5)
        k = k_ref[rs, :]
        qt = q * jnp.exp(b)
        kt = (k * jnp.exp(-b)).astype(BF16)
        kd = (k * jnp.exp(b_last - b)).astype(BF16)
        qstack = jnp.concatenate([qt * masks[h] for h in range(N_GLA)], axis=0).astype(BF16)
        att = _dotg(qstack, kt, _NT)
        inter = _dotg(qstack, st.astype(BF16), _NT)
        vb = v_ref[rs, :].astype(BF16)
        for h in range(N_GLA):
            hs = slice(h * L, (h + 1) * L)
            vs = slice(h * GLA_DV, (h + 1) * GLA_DV)
            a_h = jnp.where(causal, att[hs], 0.0).astype(BF16)
            o_h = _rms_norm(_dot(a_h, vb[:, vs]) + inter[hs], ng)
            g = og_ref[rs, vs]
            o_ref[rs, vs] = (o_h * (g * _sigmoid(g))).astype(BF16)
        upd = _dotg(vb, kd, _TN)
        new = st * jnp.exp(b_last)
        for h in range(N_GLA):
            new = new + upd[h * GLA_DV:(h + 1) * GLA_DV] * masks[h]
        st = new
    st_ref[...] = st


def _gla(z, wg, bg, ng, batch, seq, tm=256):
    n = z.shape[0]
    ns = seq // tm
    qkw = N_GLA * GLA_DK
    vw = N_GLA * GLA_DV
    col = lambda w, j: pl.BlockSpec((tm, w), lambda b, s: (b * ns + s, j))
    full = lambda a: pl.BlockSpec(a.shape, lambda b, s: (0,) * a.ndim)
    bg = bg.reshape(1, qkw)
    ng = ng.reshape(1, GLA_DV)
    return pl.pallas_call(
        functools.partial(_gla_kernel, nchunk=tm // GLA_CHUNK), grid=(batch, ns),
        in_specs=[col(qkw, 0), col(qkw, 1), col(vw, 1), col(vw, 2), col(LANES, (2 * qkw + 2 * vw) // LANES),
                  full(wg), full(bg), full(ng)],
        out_specs=pl.BlockSpec((tm, vw), lambda b, s: (b * ns + s, 0)),
        out_shape=jax.ShapeDtypeStruct((n, vw), BF16),
        scratch_shapes=[pltpu.VMEM((GLA_DV, qkw), F32)],
        compiler_params=_cparams("parallel", "arbitrary"), name="gla")(z, z, z, z, z, wg, bg, ng)


def _mlstm_kernel(qk_ref, v_ref, op_ref, if_ref, ift_ref, cw_ref, cb_ref, gb_ref, gbt_ref, ng_ref,
                  o_ref, ct_ref, n_ref, m_ref, tail_ref, *, nchunk):
    L = ML_CHUNK
    qkw = N_ML * ML_DK
    tm = nchunk * L

    @pl.when(pl.program_id(1) == 0)
    def _():
        ct_ref[...] = jnp.zeros_like(ct_ref)
        n_ref[...] = jnp.zeros_like(n_ref)
        m_ref[...] = jnp.zeros_like(m_ref)
        tail_ref[...] = jnp.zeros_like(tail_ref)

    x = qk_ref[...]
    tail = tail_ref[...]
    row8 = lax.broadcasted_iota(jnp.int32, (8, 2 * qkw), 0)
    acc = x * cw_ref[3:4, :] + cb_ref[...]
    for j in range(1, 4):
        rx = pltpu.roll(x, j, 0)
        fix = jnp.where(row8 < j, pltpu.roll(tail, j, 0), rx[0:8])
        acc = acc + jnp.concatenate([fix, rx[8:]], axis=0) * cw_ref[3 - j:4 - j, :]
    tail_ref[...] = x[tm - 8:tm]
    y = acc * _sigmoid(acc)
    q = y[:, :qkw] * (ML_DK ** -0.5)
    k = y[:, qkw:]

    gates = if_ref[...] + gb_ref[...]
    fc = _log_sigmoid(gates)
    gt = ift_ref[0] + gbt_ref[...]
    fct = _log_sigmoid(gt)

    lane = lax.broadcasted_iota(jnp.int32, (1, qkw), 1)
    masks = [((lane >= h * ML_DK) & (lane < (h + 1) * ML_DK)).astype(F32) for h in range(N_ML)]
    lane_m = lax.broadcasted_iota(jnp.int32, (1, LANES), 1)
    r = lax.broadcasted_iota(jnp.int32, (L, L), 0)
    c = lax.broadcasted_iota(jnp.int32, (L, L), 1)
    causal = c <= r
    tril = jnp.where(causal, 1.0, 0.0).astype(BF16)
    triu = jnp.where(r <= c, 1.0, 0.0).astype(BF16)
    ng = ng_ref[...]
    ct = ct_ref[...]
    nrow = n_ref[...]
    mrow = m_ref[...]

    for ci in range(nchunk):
        rs = slice(ci * L, (ci + 1) * L)
        bcol = _dot_exact_lhs(tril, fc[rs])
        brow = _dot_exact_rhs(fct[:, rs], triu)
        qc = q[rs]
        kc = k[rs]
        kcb = kc.astype(BF16)
        ctb = ct.astype(BF16)
        decay_row = jnp.zeros((1, qkw), F32)
        ct_upd = jnp.zeros_like(ct)
        n_upd = jnp.zeros_like(nrow)
        m_next = jnp.zeros_like(mrow)
        for h in range(N_ML):
            vs = slice(h * ML_DV, (h + 1) * ML_DV)
            bc = bcol[:, N_ML + h:N_ML + h + 1]
            br = brow[N_ML + h:N_ML + h + 1, :]
            ir = gt[h:h + 1, rs]
            ic = gates[rs, h:h + 1]
            m_h = mrow[:, h:h + 1]
            log_inter = bc + m_h
            log_d = jnp.where(causal, bc - br + ir, -jnp.inf)
            m_t = jnp.maximum(log_inter, jnp.max(log_d, axis=-1, keepdims=True))
            w_inter = jnp.exp(log_inter - m_t)
            qm = qc * masks[h]
            qmb = qm.astype(BF16)
            s_mat = _dotg(qmb, kcb, _NT) * jnp.exp(log_d - m_t)
            vh = v_ref[rs, vs].astype(BF16)
            num = _dot(s_mat.astype(BF16), vh) + w_inter * _dotg(qmb, ctb, _NT)
            den = jnp.sum(s_mat, axis=-1, keepdims=True) + \
                w_inter * jnp.sum(qm * nrow, axis=-1, keepdims=True)
            hh = num / jnp.maximum(jnp.abs(den), jnp.exp(-m_t))
            hh = _rms_norm(hh, ng)
            o_ref[rs, vs] = (hh * _sigmoid(op_ref[rs, vs])).astype(BF16)
            b_last = bc[L - 1:L, :]
            log_w = b_last - bc + ic
            m_new = jnp.maximum(b_last + m_h, jnp.max(log_w, axis=0, keepdims=True))
            decay = jnp.exp(b_last + m_h - m_new)
            kw = kc * masks[h] * jnp.exp(log_w - m_new)
            ct_upd = ct_upd + _dotg(vh, kw.astype(BF16), _TN)
            n_upd = n_upd + jnp.sum(kw, axis=0, keepdims=True)
            decay_row = decay_row + decay * masks[h]
            m_next = m_next + jnp.where(lane_m == h, m_new, 0.0)
        ct = ct * decay_row + ct_upd
        nrow = nrow * decay_row + n_upd
        mrow = m_next

    ct_ref[...] = ct
    n_ref[...] = nrow
    m_ref[...] = mrow


def _mlstm(z, ift, cw, cb, gb, ng, batch, seq, tm=512):
    n = z.shape[0]
    ns = seq // tm
    qkw = N_ML * ML_DK
    vw = N_ML * ML_DV
    col = lambda w, j: pl.BlockSpec((tm, w), lambda b, s: (b * ns + s, j))
    full = lambda a: pl.BlockSpec(a.shape, lambda b, s: (0,) * a.ndim)
    cb = cb.reshape(1, 2 * qkw)
    gbp = jnp.pad(gb, (0, LANES - 2 * N_ML)).reshape(1, LANES)
    gbt = gb.reshape(2 * N_ML, 1)
    ng = ng.reshape(1, ML_DV)
    return pl.pallas_call(
        functools.partial(_mlstm_kernel, nchunk=tm // ML_CHUNK), grid=(batch, ns),
        in_specs=[col(2 * qkw, 0), col(vw, 1), col(vw, 2), col(LANES, (2 * qkw + 2 * vw) // LANES),
                  pl.BlockSpec((1, 2 * N_ML, tm), lambda b, s: (b, 0, s)),
                  full(cw), full(cb), full(gbp), full(gbt), full(ng)],
        out_specs=pl.BlockSpec((tm, vw), lambda b, s: (b * ns + s, 0)),
        out_shape=jax.ShapeDtypeStruct((n, vw), BF16),
        scratch_shapes=[pltpu.VMEM((ML_DV, qkw), F32), pltpu.VMEM((1, qkw), F32),
                        pltpu.VMEM((1, LANES), F32), pltpu.VMEM((8, 2 * qkw), F32)],
        compiler_params=_cparams("parallel", "arbitrary"), name="mlstm")(
            z, z, z, z, ift, cw, cb, gbp, gbt, ng)


def _merge_kernel(ya_ref, yb_ref, yc_ref, yd_ref, g_ref, wb_ref, o_ref):
    acc = None
    for i, y_ref in enumerate((ya_ref, yb_ref, yc_ref, yd_ref)):
        p = _dot(y_ref[...], wb_ref[i])
        t = g_ref[:, i * D_MODEL:(i + 1) * D_MODEL].astype(F32) * p
        acc = t if acc is None else acc + t
    o_ref[...] = acc.astype(BF16)


def _merge(ya, yb, yc, yd, gates, wb, tm=256):
    n, bw = ya.shape
    row = pl.BlockSpec((tm, bw), lambda i: (i, 0))
    return pl.pallas_call(
        _merge_kernel, grid=(n // tm,),
        in_specs=[row, row, row, row, pl.BlockSpec((tm, 4 * D_MODEL), lambda i: (i, 0)),
                  pl.BlockSpec(wb.shape, lambda i: (0, 0, 0))],
        out_specs=pl.BlockSpec((tm, D_MODEL), lambda i: (i, 0)),
        out_shape=jax.ShapeDtypeStruct((n, D_MODEL), BF16),
        compiler_params=_cparams("parallel"), name="merge")(ya, yb, yc, yd, gates, wb)


def _xattn_kernel(hb_ref, hf_ref, wq_ref, k_ref, v_ref, wo_ref, g_ref, b_ref, of_ref, ob_ref):
    q = (_dot(hb_ref[...], wq_ref[...]) * (X_HEAD ** -0.5)).astype(BF16)
    outs = []
    for h in range(N_X):
        hs = slice(h * X_HEAD, (h + 1) * X_HEAD)
        s = _dotg(q[:, hs], k_ref[:, hs], _NT)
        p = jnp.exp(s - jnp.max(s, axis=-1, keepdims=True))
        l = jnp.sum(p, axis=-1, keepdims=True)
        outs.append((_dot(p.astype(BF16), v_ref[:, hs]) / l).astype(BF16))
    y = _dot(jnp.concatenate(outs, axis=-1), wo_ref[...])
    o = _layer_norm(ALPHA * hf_ref[...] + y, g_ref[...], b_ref[...])
    of_ref[...] = o
    ob_ref[...] = o.astype(BF16)


def _xattn(hb, hf, wq, kv, wo, g, b, batch, seq, mem_len, tm=512):
    n, d = hf.shape
    ns = seq // tm
    xw = N_X * X_HEAD
    row = pl.BlockSpec((tm, d), lambda bb, s: (bb * ns + s, 0))
    par = pl.BlockSpec((1, d), lambda bb, s: (0, 0))
    return pl.pallas_call(
        _xattn_kernel, grid=(batch, ns),
        in_specs=[row, row, pl.BlockSpec((d, xw), lambda bb, s: (0, 0)),
                  pl.BlockSpec((mem_len, xw), lambda bb, s: (bb, 0)),
                  pl.BlockSpec((mem_len, xw), lambda bb, s: (bb, 1)),
                  pl.BlockSpec((xw, d), lambda bb, s: (0, 0)), par, par],
        out_specs=[row, row],
        out_shape=[jax.ShapeDtypeStruct((n, d), F32), jax.ShapeDtypeStruct((n, d), BF16)],
        compiler_params=_cparams("parallel", "parallel"), name="xattn")(
            hb, hf, wq, kv, kv, wo, g.reshape(1, d), b.reshape(1, d))


N_GROUPS = 4
PER_GROUP = N_EXPERTS // N_GROUPS
MOE_TILE = 512
PAY_W = D_MODEL + LANES


def _route_kernel(h_ref, wrt_ref, rbc_ref, pay_ref, dest_ref, cnt_ref, carry_ref, *, region):
    tm = h_ref.shape[0]

    @pl.when(pl.program_id(0) == 0)
    def _():
        carry_ref[...] = jnp.zeros_like(carry_ref)

    h = h_ref[...]
    w = wrt_ref[...]
    w1 = w.astype(BF16)
    w2 = (w - w1.astype(F32)).astype(BF16)
    h1 = h.astype(BF16)
    h2 = (h - h1.astype(F32)).astype(BF16)
    logits = _dotg(w1, h1, _NT) + _dotg(w2, h1, _NT) + _dotg(w1, h2, _NT)
    aff = _sigmoid(logits[0:N_EXPERTS])
    biased = aff + rbc_ref[...]
    row = lax.broadcasted_iota(jnp.int32, (N_EXPERTS, 1), 0).astype(F32)
    big = float(LANES)
    best = e1 = e2 = None
    for g in range(N_GROUPS):
        x = jnp.where((row >= g * PER_GROUP) & (row < (g + 1) * PER_GROUP), biased, -jnp.inf)
        m1 = jnp.max(x, axis=0, keepdims=True)
        i1 = jnp.min(jnp.where(x == m1, row, big), axis=0, keepdims=True)
        x2 = jnp.where(row == i1, -jnp.inf, x)
        m2 = jnp.max(x2, axis=0, keepdims=True)
        i2 = jnp.min(jnp.where(x2 == m2, row, big), axis=0, keepdims=True)
        score = m1 + m2
        if g == 0:
            best, e1, e2 = score, i1, i2
        else:
            better = score > best
            best = jnp.where(better, score, best)
            e1 = jnp.where(better, i1, e1)
            e2 = jnp.where(better, i2, e2)
    s1 = jnp.sum(jnp.where(row == e1, aff, 0.0), axis=0, keepdims=True)
    s2 = jnp.sum(jnp.where(row == e2, aff, 0.0), axis=0, keepdims=True)
    tot = s1 + s2
    gates_t = jnp.where(row == e1, s1 / tot, 0.0) + jnp.where(row == e2, s2 / tot, 0.0)

    grp = jnp.zeros_like(e1)
    for g in range(1, N_GROUPS):
        grp = grp + jnp.where(e1 >= g * PER_GROUP, 1.0, 0.0)
    row8 = lax.broadcasted_iota(jnp.int32, (8, 1), 0).astype(F32)
    onehot = jnp.where(row8 == grp, 1.0, 0.0)
    r = lax.broadcasted_iota(jnp.int32, (tm, tm), 0)
    c = lax.broadcasted_iota(jnp.int32, (tm, tm), 1)
    earlier = jnp.where(r < c, 1.0, 0.0).astype(BF16)
    rank_in = _dot(onehot.astype(BF16), earlier)
    carry = carry_ref[...]
    rank = jnp.sum(onehot * (rank_in + carry[:, 0:1]), axis=0, keepdims=True)
    dest_ref[...] = (grp * float(region) + rank).astype(jnp.int32)
    carry = carry + jnp.sum(onehot, axis=1, keepdims=True)
    carry_ref[...] = carry
    cnt_ref[...] = carry.astype(jnp.int32)

    gates = jnp.concatenate([gates_t, jnp.zeros((LANES - N_EXPERTS, tm), F32)], axis=0).T
    pay_ref[:, :D_MODEL] = h
    pay_ref[:, D_MODEL:] = gates


def _route(hf, wrt, rbc, region, tm=512):
    n, d = hf.shape
    return pl.pallas_call(
        functools.partial(_route_kernel, region=region), grid=(n // tm,),
        in_specs=[pl.BlockSpec((tm, d), lambda i: (i, 0)),
                  pl.BlockSpec((LANES, d), lambda i: (0, 0)),
                  pl.BlockSpec((N_EXPERTS, 1), lambda i: (0, 0))],
        out_specs=[pl.BlockSpec((tm, PAY_W), lambda i: (i, 0)),
                   pl.BlockSpec((1, tm), lambda i: (0, i)),
                   pl.BlockSpec((8, LANES), lambda i: (0, 0))],
        out_shape=[jax.ShapeDtypeStruct((n, PAY_W), F32), jax.ShapeDtypeStruct((1, n), jnp.int32),
                   jax.ShapeDtypeStruct((8, LANES), jnp.int32)],
        scratch_shapes=[pltpu.VMEM((8, LANES), F32)],
        compiler_params=_cparams("arbitrary"), name="route")(hf, wrt, rbc)


def _scatter_kernel(dest_ref, cnt_ref, pay_ref, out_ref, buf, zbuf, sem, zsem, *, region, nsteps):
    tm = pay_ref.shape[0]
    i = pl.program_id(0)
    slot = i % 2

    def wait_rows(s):
        pltpu.make_async_copy(buf.at[s], out_ref.at[pl.ds(0, tm)], sem.at[s]).wait()

    @pl.when(i == 0)
    def _():
        zbuf[...] = jnp.zeros_like(zbuf)
        copies = []
        for g in range(N_GROUPS):
            start = pl.multiple_of(g * region + (cnt_ref[g] // MOE_TILE) * MOE_TILE, MOE_TILE)
            copies.append(pltpu.make_async_copy(zbuf, out_ref.at[pl.ds(start, MOE_TILE)], zsem))
        for cp in copies:
            cp.start()
        for cp in copies:
            cp.wait()

    @pl.when(i >= 2)
    def _():
        wait_rows(slot)

    buf[slot] = pay_ref[...]

    def issue(r, carry):
        d = dest_ref[i * tm + r]
        pltpu.make_async_copy(buf.at[slot, pl.ds(r, 1)], out_ref.at[pl.ds(d, 1)], sem.at[slot]).start()
        return carry

    lax.fori_loop(0, tm, issue, 0, unroll=8)

    @pl.when(i == nsteps - 1)
    def _():
        wait_rows(slot)
        if nsteps >= 2:
            wait_rows(1 - slot)


def _scatter(dest, cnt, pay, region, tm=256):
    n = pay.shape[0]
    nsteps = n // tm
    return pl.pallas_call(
        functools.partial(_scatter_kernel, region=region, nsteps=nsteps),
        grid_spec=pltpu.PrefetchScalarGridSpec(
            num_scalar_prefetch=2, grid=(nsteps,),
            in_specs=[pl.BlockSpec((tm, PAY_W), lambda i, d, c: (i, 0))],
            out_specs=pl.BlockSpec(memory_space=pl.ANY),
            scratch_shapes=[pltpu.VMEM((2, tm, PAY_W), F32), pltpu.VMEM((MOE_TILE, PAY_W), F32),
                            pltpu.SemaphoreType.DMA((2,)), pltpu.SemaphoreType.DMA(())]),
        out_shape=jax.ShapeDtypeStruct((N_GROUPS * region, PAY_W), F32),
        compiler_params=_cparams("arbitrary"), name="moe_scatter")(dest, cnt, pay)


def _tile_tables(cnt, region, ntiles):
    nt = (cnt + MOE_TILE - 1) // MOE_TILE
    ends = jnp.cumsum(nt)
    starts = ends - nt
    total = ends[-1]
    i = jnp.minimum(jnp.arange(ntiles, dtype=jnp.int32), total - 1)
    g = jnp.sum((i[:, None] >= ends[None, :]).astype(jnp.int32), axis=1)
    blk = g * (region // MOE_TILE) + i - starts[g]
    return g.astype(jnp.int32), blk.astype(jnp.int32), total.reshape(1).astype(jnp.int32)


def _moe_up_kernel(tg_ref, tb_ref, nt_ref, x_ref, wg_ref, wu_ref, o_ref, wgs, wus):
    e = pl.program_id(0)
    i = pl.program_id(1)
    grp = tg_ref[i]
    changed = (i == 0) | (grp != tg_ref[jnp.maximum(i - 1, 0)])

    @pl.when(changed)
    def _():
        wgs[...] = wg_ref[...].astype(BF16)
        wus[...] = wu_ref[...].astype(BF16)

    @pl.when(i < nt_ref[0])
    def _():
        x = x_ref[:, :D_MODEL].astype(BF16)
        lane = lax.broadcasted_iota(jnp.int32, (1, LANES), 1)
        ge = jnp.sum(jnp.where(lane == grp * PER_GROUP + e, x_ref[:, D_MODEL:], 0.0), axis=-1, keepdims=True)
        a = _dot(x, wgs[...])
        u = _dot(x, wus[...])
        o_ref[...] = (a * _sigmoid(a) * u * ge).astype(BF16)


def _moe_up(tg, tb, nt, xs, wg, wu, l):
    rows = xs.shape[0]
    ntiles = tg.shape[0]
    d, de = wg.shape[2], wg.shape[3]
    wspec = pl.BlockSpec((None, None, d, de), lambda e, i, tg, tb, nt: (l, tg[i] * PER_GROUP + e, 0, 0))
    return pl.pallas_call(
        _moe_up_kernel,
        grid_spec=pltpu.PrefetchScalarGridSpec(
            num_scalar_prefetch=3, grid=(PER_GROUP, ntiles),
            in_specs=[pl.BlockSpec((MOE_TILE, PAY_W), lambda e, i, tg, tb, nt: (tb[i], 0)), wspec, wspec],
            out_specs=pl.BlockSpec((MOE_TILE, de), lambda e, i, tg, tb, nt: (tb[i], e)),
            scratch_shapes=[pltpu.VMEM((d, de), BF16), pltpu.VMEM((d, de), BF16)]),
        out_shape=jax.ShapeDtypeStruct((rows, PER_GROUP * de), BF16),
        compiler_params=_cparams("arbitrary", "arbitrary"), name="moe_up")(tg, tb, nt, xs, wg, wu)


def _moe_down_kernel(tg_ref, tb_ref, nt_ref, x_ref, w_ref, o_ref, ws):
    i = pl.program_id(1)
    changed = (i == 0) | (tg_ref[i] != tg_ref[jnp.maximum(i - 1, 0)])

    @pl.when(changed)
    def _():
        ws[...] = w_ref[...].astype(BF16)

    @pl.when(i < nt_ref[0])
    def _():
        o_ref[...] = _dot(x_ref[...], ws[...])


def _moe_down(tg, tb, nt, hid, wd, l, tn=1024):
    rows, k = hid.shape
    ntiles = tg.shape[0]
    d = wd.shape[3]
    return pl.pallas_call(
        _moe_down_kernel,
        grid_spec=pltpu.PrefetchScalarGridSpec(
            num_scalar_prefetch=3, grid=(d // tn, ntiles),
            in_specs=[pl.BlockSpec((MOE_TILE, k), lambda c, i, tg, tb, nt: (tb[i], 0)),
                      pl.BlockSpec((None, None, k, tn), lambda c, i, tg, tb, nt: (l, tg[i], 0, c))],
            out_specs=pl.BlockSpec((MOE_TILE, tn), lambda c, i, tg, tb, nt: (tb[i], c)),
            scratch_shapes=[pltpu.VMEM((k, tn), BF16)]),
        out_shape=jax.ShapeDtypeStruct((rows, d), F32),
        compiler_params=_cparams("arbitrary", "arbitrary"), name="moe_down")(tg, tb, nt, hid, wd)


def _gather_ln_kernel(dest_ref, y_ref, h_ref, g_ref, b_ref, of_ref, ob_ref, buf, sem, *, nsteps):
    tm = h_ref.shape[0]
    i = pl.program_id(0)

    def issue(step, slot):
        def body(r, carry):
            d = dest_ref[step * tm + r]
            pltpu.make_async_copy(y_ref.at[pl.ds(d, 1)], buf.at[slot, pl.ds(r, 1)], sem.at[slot]).start()
            return carry
        lax.fori_loop(0, tm, body, 0, unroll=8)

    @pl.when(i == 0)
    def _():
        issue(0, 0)

    @pl.when(i + 1 < nsteps)
    def _():
        issue(i + 1, (i + 1) % 2)

    slot = i % 2
    pltpu.make_async_copy(y_ref.at[pl.ds(0, tm)], buf.at[slot], sem.at[slot]).wait()
    o = _layer_norm(ALPHA * h_ref[...] + buf[slot], g_ref[...], b_ref[...])
    of_ref[...] = o
    ob_ref[...] = o.astype(BF16)


def _gather_ln(dest, ys, hf, g, b, tm=256):
    n, d = hf.shape
    nsteps = n // tm
    row = pl.BlockSpec((tm, d), lambda i, dref: (i, 0))
    par = pl.BlockSpec((1, d), lambda i, dref: (0, 0))
    return pl.pallas_call(
        functools.partial(_gather_ln_kernel, nsteps=nsteps),
        grid_spec=pltpu.PrefetchScalarGridSpec(
            num_scalar_prefetch=1, grid=(nsteps,),
            in_specs=[pl.BlockSpec(memory_space=pl.ANY), row, par, par],
            out_specs=[row, row],
            scratch_shapes=[pltpu.VMEM((2, tm, d), F32), pltpu.SemaphoreType.DMA((2,))]),
        out_shape=[jax.ShapeDtypeStruct((n, d), F32), jax.ShapeDtypeStruct((n, d), BF16)],
        compiler_params=_cparams("arbitrary"), name="moe_gather_ln3")(
            dest, ys, hf, g.reshape(1, d), b.reshape(1, d))


def _moe(hf, p, w, l):
    n = hf.shape[0]
    region = n + MOE_TILE
    ntiles = n // MOE_TILE + N_GROUPS
    pay, dest, cnt = _route(hf, w["w_router_t"], w["router_bias_c"], region)
    dest = dest.reshape(n)
    cnt = cnt[:N_GROUPS, 0]
    tg, tb, nt = _tile_tables(cnt, region, ntiles)
    xs = _scatter(dest, cnt, pay, region)
    hid = _moe_up(tg, tb, nt, xs, p["moe_w_gate"], p["moe_w_up"], l)
    ys = _moe_down(tg, tb, nt, hid, w["moe_wd"], l)
    return _gather_ln(dest, ys, hf, p["ln3_g"][l], p["ln3_b"][l])


def _prep_params(p):
    L = p["w_in"].shape[0]
    w_in = p["w_in"]
    zeros = lambda w: jnp.zeros((L, D_MODEL, w), F32)
    o_mla = 2 * D_SG
    o_kr = o_mla + MLA_Q_RANK + MLA_KV_RANK
    o_gla = o_kr + MLA_ROPE
    gla_main = 2 * N_GLA * GLA_DK + 2 * N_GLA * GLA_DV
    o_ml = o_gla + gla_main + GLA_RANK
    ml_main = 2 * N_ML * ML_DK + 2 * N_ML * ML_DV
    o_gate = o_ml + ml_main + 2 * N_ML
    half = MLA_ROPE // 2
    kr = w_in[..., o_kr:o_gla]
    pad_r = LANES - MLA_ROPE
    w_mla = jnp.concatenate(
        [w_in[..., o_mla:o_kr], kr, zeros(pad_r), kr[..., half:], kr[..., :half], zeros(pad_r)], axis=-1)
    w_gla = jnp.concatenate([w_in[..., o_gla:o_gla + gla_main + GLA_RANK], zeros(LANES - GLA_RANK)], axis=-1)
    w_ml = jnp.concatenate([w_in[..., o_ml:o_gate], zeros(LANES - 2 * N_ML)], axis=-1)

    wq = p["mla_w_uq"].reshape(L, MLA_Q_RANK, N_MLA, MLA_NOPE + MLA_ROPE)
    rq = wq[..., MLA_NOPE:]
    zq = jnp.zeros((L, MLA_Q_RANK, N_MLA, pad_r), F32)
    wuq = jnp.concatenate([wq[..., :MLA_NOPE], rq, zq, rq[..., half:], rq[..., :half], zq], axis=-1)
    wkv = p["mla_w_ukv"].reshape(L, MLA_KV_RANK, N_MLA, MLA_NOPE + MLA_V)

    bf = lambda a: a.astype(BF16)
    ne = p["moe_w_down"].shape[1]
    return dict(
        w_sg=bf(w_in[..., :o_mla]), w_mla=bf(w_mla), w_gla=bf(w_gla), w_ml=bf(w_ml), w_gate=bf(w_in[..., o_gate:]),
        wuq=bf(wuq.reshape(L, MLA_Q_RANK, N_MLA * 3 * LANES)),
        wuk=bf(wkv[..., :MLA_NOPE].reshape(L, MLA_KV_RANK, N_MLA * MLA_NOPE)),
        wuv=bf(wkv[..., MLA_NOPE:].reshape(L, MLA_KV_RANK, N_MLA * MLA_V)),
        gla_wg=jnp.pad(p["gla_w_gate"], ((0, 0), (0, LANES - GLA_RANK), (0, 0))),
        w_branch=bf(p["w_branch"]), w_out=bf(p["w_out"]),
        x_w_q=bf(p["x_w_q"]), x_w_kv=bf(p["x_w_kv"]), x_w_o=bf(p["x_w_o"]),
        w_router_t=jnp.pad(p["w_router"].T, ((0, LANES - N_EXPERTS), (0, 0))),
        router_bias_c=p["router_bias"].reshape(N_EXPERTS, 1),
        moe_wd=p["moe_w_down"].reshape(L, N_GROUPS, (ne // N_GROUPS) * D_EXPERT, D_MODEL),
    )


def _mixer(hf, hb, cos_p, sin_p, p, w, l, batch, seq):
    z_sg = _mm(hb, w["w_sg"][l], F32, name="mm_sg")
    z_mla = _mm(hb, w["w_mla"][l], F32, name="mm_mla")
    z_gla = _mm(hb, w["w_gla"][l], F32, name="mm_gla")
    z_ml = _mm(hb, w["w_ml"][l], F32, name="mm_ml")
    gates = _mm(hb, w["w_gate"][l], BF16, act="sigmoid", tm=1024, tn=1024, name="mm_gate")

    y_a = _sg(z_sg, p["sg_vnorm_g"][l], p["sg_vnorm_b"][l], p["sg_w_s"][l], p["sg_b_s"][l])
    q, k, v = _mla_prep(z_mla, cos_p, sin_p, p["mla_qnorm_g"][l], p["mla_kvnorm_g"][l],
                        w["wuq"][l], w["wuk"][l], w["wuv"][l])
    y_b = _mla_attn(q, k, v, batch, seq)
    y_c = _gla(z_gla, w["gla_wg"][l], p["gla_b_gate"][l], p["gla_norm_g"][l], batch, seq)
    if_cols = z_ml[:, -LANES:-LANES + 2 * N_ML]
    ift = if_cols.reshape(batch, seq, 2 * N_ML).transpose(0, 2, 1)
    y_d = _mlstm(z_ml, ift, p["ml_conv_w"][l], p["ml_conv_b"][l], p["ml_gate_b"][l], p["ml_norm_g"][l],
                 batch, seq)
    merged = _merge(y_a, y_b, y_c, y_d, gates, w["w_branch"][l])
    return _mm_res_ln(merged, w["w_out"][l], hf, p["ln1_g"][l], p["ln1_b"][l], name="out_ln1")


def _forward(p):
    x = p["x"]
    batch, seq, d = x.shape
    n = batch * seq
    mem = p["mem"]
    mem_len = mem.shape[1]
    w = _prep_params(p)
    posb = jnp.broadcast_to(p["positions"].reshape(n, 1).astype(F32), (n, LANES))
    cos_p, sin_p = _rope_tables(posb)
    memb = mem.reshape(batch * mem_len, d).astype(BF16)
    hf, hb = _ln(x.reshape(n, d), p["ln_in_g"], p["ln_in_b"])
    for l in range(p["w_in"].shape[0]):
        hf, hb = _mixer(hf, hb, cos_p, sin_p, p, w, l, batch, seq)
        kv = _mm(memb, w["x_w_kv"][l], BF16, tm=512, tn=1024, name="mm_xkv")
        hf, hb = _xattn(hb, hf, w["x_w_q"][l], kv, w["x_w_o"][l], p["ln2_g"][l], p["ln2_b"][l],
                        batch, seq, mem_len)
        hf, hb = _moe(hf, p, w, l)
    return hf.reshape(batch, seq, d)


def kernel(x, mem, positions, ln_in_g, ln_in_b, w_in, sg_vnorm_g, sg_vnorm_b, sg_w_s, sg_b_s, mla_qnorm_g, mla_kvnorm_g, mla_w_uq, mla_w_ukv, gla_w_gate, gla_b_gate, gla_norm_g, ml_conv_w, ml_conv_b, ml_gate_b, ml_norm_g, w_branch, w_out, ln1_g, ln1_b, x_w_q, x_w_kv, x_w_o, ln2_g, ln2_b, w_router, router_bias, moe_w_gate, moe_w_up, moe_w_down, ln3_g, ln3_b):
    return _forward(dict(
        x=x, mem=mem, positions=positions, ln_in_g=ln_in_g, ln_in_b=ln_in_b, w_in=w_in,
        sg_vnorm_g=sg_vnorm_g, sg_vnorm_b=sg_vnorm_b, sg_w_s=sg_w_s, sg_b_s=sg_b_s,
        mla_qnorm_g=mla_qnorm_g, mla_kvnorm_g=mla_kvnorm_g, mla_w_uq=mla_w_uq, mla_w_ukv=mla_w_ukv,
        gla_w_gate=gla_w_gate, gla_b_gate=gla_b_gate, gla_norm_g=gla_norm_g,
        ml_conv_w=ml_conv_w, ml_conv_b=ml_conv_b, ml_gate_b=ml_gate_b, ml_norm_g=ml_norm_g,
        w_branch=w_branch, w_out=w_out, ln1_g=ln1_g, ln1_b=ln1_b,
        x_w_q=x_w_q, x_w_kv=x_w_kv, x_w_o=x_w_o, ln2_g=ln2_g, ln2_b=ln2_b,
        w_router=w_router, router_bias=router_bias,
        moe_w_gate=moe_w_gate, moe_w_up=moe_w_up, moe_w_down=moe_w_down, ln3_g=ln3_g, ln3_b=ln3_b))
```

```python
import functools
import math

import jax
import jax.numpy as jnp
from jax import lax
from jax.experimental import pallas as pl
from jax.experimental.pallas import tpu as pltpu

F32 = jnp.float32
BF16 = jnp.bfloat16

D_MODEL = 2048
DEPTH = 4
EPS = 1e-5
ALPHA = (2.0 * DEPTH) ** 0.25

SG_CHUNK = 128
N_SG = 4
D_SG = 512
N_MLA = 4
MLA_Q_RANK = 384
MLA_KV_RANK = 256
MLA_NOPE = 128
MLA_ROPE = 64
MLA_V = 128
MLA_QK = 256
ROPE_BASE = 10000.0
N_GLA = 4
GLA_DK = 64
GLA_DV = 128
GLA_RANK = 16
GLA_TAU = 16.0
GLA_CHUNK = 64
N_ML = 4
ML_DK = 64
ML_DV = 128
ML_CHUNK = 128
N_X = 4
X_HEAD = 128
N_EXPERTS = 16
D_EXPERT = 512

LANES = 128
VMEM_LIMIT = 48 * 1024 * 1024

_NT = (((1,), (1,)), ((), ()))
_TN = (((0,), (0,)), ((), ()))


def _cparams(*sem):
    return pltpu.CompilerParams(dimension_semantics=sem, vmem_limit_bytes=VMEM_LIMIT)


def _dot(a, b):
    return jnp.dot(a, b, preferred_element_type=F32)


def _dotg(a, b, dims):
    return lax.dot_general(a, b, dims, preferred_element_type=F32)


def _split3(a):
    a1 = a.astype(BF16)
    r1 = a - a1.astype(F32)
    a2 = r1.astype(BF16)
    a3 = (r1 - a2.astype(F32)).astype(BF16)
    return a1, a2, a3


def _dot_exact_rhs(a, ones_bf16):
    a1, a2, a3 = _split3(a)
    return _dot(a1, ones_bf16) + _dot(a2, ones_bf16) + _dot(a3, ones_bf16)


def _dot_exact_lhs(ones_bf16, a):
    a1, a2, a3 = _split3(a)
    return _dot(ones_bf16, a1) + _dot(ones_bf16, a2) + _dot(ones_bf16, a3)


def _dot_hi(a, b):
    a1 = a.astype(BF16)
    a2 = (a - a1.astype(F32)).astype(BF16)
    b1 = b.astype(BF16)
    b2 = (b - b1.astype(F32)).astype(BF16)
    return _dot(a1, b1) + _dot(a2, b1) + _dot(a1, b2)


def _sigmoid(x):
    return 1.0 / (1.0 + jnp.exp(-x))


def _log_sigmoid(x):
    return jnp.minimum(x, 0.0) - jnp.log(1.0 + jnp.exp(-jnp.abs(x)))


def _layer_norm(t, g, b):
    mu = jnp.mean(t, axis=-1, keepdims=True)
    c = t - mu
    var = jnp.mean(c * c, axis=-1, keepdims=True)
    return c * lax.rsqrt(var + EPS) * g + b


def _rms_norm(t, g):
    return t * lax.rsqrt(jnp.mean(t * t, axis=-1, keepdims=True) + EPS) * g


def _ln_kernel(x_ref, g_ref, b_ref, of_ref, ob_ref):
    y = _layer_norm(x_ref[...], g_ref[...], b_ref[...])
    of_ref[...] = y
    ob_ref[...] = y.astype(BF16)


def _ln(x, g, b, tm=256):
    n, d = x.shape
    row = pl.BlockSpec((tm, d), lambda i: (i, 0))
    par = pl.BlockSpec((1, d), lambda i: (0, 0))
    return pl.pallas_call(
        _ln_kernel, grid=(n // tm,), in_specs=[row, par, par], out_specs=[row, row],
        out_shape=[jax.ShapeDtypeStruct((n, d), F32), jax.ShapeDtypeStruct((n, d), BF16)],
        compiler_params=_cparams("parallel"), name="ln_in")(x, g.reshape(1, d), b.reshape(1, d))


def _mm_kernel(x_ref, w_ref, o_ref, *, act):
    acc = _dot(x_ref[...], w_ref[...])
    if act == "sigmoid":
        acc = _sigmoid(acc)
    o_ref[...] = acc.astype(o_ref.dtype)


def _mm(x, w, out_dtype, act=None, tm=512, tn=None, name="mm"):
    n, k = x.shape
    m = w.shape[1]
    tn = m if tn is None else tn
    return pl.pallas_call(
        functools.partial(_mm_kernel, act=act), grid=(n // tm, m // tn),
        in_specs=[pl.BlockSpec((tm, k), lambda i, j: (i, 0)),
                  pl.BlockSpec((k, tn), lambda i, j: (0, j))],
        out_specs=pl.BlockSpec((tm, tn), lambda i, j: (i, j)),
        out_shape=jax.ShapeDtypeStruct((n, m), out_dtype),
        compiler_params=_cparams("parallel", "parallel"), name=name)(x, w)


def _mm_w32_kernel(x_ref, w_ref, o_ref, ws, *, act, col0, nblk, ncols):
    j = pl.program_id(0)

    @pl.when(pl.program_id(1) == 0)
    def _():
        w = w_ref[...]
        tn = w.shape[1]
        if col0 + nblk * tn > ncols:
            col = col0 + j * tn + lax.broadcasted_iota(jnp.int32, (1, tn), 1)
            w = jnp.where(col < ncols, w, 0.0)
        ws[...] = w.astype(BF16)

    acc = _dot(x_ref[...], ws[...])
    if act == "sigmoid":
        acc = _sigmoid(acc)
    o_ref[...] = acc.astype(o_ref.dtype)


def _mm_w32(x, w_all, l, col0, nblk, tn, out_dtype, act=None, tm=1024, name="mm_w32"):
    n, k = x.shape
    assert col0 % tn == 0
    ncols = w_all.shape[2]
    return pl.pallas_call(
        functools.partial(_mm_w32_kernel, act=act, col0=col0, nblk=nblk, ncols=ncols), grid=(nblk, n // tm),
        in_specs=[pl.BlockSpec((tm, k), lambda j, i: (i, 0)),
                  pl.BlockSpec((None, k, tn), lambda j, i: (l, 0, col0 // tn + j))],
        out_specs=pl.BlockSpec((tm, tn), lambda j, i: (i, j)),
        out_shape=jax.ShapeDtypeStruct((n, nblk * tn), out_dtype),
        scratch_shapes=[pltpu.VMEM((k, tn), BF16)],
        compiler_params=_cparams("arbitrary", "arbitrary"), name=name)(x, w_all)


def _mm_res_ln_kernel(x_ref, w_ref, h_ref, g_ref, b_ref, of_ref, ob_ref, *acc, nk):
    def finish(y):
        t = ALPHA * h_ref[...] + y
        o = _layer_norm(t, g_ref[...], b_ref[...])
        of_ref[...] = o
        ob_ref[...] = o.astype(BF16)

    if nk == 1:
        finish(_dot(x_ref[...], w_ref[...]))
        return
    acc_ref, = acc
    kk = pl.program_id(1)

    @pl.when(kk == 0)
    def _():
        acc_ref[...] = jnp.zeros_like(acc_ref)

    acc_ref[...] += _dot(x_ref[...], w_ref[...])

    @pl.when(kk == nk - 1)
    def _():
        finish(acc_ref[...])


def _mm_res_ln(x, w, h, g, b, tm=256, tk=None, name="mm_res_ln"):
    n, k = x.shape
    d = w.shape[1]
    tk = k if tk is None else tk
    nk = k // tk
    row = pl.BlockSpec((tm, d), lambda i, j: (i, 0))
    par = pl.BlockSpec((1, d), lambda i, j: (0, 0))
    scratch = [] if nk == 1 else [pltpu.VMEM((tm, d), F32)]
    return pl.pallas_call(
        functools.partial(_mm_res_ln_kernel, nk=nk), grid=(n // tm, nk),
        in_specs=[pl.BlockSpec((tm, tk), lambda i, j: (i, j)),
                  pl.BlockSpec((tk, d), lambda i, j: (j, 0)), row, par, par],
        out_specs=[row, row],
        out_shape=[jax.ShapeDtypeStruct((n, d), F32), jax.ShapeDtypeStruct((n, d), BF16)],
        scratch_shapes=scratch,
        compiler_params=_cparams("parallel", "arbitrary"), name=name)(
            x, w, h, g.reshape(1, d), b.reshape(1, d))


def _sg_kernel(z_ref, vg_ref, vb_ref, ws_ref, bst_ref, o_ref, *, nchunk):
    z = z_ref[...]
    z = 0.5 * z * (1.0 + jnp.tanh(math.sqrt(2.0 / math.pi) * (z + 0.044715 * (z * z * z))))
    u = z[:, :D_SG]
    vn = _layer_norm(z[:, D_SG:], vg_ref[...], vb_ref[...]).astype(BF16)
    r = lax.broadcasted_iota(jnp.int32, (SG_CHUNK, SG_CHUNK), 0)
    c = lax.broadcasted_iota(jnp.int32, (SG_CHUNK, SG_CHUNK), 1)
    causal = c <= r
    gw = SG_CHUNK
    for g in range(N_SG):
        w = jnp.where(causal, ws_ref[g], 0.0).astype(BF16)
        bias = bst_ref[:, g:g + 1]
        for ci in range(nchunk):
            rs = slice(ci * SG_CHUNK, (ci + 1) * SG_CHUNK)
            cs = slice(g * gw, (g + 1) * gw)
            mixed = _dot(w, vn[rs, cs]) + bias
            o_ref[rs, cs] = (u[rs, cs] * mixed).astype(BF16)


def _sg(z, vg, vb, ws, bs, tm=512):
    n = z.shape[0]
    return pl.pallas_call(
        functools.partial(_sg_kernel, nchunk=tm // SG_CHUNK), grid=(n // tm,),
        in_specs=[pl.BlockSpec((tm, 2 * D_SG), lambda i: (i, 0)),
                  pl.BlockSpec((1, D_SG), lambda i: (0, 0)),
                  pl.BlockSpec((1, D_SG), lambda i: (0, 0)),
                  pl.BlockSpec((N_SG, SG_CHUNK, SG_CHUNK), lambda i: (0, 0, 0)),
                  pl.BlockSpec((SG_CHUNK, N_SG), lambda i: (0, 0))],
        out_specs=pl.BlockSpec((tm, D_SG), lambda i: (i, 0)),
        out_shape=jax.ShapeDtypeStruct((n, D_SG), BF16),
        compiler_params=_cparams("parallel"), name="sg")(
            z, vg.reshape(1, D_SG), vb.reshape(1, D_SG), ws, bs.T)


def _rope_kernel(pos_ref, cos_ref, sin_ref):
    half = MLA_ROPE // 2
    lane = lax.broadcasted_iota(jnp.int32, (1, LANES), 1)
    idx = jnp.bitwise_and(lane, half - 1).astype(F32)
    freq = jnp.exp(idx * (-math.log(ROPE_BASE) / half))
    ang = pos_ref[...] * freq
    c = jnp.cos(ang)
    s = jnp.sin(ang)
    cos_ref[...] = jnp.where(lane < MLA_ROPE, c, 0.0)
    sin_ref[...] = jnp.where(lane < half, -s, jnp.where(lane < MLA_ROPE, s, 0.0))


def _rope_tables(posb, tm=512):
    n = posb.shape[0]
    row = pl.BlockSpec((tm, LANES), lambda i: (i, 0))
    return pl.pallas_call(
        _rope_kernel, grid=(n // tm,), in_specs=[row], out_specs=[row, row],
        out_shape=[jax.ShapeDtypeStruct((n, LANES), F32)] * 2,
        compiler_params=_cparams("parallel"), name="rope_tables")(posb)


def _mla_prep_kernel(z_ref, cos_ref, sin_ref, qg_ref, kvg_ref, wuq_ref, wuk_ref, wuv_ref,
                     q_ref, k_ref, v_ref):
    z = z_ref[...]
    cq = _rms_norm(z[:, :MLA_Q_RANK], qg_ref[...]).astype(BF16)
    o1 = MLA_Q_RANK + MLA_KV_RANK
    ckv = _rms_norm(z[:, MLA_Q_RANK:o1], kvg_ref[...]).astype(BF16)
    cos_p = cos_ref[...]
    sin_p = sin_ref[...]
    kr = z[:, o1:o1 + LANES]
    half = MLA_ROPE // 2
    lane = lax.broadcasted_iota(jnp.int32, (1, LANES), 1)
    kr_swapped = jnp.where(lane < half, pltpu.roll(kr, LANES - half, 1), pltpu.roll(kr, half, 1))
    k_tail = (kr * cos_p + kr_swapped * sin_p).astype(BF16)
    qa = _dot(cq, wuq_ref[...])
    kn = _dot(ckv, wuk_ref[...])
    v_ref[...] = _dot(ckv, wuv_ref[...]).astype(BF16)
    scale = (MLA_NOPE + MLA_ROPE) ** -0.5
    for h in range(N_MLA):
        b0 = h * 3 * LANES
        q_tail = qa[:, b0 + LANES:b0 + 2 * LANES] * cos_p + qa[:, b0 + 2 * LANES:b0 + 3 * LANES] * sin_p
        q_ref[:, h * MLA_QK:h * MLA_QK + LANES] = (qa[:, b0:b0 + LANES] * scale).astype(BF16)
        q_ref[:, h * MLA_QK + LANES:(h + 1) * MLA_QK] = (q_tail * scale).astype(BF16)
        k_ref[:, h * MLA_QK:h * MLA_QK + LANES] = kn[:, h * LANES:(h + 1) * LANES].astype(BF16)
        k_ref[:, h * MLA_QK + LANES:(h + 1) * MLA_QK] = k_tail


def _mla_prep(z, cos_p, sin_p, qg, kvg, wuq, wuk, wuv, tm=512):
    n, zw = z.shape
    row = lambda w: pl.BlockSpec((tm, w), lambda i: (i, 0))
    full = lambda a: pl.BlockSpec(a.shape, lambda i: (0,) * a.ndim)
    qg = qg.reshape(1, -1)
    kvg = kvg.reshape(1, -1)
    return pl.pallas_call(
        _mla_prep_kernel, grid=(n // tm,),
        in_specs=[row(zw), row(LANES), row(LANES), full(qg), full(kvg), full(wuq), full(wuk), full(wuv)],
        out_specs=[row(N_MLA * MLA_QK), row(N_MLA * MLA_QK), row(N_MLA * MLA_V)],
        out_shape=[jax.ShapeDtypeStruct((n, N_MLA * MLA_QK), BF16),
                   jax.ShapeDtypeStruct((n, N_MLA * MLA_QK), BF16),
                   jax.ShapeDtypeStruct((n, N_MLA * MLA_V), BF16)],
        compiler_params=_cparams("parallel"), name="mla_prep")(z, cos_p, sin_p, qg, kvg, wuq, wuk, wuv)


def _mla_attn_kernel(q_ref, k_ref, v_ref, o_ref, *, tq, tk):
    i = pl.program_id(2)
    q = q_ref[...]
    qpos = i * tq + lax.broadcasted_iota(jnp.int32, (tq, tk), 0)
    kofs = lax.broadcasted_iota(jnp.int32, (tq, tk), 1)

    def body(j, carry):
        m, l, acc = carry
        start = pl.multiple_of(j * tk, tk)
        s = _dotg(q, k_ref[pl.ds(start, tk), :], _NT)
        s = jnp.where(kofs + j * tk <= qpos, s, -jnp.inf)
        m_new = jnp.maximum(m, jnp.max(s, axis=-1, keepdims=True))
        p = jnp.exp(s - m_new)
        a = jnp.exp(m - m_new)
        l = a * l + jnp.sum(p, axis=-1, keepdims=True)
        acc = a * acc + _dot(p.astype(BF16), v_ref[pl.ds(start, tk), :])
        return m_new, l, acc

    init = (jnp.full((tq, 1), -jnp.inf, F32), jnp.zeros((tq, 1), F32), jnp.zeros((tq, MLA_V), F32))
    nblk = (i * tq + tq + tk - 1) // tk
    _, l, acc = lax.fori_loop(0, nblk, body, init)
    o_ref[...] = (acc / l).astype(BF16)


def _mla_attn(q, k, v, batch, seq, tq=512, tk=512):
    n = q.shape[0]
    nq = seq // tq
    return pl.pallas_call(
        functools.partial(_mla_attn_kernel, tq=tq, tk=tk), grid=(batch, N_MLA, nq),
        in_specs=[pl.BlockSpec((tq, MLA_QK), lambda b, h, i: (b * nq + i, h)),
                  pl.BlockSpec((seq, MLA_QK), lambda b, h, i: (b, h)),
                  pl.BlockSpec((seq, MLA_V), lambda b, h, i: (b, h))],
        out_specs=pl.BlockSpec((tq, MLA_V), lambda b, h, i: (b * nq + i, h)),
        out_shape=jax.ShapeDtypeStruct((n, N_MLA * MLA_V), BF16),
        compiler_params=_cparams("parallel", "parallel", "arbitrary"), name="mla_attn")(q, k, v)


def _gla_kernel(q_ref, k_ref, v_ref, og_ref, lr_ref, wg_ref, bg_ref, ng_ref, o_ref, st_ref, *, nchunk):
    L = GLA_CHUNK
    qkw = N_GLA * GLA_DK

    @pl.when(pl.program_id(1) == 0)
    def _():
        st_ref[...] = jnp.zeros_like(st_ref)

    logits = _dot_hi(lr_ref[...], wg_ref[...]) + bg_ref[...]
    log_a = _log_sigmoid(logits) * (1.0 / GLA_TAU)
    lane = lax.broadcasted_iota(jnp.int32, (1, qkw), 1)
    masks = [((lane >= h * GLA_DK) & (lane < (h + 1) * GLA_DK)).astype(F32) for h in range(N_GLA)]
    r = lax.broadcasted_iota(jnp.int32, (L, L), 0)
    c = lax.broadcasted_iota(jnp.int32, (L, L), 1)
    causal = c <= r
    tril = jnp.where(causal, 1.0, 0.0).astype(BF16)
    ng = ng_ref[...]
    st = st_ref[...]
    for ci in range(nchunk):
        rs = slice(ci * L, (ci + 1) * L)
        b = _dot_exact_lhs(tril, log_a[rs])
        b_last = b[L - 1:L, :]
        q = q_ref[rs, :] * (GLA_DK ** -0.5)
        k = k_ref[rs, :]
        qt = q * jnp.exp(b)
        kt = (k * jnp.exp(-b)).astype(BF16)
        kd = (k * jnp.exp(b_last - b)).astype(BF16)
        qstack = jnp.concatenate([qt * masks[h] for h in range(N_GLA)], axis=0).astype(BF16)
        att = _dotg(qstack, kt, _NT)
        inter = _dotg(qstack, st.astype(BF16), _NT)
        vb = v_ref[rs, :].astype(BF16)
        for h in range(N_GLA):
            hs = slice(h * L, (h + 1) * L)
            vs = slice(h * GLA_DV, (h + 1) * GLA_DV)
            a_h = jnp.where(causal, att[hs], 0.0).astype(BF16)
            o_h = _rms_norm(_dot(a_h, vb[:, vs]) + inter[hs], ng)
            g = og_ref[rs, vs]
            o_ref[rs, vs] = (o_h * (g * _sigmoid(g))).astype(BF16)
        upd = _dotg(vb, kd, _TN)
        new = st * jnp.exp(b_last)
        for h in range(N_GLA):
            new = new + upd[h * GLA_DV:(h + 1) * GLA_DV] * masks[h]
        st = new
    st_ref[...] = st


def _gla(z, wg, bg, ng, batch, seq, tm=256):
    n = z.shape[0]
    ns = seq // tm
    qkw = N_GLA * GLA_DK
    vw = N_GLA * GLA_DV
    col = lambda w, j: pl.BlockSpec((tm, w), lambda b, s: (b * ns + s, j))
    full = lambda a: pl.BlockSpec(a.shape, lambda b, s: (0,) * a.ndim)
    bg = bg.reshape(1, qkw)
    ng = ng.reshape(1, GLA_DV)
    return pl.pallas_call(
        functools.partial(_gla_kernel, nchunk=tm // GLA_CHUNK), grid=(batch, ns),
        in_specs=[col(qkw, 0), col(qkw, 1), col(vw, 1), col(vw, 2), col(LANES, (2 * qkw + 2 * vw) // LANES),
                  full(wg), full(bg), full(ng)],
        out_specs=pl.BlockSpec((tm, vw), lambda b, s: (b * ns + s, 0)),
        out_shape=jax.ShapeDtypeStruct((n, vw), BF16),
        scratch_shapes=[pltpu.VMEM((GLA_DV, qkw), F32)],
        compiler_params=_cparams("parallel", "arbitrary"), name="gla")(z, z, z, z, z, wg, bg, ng)


def _mlstm_kernel(qk_ref, v_ref, op_ref, if_ref, ift_ref, cw_ref, cb_ref, gb_ref, gbt_ref, ng_ref,
                  o_ref, ct_ref, n_ref, mw_ref, mk_ref, tail_ref, *, nchunk):
    L = ML_CHUNK
    qkw = N_ML * ML_DK
    tm = nchunk * L

    @pl.when(pl.program_id(1) == 0)
    def _():
        ct_ref[...] = jnp.zeros_like(ct_ref)
        n_ref[...] = jnp.zeros_like(n_ref)
        mw_ref[...] = jnp.zeros_like(mw_ref)
        mk_ref[...] = jnp.zeros_like(mk_ref)
        tail_ref[...] = jnp.zeros_like(tail_ref)

    x = qk_ref[...]
    tail = tail_ref[...]
    row8 = lax.broadcasted_iota(jnp.int32, (8, 2 * qkw), 0)
    acc = x * cw_ref[3:4, :] + cb_ref[...]
    for j in range(1, 4):
        rx = pltpu.roll(x, j, 0)
        fix = jnp.where(row8 < j, pltpu.roll(tail, j, 0), rx[0:8])
        acc = acc + jnp.concatenate([fix, rx[8:]], axis=0) * cw_ref[3 - j:4 - j, :]
    tail_ref[...] = x[tm - 8:tm]
    y = acc * _sigmoid(acc)
    q = y[:, :qkw] * (ML_DK ** -0.5)
    k = y[:, qkw:]

    gates = if_ref[...] + gb_ref[...]
    fc = _log_sigmoid(gates)
    gt = ift_ref[0] + gbt_ref[...]
    fct = _log_sigmoid(gt)

    lane = lax.broadcasted_iota(jnp.int32, (1, qkw), 1)
    masks = [((lane >= h * ML_DK) & (lane < (h + 1) * ML_DK)).astype(F32) for h in range(N_ML)]
    r = lax.broadcasted_iota(jnp.int32, (L, L), 0)
    c = lax.broadcasted_iota(jnp.int32, (L, L), 1)
    causal = c <= r
    tril = jnp.where(causal, 1.0, 0.0).astype(BF16)
    triu = jnp.where(r <= c, 1.0, 0.0).astype(BF16)

    def selector(width, block, first):
        rr = lax.broadcasted_iota(jnp.int32, (LANES, width), 0)
        cc = lax.broadcasted_iota(jnp.int32, (LANES, width), 1)
        return jnp.where(rr == (cc >> int(math.log2(block))) + first, 1.0, 0.0).astype(BF16)

    wide = N_ML * L
    fcb = _dot_exact_rhs(fc, selector(wide, L, N_ML))
    icb = _dot_exact_rhs(gates, selector(wide, L, 0))
    fck = _dot_exact_rhs(fc, selector(qkw, ML_DK, N_ML))
    ick = _dot_exact_rhs(gates, selector(qkw, ML_DK, 0))

    ng = ng_ref[...]
    ct = ct_ref[...]
    nrow = n_ref[...]
    mwide = mw_ref[...]
    mk = mk_ref[...]

    for ci in range(nchunk):
        rs = slice(ci * L, (ci + 1) * L)
        bb = _dot_exact_lhs(tril, fcb[rs])
        bk = _dot_exact_lhs(tril, fck[rs])
        brow = _dot_exact_rhs(fct[:, rs], triu)
        rowterm = gt[:, rs] - pltpu.roll(brow, N_ML, 0)
        qc = q[rs]
        kc = k[rs]
        log_d = jnp.concatenate(
            [jnp.where(causal, bb[:, h * L:(h + 1) * L] + rowterm[h:h + 1, :], -jnp.inf) for h in range(N_ML)],
            axis=0)
        log_inter = jnp.concatenate(
            [bb[:, h * L:(h + 1) * L] + mwide[:, h * L:(h + 1) * L] for h in range(N_ML)], axis=0)
        m_t = jnp.maximum(log_inter, jnp.max(log_d, axis=-1, keepdims=True))
        w_inter = jnp.exp(log_inter - m_t)
        qst = jnp.concatenate([qc * masks[h] for h in range(N_ML)], axis=0)
        qsb = qst.astype(BF16)
        s_all = _dotg(qsb, kc.astype(BF16), _NT) * jnp.exp(log_d - m_t)
        sb = s_all.astype(BF16)
        vb = v_ref[rs, :].astype(BF16)
        num = jnp.concatenate(
            [_dot(sb[h * L:(h + 1) * L], vb[:, h * ML_DV:(h + 1) * ML_DV]) for h in range(N_ML)], axis=0)
        num = num + w_inter * _dotg(qsb, ct.astype(BF16), _NT)
        den = jnp.sum(s_all, axis=-1, keepdims=True) + w_inter * jnp.sum(qst * nrow, axis=-1, keepdims=True)
        hh = _rms_norm(num / jnp.maximum(jnp.abs(den), jnp.exp(-m_t)), ng)
        for h in range(N_ML):
            vs = slice(h * ML_DV, (h + 1) * ML_DV)
            o_ref[rs, vs] = (hh[h * L:(h + 1) * L] * _sigmoid(op_ref[rs, vs])).astype(BF16)
        bl_w = bb[L - 1:L, :]
        lw_w = bl_w - bb + icb[rs]
        mwide_new = jnp.maximum(bl_w + mwide, jnp.max(lw_w, axis=0, keepdims=True))
        bl_k = bk[L - 1:L, :]
        lw_k = bl_k - bk + ick[rs]
        mk_new = jnp.maximum(bl_k + mk, jnp.max(lw_k, axis=0, keepdims=True))
        decay = jnp.exp(bl_k + mk - mk_new)
        kw = kc * jnp.exp(lw_k - mk_new)
        upd = _dotg(vb, kw.astype(BF16), _TN)
        ct = ct * decay
        for h in range(N_ML):
            ct = ct + upd[h * ML_DV:(h + 1) * ML_DV] * masks[h]
        nrow = nrow * decay + jnp.sum(kw, axis=0, keepdims=True)
        mwide = mwide_new
        mk = mk_new

    ct_ref[...] = ct
    n_ref[...] = nrow
    mw_ref[...] = mwide
    mk_ref[...] = mk


def _mlstm(z, ift, cw, cb, gb, ng, batch, seq, tm=512):
    n = z.shape[0]
    ns = seq // tm
    qkw = N_ML * ML_DK
    vw = N_ML * ML_DV
    col = lambda w, j: pl.BlockSpec((tm, w), lambda b, s: (b * ns + s, j))
    full = lambda a: pl.BlockSpec(a.shape, lambda b, s: (0,) * a.ndim)
    cb = cb.reshape(1, 2 * qkw)
    gbp = jnp.pad(gb, (0, LANES - 2 * N_ML)).reshape(1, LANES)
    gbt = gb.reshape(2 * N_ML, 1)
    ng = ng.reshape(1, ML_DV)
    return pl.pallas_call(
        functools.partial(_mlstm_kernel, nchunk=tm // ML_CHUNK), grid=(batch, ns),
        in_specs=[col(2 * qkw, 0), col(vw, 1), col(vw, 2), col(LANES, (2 * qkw + 2 * vw) // LANES),
                  pl.BlockSpec((1, 2 * N_ML, tm), lambda b, s: (b, 0, s)),
                  full(cw), full(cb), full(gbp), full(gbt), full(ng)],
        out_specs=pl.BlockSpec((tm, vw), lambda b, s: (b * ns + s, 0)),
        out_shape=jax.ShapeDtypeStruct((n, vw), BF16),
        scratch_shapes=[pltpu.VMEM((ML_DV, qkw), F32), pltpu.VMEM((1, qkw), F32),
                        pltpu.VMEM((1, N_ML * ML_CHUNK), F32), pltpu.VMEM((1, qkw), F32),
                        pltpu.VMEM((8, 2 * qkw), F32)],
        compiler_params=_cparams("parallel", "arbitrary"), name="mlstm")(
            z, z, z, z, ift, cw, cb, gbp, gbt, ng)


GATE_COL = 2 * D_SG + (MLA_Q_RANK + MLA_KV_RANK + MLA_ROPE) + \
    (2 * N_GLA * GLA_DK + 2 * N_GLA * GLA_DV + GLA_RANK) + (2 * N_ML * ML_DK + 2 * N_ML * ML_DV + 2 * N_ML)
GATE_TN = 512
GATE_COL0 = (GATE_COL // GATE_TN) * GATE_TN
GATE_WIN0 = ((GATE_COL - GATE_COL0) // LANES) * LANES
GATE_LANE0 = (GATE_COL - GATE_COL0) % LANES
GATE_WINW = D_MODEL + LANES
GATE_NBLK = -(-(GATE_COL - GATE_COL0 + 4 * D_MODEL) // GATE_TN)


def _merge_kernel(ya_ref, yb_ref, yc_ref, yd_ref, g_ref, wb_ref, o_ref):
    acc = None
    for i, y_ref in enumerate((ya_ref, yb_ref, yc_ref, yd_ref)):
        p = _dot(y_ref[...], wb_ref[i])
        c0 = i * D_MODEL + GATE_WIN0
        t = g_ref[:, c0:c0 + GATE_WINW].astype(F32) * p
        acc = t if acc is None else acc + t
    o_ref[...] = acc.astype(BF16)


def _merge(ya, yb, yc, yd, gates, wb, tm=256):
    n, bw = ya.shape
    row = pl.BlockSpec((tm, bw), lambda i: (i, 0))
    return pl.pallas_call(
        _merge_kernel, grid=(n // tm,),
        in_specs=[row, row, row, row, pl.BlockSpec((tm, gates.shape[1]), lambda i: (i, 0)),
                  pl.BlockSpec(wb.shape, lambda i: (0, 0, 0))],
        out_specs=pl.BlockSpec((tm, GATE_WINW), lambda i: (i, 0)),
        out_shape=jax.ShapeDtypeStruct((n, GATE_WINW), BF16),
        compiler_params=_cparams("parallel"), name="merge")(ya, yb, yc, yd, gates, wb)


def _xattn_kernel(hb_ref, hf_ref, wq_ref, k_ref, v_ref, wo_ref, g_ref, b_ref, of_ref, ob_ref):
    q = (_dot(hb_ref[...], wq_ref[...]) * (X_HEAD ** -0.5)).astype(BF16)
    outs = []
    for h in range(N_X):
        hs = slice(h * X_HEAD, (h + 1) * X_HEAD)
        s = _dotg(q[:, hs], k_ref[:, hs], _NT)
        p = jnp.exp(s - jnp.max(s, axis=-1, keepdims=True))
        l = jnp.sum(p, axis=-1, keepdims=True)
        outs.append((_dot(p.astype(BF16), v_ref[:, hs]) / l).astype(BF16))
    y = _dot(jnp.concatenate(outs, axis=-1), wo_ref[...])
    o = _layer_norm(ALPHA * hf_ref[...] + y, g_ref[...], b_ref[...])
    of_ref[...] = o
    ob_ref[...] = o.astype(BF16)


def _xattn(hb, hf, wq, kv, wo, g, b, batch, seq, mem_len, tm=512):
    n, d = hf.shape
    ns = seq // tm
    xw = N_X * X_HEAD
    row = pl.BlockSpec((tm, d), lambda bb, s: (bb * ns + s, 0))
    par = pl.BlockSpec((1, d), lambda bb, s: (0, 0))
    return pl.pallas_call(
        _xattn_kernel, grid=(batch, ns),
        in_specs=[row, row, pl.BlockSpec((d, xw), lambda bb, s: (0, 0)),
                  pl.BlockSpec((mem_len, xw), lambda bb, s: (bb, 0)),
                  pl.BlockSpec((mem_len, xw), lambda bb, s: (bb, 1)),
                  pl.BlockSpec((xw, d), lambda bb, s: (0, 0)), par, par],
        out_specs=[row, row],
        out_shape=[jax.ShapeDtypeStruct((n, d), F32), jax.ShapeDtypeStruct((n, d), BF16)],
        compiler_params=_cparams("parallel", "parallel"), name="xattn")(
            hb, hf, wq, kv, kv, wo, g.reshape(1, d), b.reshape(1, d))


N_GROUPS = 4
PER_GROUP = N_EXPERTS // N_GROUPS
MOE_TILE = 512
PAY_W = D_MODEL + LANES


def _route_kernel(h_ref, wrt_ref, rbc_ref, pay_ref, dest_ref, cnt_ref, carry_ref, *, region):
    tm = h_ref.shape[0]

    @pl.when(pl.program_id(0) == 0)
    def _():
        carry_ref[...] = jnp.zeros_like(carry_ref)

    h = h_ref[...]
    w = wrt_ref[...]
    w1 = w.astype(BF16)
    w2 = (w - w1.astype(F32)).astype(BF16)
    h1 = h.astype(BF16)
    h2 = (h - h1.astype(F32)).astype(BF16)
    logits = _dotg(w1, h1, _NT) + _dotg(w2, h1, _NT) + _dotg(w1, h2, _NT)
    aff = _sigmoid(logits[0:N_EXPERTS])
    biased = aff + rbc_ref[...]
    row = lax.broadcasted_iota(jnp.int32, (N_EXPERTS, 1), 0).astype(F32)
    big = float(LANES)
    best = e1 = e2 = None
    for g in range(N_GROUPS):
        x = jnp.where((row >= g * PER_GROUP) & (row < (g + 1) * PER_GROUP), biased, -jnp.inf)
        m1 = jnp.max(x, axis=0, keepdims=True)
        i1 = jnp.min(jnp.where(x == m1, row, big), axis=0, keepdims=True)
        x2 = jnp.where(row == i1, -jnp.inf, x)
        m2 = jnp.max(x2, axis=0, keepdims=True)
        i2 = jnp.min(jnp.where(x2 == m2, row, big), axis=0, keepdims=True)
        score = m1 + m2
        if g == 0:
            best, e1, e2 = score, i1, i2
        else:
            better = score > best
            best = jnp.where(better, score, best)
            e1 = jnp.where(better, i1, e1)
            e2 = jnp.where(better, i2, e2)
    s1 = jnp.sum(jnp.where(row == e1, aff, 0.0), axis=0, keepdims=True)
    s2 = jnp.sum(jnp.where(row == e2, aff, 0.0), axis=0, keepdims=True)
    tot = s1 + s2
    gates_t = jnp.where(row == e1, s1 / tot, 0.0) + jnp.where(row == e2, s2 / tot, 0.0)

    grp = jnp.zeros_like(e1)
    for g in range(1, N_GROUPS):
        grp = grp + jnp.where(e1 >= g * PER_GROUP, 1.0, 0.0)
    row8 = lax.broadcasted_iota(jnp.int32, (8, 1), 0).astype(F32)
    onehot = jnp.where(row8 == grp, 1.0, 0.0)
    r = lax.broadcasted_iota(jnp.int32, (tm, tm), 0)
    c = lax.broadcasted_iota(jnp.int32, (tm, tm), 1)
    earlier = jnp.where(r < c, 1.0, 0.0).astype(BF16)
    rank_in = _dot(onehot.astype(BF16), earlier)
    carry = carry_ref[...]
    rank = jnp.sum(onehot * (rank_in + carry[:, 0:1]), axis=0, keepdims=True)
    dest_ref[...] = (grp * float(region) + rank).astype(jnp.int32)
    carry = carry + jnp.sum(onehot, axis=1, keepdims=True)
    carry_ref[...] = carry
    cnt_ref[...] = carry.astype(jnp.int32)

    gates = jnp.concatenate([gates_t, jnp.zeros((LANES - N_EXPERTS, tm), F32)], axis=0).T
    pay_ref[:, :D_MODEL] = h
    pay_ref[:, D_MODEL:] = gates


def _route(hf, wrt, rbc, region, tm=512):
    n, d = hf.shape
    return pl.pallas_call(
        functools.partial(_route_kernel, region=region), grid=(n // tm,),
        in_specs=[pl.BlockSpec((tm, d), lambda i: (i, 0)),
                  pl.BlockSpec((LANES, d), lambda i: (0, 0)),
                  pl.BlockSpec((N_EXPERTS, 1), lambda i: (0, 0))],
        out_specs=[pl.BlockSpec((tm, PAY_W), lambda i: (i, 0)),
                   pl.BlockSpec((1, tm), lambda i: (0, i)),
                   pl.BlockSpec((8, LANES), lambda i: (0, 0))],
        out_shape=[jax.ShapeDtypeStruct((n, PAY_W), F32), jax.ShapeDtypeStruct((1, n), jnp.int32),
                   jax.ShapeDtypeStruct((8, LANES), jnp.int32)],
        scratch_shapes=[pltpu.VMEM((8, LANES), F32)],
        compiler_params=_cparams("arbitrary"), name="route")(hf, wrt, rbc)


def _scatter_kernel(dest_ref, cnt_ref, pay_ref, out_ref, buf, zbuf, sem, zsem, *, region, nsteps):
    tm = pay_ref.shape[0]
    i = pl.program_id(0)
    slot = i % 2

    def wait_rows(s):
        pltpu.make_async_copy(buf.at[s], out_ref.at[pl.ds(0, tm)], sem.at[s]).wait()

    @pl.when(i == 0)
    def _():
        zbuf[...] = jnp.zeros_like(zbuf)
        copies = []
        for g in range(N_GROUPS):
            start = pl.multiple_of(g * region + (cnt_ref[g] // MOE_TILE) * MOE_TILE, MOE_TILE)
            copies.append(pltpu.make_async_copy(zbuf, out_ref.at[pl.ds(start, MOE_TILE)], zsem))
        for cp in copies:
            cp.start()
        for cp in copies:
            cp.wait()

    @pl.when(i >= 2)
    def _():
        wait_rows(slot)

    buf[slot] = pay_ref[...]

    base = i * tm
    for r in range(tm):
        d = dest_ref[base + r]
        pltpu.make_async_copy(buf.at[slot, pl.ds(r, 1)], out_ref.at[pl.ds(d, 1)],
                              sem.at[slot]).start(priority=r % 2)

    @pl.when(i == nsteps - 1)
    def _():
        wait_rows(slot)
        if nsteps >= 2:
            wait_rows(1 - slot)


def _scatter(dest, cnt, pay, region, tm=256):
    n = pay.shape[0]
    nsteps = n // tm
    return pl.pallas_call(
        functools.partial(_scatter_kernel, region=region, nsteps=nsteps),
        grid_spec=pltpu.PrefetchScalarGridSpec(
            num_scalar_prefetch=2, grid=(nsteps,),
            in_specs=[pl.BlockSpec((tm, PAY_W), lambda i, d, c: (i, 0))],
            out_specs=pl.BlockSpec(memory_space=pl.ANY),
            scratch_shapes=[pltpu.VMEM((2, tm, PAY_W), F32), pltpu.VMEM((MOE_TILE, PAY_W), F32),
                            pltpu.SemaphoreType.DMA((2,)), pltpu.SemaphoreType.DMA(())]),
        out_shape=jax.ShapeDtypeStruct((N_GROUPS * region, PAY_W), F32),
        compiler_params=_cparams("arbitrary"), name="moe_scatter")(dest, cnt, pay)


def _tile_tables(cnt, region, ntiles):
    nt = (cnt + MOE_TILE - 1) // MOE_TILE
    ends = jnp.cumsum(nt)
    starts = ends - nt
    total = ends[-1]
    i = jnp.minimum(jnp.arange(ntiles, dtype=jnp.int32), total - 1)
    g = jnp.sum((i[:, None] >= ends[None, :]).astype(jnp.int32), axis=1)
    blk = g * (region // MOE_TILE) + i - starts[g]
    return g.astype(jnp.int32), blk.astype(jnp.int32), total.reshape(1).astype(jnp.int32)


def _moe_up_kernel(tg_ref, tb_ref, nt_ref, x_ref, wg_ref, wu_ref, o_ref, wgs, wus):
    e = pl.program_id(0)
    i = pl.program_id(1)
    grp = tg_ref[i]
    changed = (i == 0) | (grp != tg_ref[jnp.maximum(i - 1, 0)])

    @pl.when(changed)
    def _():
        wgs[...] = wg_ref[...].astype(BF16)
        wus[...] = wu_ref[...].astype(BF16)

    @pl.when(i < nt_ref[0])
    def _():
        x = x_ref[:, :D_MODEL].astype(BF16)
        lane = lax.broadcasted_iota(jnp.int32, (1, LANES), 1)
        ge = jnp.sum(jnp.where(lane == grp * PER_GROUP + e, x_ref[:, D_MODEL:], 0.0), axis=-1, keepdims=True)
        a = _dot(x, wgs[...])
        u = _dot(x, wus[...])
        o_ref[...] = (a * _sigmoid(a) * u * ge).astype(BF16)


def _moe_up(tg, tb, nt, xs, wg, wu, l):
    rows = xs.shape[0]
    ntiles = tg.shape[0]
    d, de = wg.shape[2], wg.shape[3]
    wspec = pl.BlockSpec((None, None, d, de), lambda e, i, tg, tb, nt: (l, tg[i] * PER_GROUP + e, 0, 0))
    return pl.pallas_call(
        _moe_up_kernel,
        grid_spec=pltpu.PrefetchScalarGridSpec(
            num_scalar_prefetch=3, grid=(PER_GROUP, ntiles),
            in_specs=[pl.BlockSpec((MOE_TILE, PAY_W), lambda e, i, tg, tb, nt: (tb[i], 0)), wspec, wspec],
            out_specs=pl.BlockSpec((MOE_TILE, de), lambda e, i, tg, tb, nt: (tb[i], e)),
            scratch_shapes=[pltpu.VMEM((d, de), BF16), pltpu.VMEM((d, de), BF16)]),
        out_shape=jax.ShapeDtypeStruct((rows, PER_GROUP * de), BF16),
        compiler_params=_cparams("arbitrary", "arbitrary"), name="moe_up")(tg, tb, nt, xs, wg, wu)


def _moe_down_kernel(tg_ref, tb_ref, nt_ref, x_ref, w_ref, o_ref, ws):
    i = pl.program_id(1)
    changed = (i == 0) | (tg_ref[i] != tg_ref[jnp.maximum(i - 1, 0)])

    @pl.when(changed)
    def _():
        ws[...] = w_ref[...].astype(BF16)

    @pl.when(i < nt_ref[0])
    def _():
        o_ref[...] = _dot(x_ref[...], ws[...])


def _moe_down(tg, tb, nt, hid, wd, l, tn=1024):
    rows, k = hid.shape
    ntiles = tg.shape[0]
    d = wd.shape[3]
    return pl.pallas_call(
        _moe_down_kernel,
        grid_spec=pltpu.PrefetchScalarGridSpec(
            num_scalar_prefetch=3, grid=(d // tn, ntiles),
            in_specs=[pl.BlockSpec((MOE_TILE, k), lambda c, i, tg, tb, nt: (tb[i], 0)),
                      pl.BlockSpec((None, None, k, tn), lambda c, i, tg, tb, nt: (l, tg[i], 0, c))],
            out_specs=pl.BlockSpec((MOE_TILE, tn), lambda c, i, tg, tb, nt: (tb[i], c)),
            scratch_shapes=[pltpu.VMEM((k, tn), BF16)]),
        out_shape=jax.ShapeDtypeStruct((rows, d), F32),
        compiler_params=_cparams("arbitrary", "arbitrary"), name="moe_down")(tg, tb, nt, hid, wd)


def _gather_ln_kernel(dest_ref, y_ref, h_ref, g_ref, b_ref, of_ref, ob_ref, buf, sem, *, nsteps):
    tm = h_ref.shape[0]
    i = pl.program_id(0)

    def issue(step, slot):
        base = step * tm
        for r in range(tm):
            d = dest_ref[base + r]
            pltpu.make_async_copy(y_ref.at[pl.ds(d, 1)], buf.at[slot, pl.ds(r, 1)],
                                  sem.at[slot]).start(priority=r % 2)

    @pl.when(i == 0)
    def _():
        issue(0, 0)

    @pl.when(i + 1 < nsteps)
    def _():
        issue(i + 1, (i + 1) % 2)

    slot = i % 2
    pltpu.make_async_copy(y_ref.at[pl.ds(0, tm)], buf.at[slot], sem.at[slot]).wait()
    o = _layer_norm(ALPHA * h_ref[...] + buf[slot], g_ref[...], b_ref[...])
    of_ref[...] = o
    ob_ref[...] = o.astype(BF16)


def _gather_ln(dest, ys, hf, g, b, tm=256):
    n, d = hf.shape
    nsteps = n // tm
    row = pl.BlockSpec((tm, d), lambda i, dref: (i, 0))
    par = pl.BlockSpec((1, d), lambda i, dref: (0, 0))
    return pl.pallas_call(
        functools.partial(_gather_ln_kernel, nsteps=nsteps),
        grid_spec=pltpu.PrefetchScalarGridSpec(
            num_scalar_prefetch=1, grid=(nsteps,),
            in_specs=[pl.BlockSpec(memory_space=pl.ANY), row, par, par],
            out_specs=[row, row],
            scratch_shapes=[pltpu.VMEM((2, tm, d), F32), pltpu.SemaphoreType.DMA((2,))]),
        out_shape=[jax.ShapeDtypeStruct((n, d), F32), jax.ShapeDtypeStruct((n, d), BF16)],
        compiler_params=_cparams("arbitrary"), name="moe_gather_ln3")(
            dest, ys, hf, g.reshape(1, d), b.reshape(1, d))


def _moe(hf, p, w, l):
    n = hf.shape[0]
    region = n + MOE_TILE
    ntiles = n // MOE_TILE + N_GROUPS
    pay, dest, cnt = _route(hf, w["w_router_t"], w["router_bias_c"], region)
    dest = dest.reshape(n)
    cnt = cnt[:N_GROUPS, 0]
    tg, tb, nt = _tile_tables(cnt, region, ntiles)
    xs = _scatter(dest, cnt, pay, region)
    hid = _moe_up(tg, tb, nt, xs, p["moe_w_gate"], p["moe_w_up"], l)
    ys = _moe_down(tg, tb, nt, hid, w["moe_wd"], l)
    return _gather_ln(dest, ys, hf, p["ln3_g"][l], p["ln3_b"][l])


def _prep_params(p):
    L = p["w_in"].shape[0]
    w_in = p["w_in"]
    zeros = lambda w: jnp.zeros((L, D_MODEL, w), F32)
    o_mla = 2 * D_SG
    o_kr = o_mla + MLA_Q_RANK + MLA_KV_RANK
    o_gla = o_kr + MLA_ROPE
    gla_main = 2 * N_GLA * GLA_DK + 2 * N_GLA * GLA_DV
    o_ml = o_gla + gla_main + GLA_RANK
    ml_main = 2 * N_ML * ML_DK + 2 * N_ML * ML_DV
    o_gate = o_ml + ml_main + 2 * N_ML
    assert o_gate == GATE_COL
    half = MLA_ROPE // 2
    pad_r = LANES - MLA_ROPE
    w_gla = jnp.concatenate([w_in[..., o_gla:o_gla + gla_main + GLA_RANK], zeros(LANES - GLA_RANK)], axis=-1)
    w_ml = jnp.concatenate([w_in[..., o_ml:o_gate], zeros(LANES - 2 * N_ML)], axis=-1)

    wq = p["mla_w_uq"].reshape(L, MLA_Q_RANK, N_MLA, MLA_NOPE + MLA_ROPE)
    rq = wq[..., MLA_NOPE:]
    zq = jnp.zeros((L, MLA_Q_RANK, N_MLA, pad_r), F32)
    wuq = jnp.concatenate([wq[..., :MLA_NOPE], rq, zq, rq[..., half:], rq[..., :half], zq], axis=-1)
    wkv = p["mla_w_ukv"].reshape(L, MLA_KV_RANK, N_MLA, MLA_NOPE + MLA_V)

    bf = lambda a: a.astype(BF16)
    ne = p["moe_w_down"].shape[1]
    return dict(
        w_gla=bf(w_gla), w_ml=bf(w_ml),
        wuq=bf(wuq.reshape(L, MLA_Q_RANK, N_MLA * 3 * LANES)),
        wuk=bf(wkv[..., :MLA_NOPE].reshape(L, MLA_KV_RANK, N_MLA * MLA_NOPE)),
        wuv=bf(wkv[..., MLA_NOPE:].reshape(L, MLA_KV_RANK, N_MLA * MLA_V)),
        gla_wg=jnp.pad(p["gla_w_gate"], ((0, 0), (0, LANES - GLA_RANK), (0, 0))),
        w_branch=bf(jnp.pad(p["w_branch"],
                            ((0, 0), (0, 0), (0, 0), (GATE_LANE0, GATE_WINW - D_MODEL - GATE_LANE0)))),
        w_out=bf(jnp.pad(p["w_out"], ((0, 0), (GATE_LANE0, GATE_WINW - D_MODEL - GATE_LANE0), (0, 0)))),
        x_w_q=bf(p["x_w_q"]), x_w_kv=bf(p["x_w_kv"]), x_w_o=bf(p["x_w_o"]),
        w_router_t=jnp.pad(p["w_router"].T, ((0, LANES - N_EXPERTS), (0, 0))),
        router_bias_c=p["router_bias"].reshape(N_EXPERTS, 1),
        moe_wd=p["moe_w_down"].reshape(L, N_GROUPS, (ne // N_GROUPS) * D_EXPERT, D_MODEL),
    )


def _mixer(hf, hb, cos_p, sin_p, p, w, l, batch, seq):
    w_in = p["w_in"]
    z_sg = _mm_w32(hb, w_in, l, 0, 1, 2 * D_SG, F32, name="mm_sg")
    z_mla = _mm_w32(hb, w_in, l, 2 * D_SG, 3, 256, F32, tm=2048, name="mm_mla")
    z_gla = _mm(hb, w["w_gla"][l], F32, name="mm_gla")
    z_ml = _mm(hb, w["w_ml"][l], F32, name="mm_ml")
    gates = _mm_w32(hb, w_in, l, GATE_COL0, GATE_NBLK, GATE_TN, BF16, act="sigmoid", name="mm_gate")

    y_a = _sg(z_sg, p["sg_vnorm_g"][l], p["sg_vnorm_b"][l], p["sg_w_s"][l], p["sg_b_s"][l])
    q, k, v = _mla_prep(z_mla, cos_p, sin_p, p["mla_qnorm_g"][l], p["mla_kvnorm_g"][l],
                        w["wuq"][l], w["wuk"][l], w["wuv"][l])
    y_b = _mla_attn(q, k, v, batch, seq)
    y_c = _gla(z_gla, w["gla_wg"][l], p["gla_b_gate"][l], p["gla_norm_g"][l], batch, seq)
    if_cols = z_ml[:, -LANES:-LANES + 2 * N_ML]
    ift = if_cols.reshape(batch, seq, 2 * N_ML).transpose(0, 2, 1)
    y_d = _mlstm(z_ml, ift, p["ml_conv_w"][l], p["ml_conv_b"][l], p["ml_gate_b"][l], p["ml_norm_g"][l],
                 batch, seq)
    merged = _merge(y_a, y_b, y_c, y_d, gates, w["w_branch"][l])
    return _mm_res_ln(merged, w["w_out"][l], hf, p["ln1_g"][l], p["ln1_b"][l], name="out_ln1")


def _forward(p):
    x = p["x"]
    batch, seq, d = x.shape
    n = batch * seq
    mem = p["mem"]
    mem_len = mem.shape[1]
    w = _prep_params(p)
    posb = jnp.broadcast_to(p["positions"].reshape(n, 1).astype(F32), (n, LANES))
    cos_p, sin_p = _rope_tables(posb)
    memb = mem.reshape(batch * mem_len, d).astype(BF16)
    hf, hb = _ln(x.reshape(n, d), p["ln_in_g"], p["ln_in_b"])
    for l in range(p["w_in"].shape[0]):
        hf, hb = _mixer(hf, hb, cos_p, sin_p, p, w, l, batch, seq)
        kv = _mm(memb, w["x_w_kv"][l], BF16, tm=512, tn=1024, name="mm_xkv")
        hf, hb = _xattn(hb, hf, w["x_w_q"][l], kv, w["x_w_o"][l], p["ln2_g"][l], p["ln2_b"][l],
                        batch, seq, mem_len)
        hf, hb = _moe(hf, p, w, l)
    return hf.reshape(batch, seq, d)


def kernel(x, mem, positions, ln_in_g, ln_in_b, w_in, sg_vnorm_g, sg_vnorm_b, sg_w_s, sg_b_s, mla_qnorm_g, mla_kvnorm_g, mla_w_uq, mla_w_ukv, gla_w_gate, gla_b_gate, gla_norm_g, ml_conv_w, ml_conv_b, ml_gate_b, ml_norm_g, w_branch, w_out, ln1_g, ln1_b, x_w_q, x_w_kv, x_w_o, ln2_g, ln2_b, w_router, router_bias, moe_w_gate, moe_w_up, moe_w_down, ln3_g, ln3_b):
    return _forward(dict(
        x=x, mem=mem, positions=positions, ln_in_g=ln_in_g, ln_in_b=ln_in_b, w_in=w_in,
        sg_vnorm_g=sg_vnorm_g, sg_vnorm_b=sg_vnorm_b, sg_w_s=sg_w_s, sg_b_s=sg_b_s,
        mla_qnorm_g=mla_qnorm_g, mla_kvnorm_g=mla_kvnorm_g, mla_w_uq=mla_w_uq, mla_w_ukv=mla_w_ukv,
        gla_w_gate=gla_w_gate, gla_b_gate=gla_b_gate, gla_norm_g=gla_norm_g,
        ml_conv_w=ml_conv_w, ml_conv_b=ml_conv_b, ml_gate_b=ml_gate_b, ml_norm_g=ml_norm_g,
        w_branch=w_branch, w_out=w_out, ln1_g=ln1_g, ln1_b=ln1_b,
        x_w_q=x_w_q, x_w_kv=x_w_kv, x_w_o=x_w_o, ln2_g=ln2_g, ln2_b=ln2_b,
        w_router=w_router, router_bias=router_bias,
        moe_w_gate=moe_w_gate, moe_w_up=moe_w_up, moe_w_down=moe_w_down, ln3_g=ln3_g, ln3_b=ln3_b))
```

```python
import functools
import math

import jax
import jax.numpy as jnp
from jax import lax
from jax.experimental import pallas as pl
from jax.experimental.pallas import tpu as pltpu

F32 = jnp.float32
BF16 = jnp.bfloat16

D_MODEL = 2048
DEPTH = 4
EPS = 1e-5
ALPHA = (2.0 * DEPTH) ** 0.25

SG_CHUNK = 128
N_SG = 4
D_SG = 512
N_MLA = 4
MLA_Q_RANK = 384
MLA_KV_RANK = 256
MLA_NOPE = 128
MLA_ROPE = 64
MLA_V = 128
MLA_QK = 256
ROPE_BASE = 10000.0
N_GLA = 4
GLA_DK = 64
GLA_DV = 128
GLA_RANK = 16
GLA_TAU = 16.0
GLA_CHUNK = 64
N_ML = 4
ML_DK = 64
ML_DV = 128
ML_CHUNK = 128
N_X = 4
X_HEAD = 128
N_EXPERTS = 16
D_EXPERT = 512

LANES = 128
VMEM_LIMIT = 48 * 1024 * 1024

_NT = (((1,), (1,)), ((), ()))
_TN = (((0,), (0,)), ((), ()))


def _cparams(*sem):
    return pltpu.CompilerParams(dimension_semantics=sem, vmem_limit_bytes=VMEM_LIMIT)


def _dot(a, b):
    return jnp.dot(a, b, preferred_element_type=F32)


def _dotg(a, b, dims):
    return lax.dot_general(a, b, dims, preferred_element_type=F32)


def _split3(a):
    a1 = a.astype(BF16)
    r1 = a - a1.astype(F32)
    a2 = r1.astype(BF16)
    a3 = (r1 - a2.astype(F32)).astype(BF16)
    return a1, a2, a3


def _dot_exact_rhs(a, ones_bf16):
    a1, a2, a3 = _split3(a)
    return _dot(a1, ones_bf16) + _dot(a2, ones_bf16) + _dot(a3, ones_bf16)


def _dot_exact_lhs(ones_bf16, a):
    a1, a2, a3 = _split3(a)
    return _dot(ones_bf16, a1) + _dot(ones_bf16, a2) + _dot(ones_bf16, a3)


def _dot_hi(a, b):
    a1 = a.astype(BF16)
    a2 = (a - a1.astype(F32)).astype(BF16)
    b1 = b.astype(BF16)
    b2 = (b - b1.astype(F32)).astype(BF16)
    return _dot(a1, b1) + _dot(a2, b1) + _dot(a1, b2)


def _sigmoid(x):
    return 1.0 / (1.0 + jnp.exp(-x))


def _log_sigmoid(x):
    return jnp.minimum(x, 0.0) - jnp.log(1.0 + jnp.exp(-jnp.abs(x)))


def _layer_norm(t, g, b):
    mu = jnp.mean(t, axis=-1, keepdims=True)
    c = t - mu
    var = jnp.mean(c * c, axis=-1, keepdims=True)
    return c * lax.rsqrt(var + EPS) * g + b


def _rms_norm(t, g):
    return t * lax.rsqrt(jnp.mean(t * t, axis=-1, keepdims=True) + EPS) * g


def _ln_kernel(x_ref, g_ref, b_ref, of_ref, ob_ref):
    y = _layer_norm(x_ref[...], g_ref[...], b_ref[...])
    of_ref[...] = y
    ob_ref[...] = y.astype(BF16)


def _ln(x, g, b, tm=256):
    n, d = x.shape
    row = pl.BlockSpec((tm, d), lambda i: (i, 0))
    par = pl.BlockSpec((1, d), lambda i: (0, 0))
    return pl.pallas_call(
        _ln_kernel, grid=(n // tm,), in_specs=[row, par, par], out_specs=[row, row],
        out_shape=[jax.ShapeDtypeStruct((n, d), F32), jax.ShapeDtypeStruct((n, d), BF16)],
        compiler_params=_cparams("parallel"), name="ln_in")(x, g.reshape(1, d), b.reshape(1, d))


def _mm_kernel(x_ref, w_ref, o_ref, *, act):
    acc = _dot(x_ref[...], w_ref[...])
    if act == "sigmoid":
        acc = _sigmoid(acc)
    o_ref[...] = acc.astype(o_ref.dtype)


def _mm(x, w, out_dtype, act=None, tm=512, tn=None, name="mm"):
    n, k = x.shape
    m = w.shape[1]
    tn = m if tn is None else tn
    return pl.pallas_call(
        functools.partial(_mm_kernel, act=act), grid=(n // tm, m // tn),
        in_specs=[pl.BlockSpec((tm, k), lambda i, j: (i, 0)),
                  pl.BlockSpec((k, tn), lambda i, j: (0, j))],
        out_specs=pl.BlockSpec((tm, tn), lambda i, j: (i, j)),
        out_shape=jax.ShapeDtypeStruct((n, m), out_dtype),
        compiler_params=_cparams("parallel", "parallel"), name=name)(x, w)


def _mm_wt_kernel(x_ref, w_ref, o_ref, ws, *, act):
    @pl.when(pl.program_id(1) == 0)
    def _():
        ws[...] = w_ref[0].astype(BF16)

    acc = _dotg(x_ref[...], ws[...], _NT)
    if act == "sigmoid":
        acc = _sigmoid(acc)
    o_ref[...] = acc.astype(o_ref.dtype)


def _mm_wt(x, wt_all, l, row0, nblk, tn, out_dtype, act=None, tm=1024, name="mm_wt"):
    n, k = x.shape
    assert row0 % 8 == 0
    wmode = dict(pipeline_mode=pl.Buffered(1)) if nblk == 1 else {}
    return pl.pallas_call(
        functools.partial(_mm_wt_kernel, act=act), grid=(nblk, n // tm),
        in_specs=[pl.BlockSpec((tm, k), lambda j, i: (i, 0)),
                  pl.BlockSpec((pl.Element(1), pl.Element(tn), pl.Element(k)),
                               lambda j, i: (l, pl.multiple_of(row0 + j * tn, 8), 0), **wmode)],
        out_specs=pl.BlockSpec((tm, tn), lambda j, i: (i, j)),
        out_shape=jax.ShapeDtypeStruct((n, nblk * tn), out_dtype),
        scratch_shapes=[pltpu.VMEM((tn, k), BF16)],
        compiler_params=_cparams("arbitrary", "arbitrary"), name=name)(x, wt_all)


def _mm_res_ln_kernel(x_ref, w_ref, h_ref, g_ref, b_ref, of_ref, ob_ref, *acc, nk):
    def finish(y):
        t = ALPHA * h_ref[...] + y
        o = _layer_norm(t, g_ref[...], b_ref[...])
        of_ref[...] = o
        ob_ref[...] = o.astype(BF16)

    if nk == 1:
        finish(_dot(x_ref[...], w_ref[...]))
        return
    acc_ref, = acc
    kk = pl.program_id(1)

    @pl.when(kk == 0)
    def _():
        acc_ref[...] = jnp.zeros_like(acc_ref)

    acc_ref[...] += _dot(x_ref[...], w_ref[...])

    @pl.when(kk == nk - 1)
    def _():
        finish(acc_ref[...])


def _mm_res_ln(x, w, h, g, b, tm=256, tk=None, name="mm_res_ln"):
    n, k = x.shape
    d = w.shape[1]
    tk = k if tk is None else tk
    nk = k // tk
    row = pl.BlockSpec((tm, d), lambda i, j: (i, 0))
    par = pl.BlockSpec((1, d), lambda i, j: (0, 0))
    scratch = [] if nk == 1 else [pltpu.VMEM((tm, d), F32)]
    return pl.pallas_call(
        functools.partial(_mm_res_ln_kernel, nk=nk), grid=(n // tm, nk),
        in_specs=[pl.BlockSpec((tm, tk), lambda i, j: (i, j)),
                  pl.BlockSpec((tk, d), lambda i, j: (j, 0)), row, par, par],
        out_specs=[row, row],
        out_shape=[jax.ShapeDtypeStruct((n, d), F32), jax.ShapeDtypeStruct((n, d), BF16)],
        scratch_shapes=scratch,
        compiler_params=_cparams("parallel", "arbitrary"), name=name)(
            x, w, h, g.reshape(1, d), b.reshape(1, d))


def _sg_kernel(z_ref, vg_ref, vb_ref, ws_ref, bst_ref, o_ref, *, nchunk):
    z = z_ref[...]
    z = 0.5 * z * (1.0 + jnp.tanh(math.sqrt(2.0 / math.pi) * (z + 0.044715 * (z * z * z))))
    u = z[:, :D_SG]
    vn = _layer_norm(z[:, D_SG:], vg_ref[...], vb_ref[...]).astype(BF16)
    r = lax.broadcasted_iota(jnp.int32, (SG_CHUNK, SG_CHUNK), 0)
    c = lax.broadcasted_iota(jnp.int32, (SG_CHUNK, SG_CHUNK), 1)
    causal = c <= r
    gw = SG_CHUNK
    for g in range(N_SG):
        w = jnp.where(causal, ws_ref[g], 0.0).astype(BF16)
        bias = bst_ref[:, g:g + 1]
        for ci in range(nchunk):
            rs = slice(ci * SG_CHUNK, (ci + 1) * SG_CHUNK)
            cs = slice(g * gw, (g + 1) * gw)
            mixed = _dot(w, vn[rs, cs]) + bias
            o_ref[rs, cs] = (u[rs, cs] * mixed).astype(BF16)


def _sg(z, vg, vb, ws, bs, tm=512):
    n = z.shape[0]
    return pl.pallas_call(
        functools.partial(_sg_kernel, nchunk=tm // SG_CHUNK), grid=(n // tm,),
        in_specs=[pl.BlockSpec((tm, 2 * D_SG), lambda i: (i, 0)),
                  pl.BlockSpec((1, D_SG), lambda i: (0, 0)),
                  pl.BlockSpec((1, D_SG), lambda i: (0, 0)),
                  pl.BlockSpec((N_SG, SG_CHUNK, SG_CHUNK), lambda i: (0, 0, 0)),
                  pl.BlockSpec((SG_CHUNK, N_SG), lambda i: (0, 0))],
        out_specs=pl.BlockSpec((tm, D_SG), lambda i: (i, 0)),
        out_shape=jax.ShapeDtypeStruct((n, D_SG), BF16),
        compiler_params=_cparams("parallel"), name="sg")(
            z, vg.reshape(1, D_SG), vb.reshape(1, D_SG), ws, bs.T)


def _rope_kernel(pos_ref, cos_ref, sin_ref):
    half = MLA_ROPE // 2
    lane = lax.broadcasted_iota(jnp.int32, (1, LANES), 1)
    idx = jnp.bitwise_and(lane, half - 1).astype(F32)
    freq = jnp.exp(idx * (-math.log(ROPE_BASE) / half))
    ang = pos_ref[...] * freq
    c = jnp.cos(ang)
    s = jnp.sin(ang)
    cos_ref[...] = jnp.where(lane < MLA_ROPE, c, 0.0)
    sin_ref[...] = jnp.where(lane < half, -s, jnp.where(lane < MLA_ROPE, s, 0.0))


def _rope_tables(posb, tm=512):
    n = posb.shape[0]
    row = pl.BlockSpec((tm, LANES), lambda i: (i, 0))
    return pl.pallas_call(
        _rope_kernel, grid=(n // tm,), in_specs=[row], out_specs=[row, row],
        out_shape=[jax.ShapeDtypeStruct((n, LANES), F32)] * 2,
        compiler_params=_cparams("parallel"), name="rope_tables")(posb)


def _mla_prep_kernel(z_ref, cos_ref, sin_ref, qg_ref, kvg_ref, wuq_ref, wuk_ref, wuv_ref,
                     q_ref, k_ref, v_ref):
    z = z_ref[...]
    cq = _rms_norm(z[:, :MLA_Q_RANK], qg_ref[...]).astype(BF16)
    o1 = MLA_Q_RANK + MLA_KV_RANK
    ckv = _rms_norm(z[:, MLA_Q_RANK:o1], kvg_ref[...]).astype(BF16)
    cos_p = cos_ref[...]
    sin_p = sin_ref[...]
    kr = z[:, o1:o1 + LANES]
    half = MLA_ROPE // 2
    lane = lax.broadcasted_iota(jnp.int32, (1, LANES), 1)
    kr_swapped = jnp.where(lane < half, pltpu.roll(kr, LANES - half, 1), pltpu.roll(kr, half, 1))
    k_tail = (kr * cos_p + kr_swapped * sin_p).astype(BF16)
    qa = _dot(cq, wuq_ref[...])
    kn = _dot(ckv, wuk_ref[...])
    v_ref[...] = _dot(ckv, wuv_ref[...]).astype(BF16)
    scale = (MLA_NOPE + MLA_ROPE) ** -0.5
    for h in range(N_MLA):
        b0 = h * 3 * LANES
        q_tail = qa[:, b0 + LANES:b0 + 2 * LANES] * cos_p + qa[:, b0 + 2 * LANES:b0 + 3 * LANES] * sin_p
        q_ref[:, h * MLA_QK:h * MLA_QK + LANES] = (qa[:, b0:b0 + LANES] * scale).astype(BF16)
        q_ref[:, h * MLA_QK + LANES:(h + 1) * MLA_QK] = (q_tail * scale).astype(BF16)
        k_ref[:, h * MLA_QK:h * MLA_QK + LANES] = kn[:, h * LANES:(h + 1) * LANES].astype(BF16)
        k_ref[:, h * MLA_QK + LANES:(h + 1) * MLA_QK] = k_tail


def _mla_prep(z, cos_p, sin_p, qg, kvg, wuq, wuk, wuv, tm=512):
    n, zw = z.shape
    row = lambda w: pl.BlockSpec((tm, w), lambda i: (i, 0))
    full = lambda a: pl.BlockSpec(a.shape, lambda i: (0,) * a.ndim)
    qg = qg.reshape(1, -1)
    kvg = kvg.reshape(1, -1)
    return pl.pallas_call(
        _mla_prep_kernel, grid=(n // tm,),
        in_specs=[row(zw), row(LANES), row(LANES), full(qg), full(kvg), full(wuq), full(wuk), full(wuv)],
        out_specs=[row(N_MLA * MLA_QK), row(N_MLA * MLA_QK), row(N_MLA * MLA_V)],
        out_shape=[jax.ShapeDtypeStruct((n, N_MLA * MLA_QK), BF16),
                   jax.ShapeDtypeStruct((n, N_MLA * MLA_QK), BF16),
                   jax.ShapeDtypeStruct((n, N_MLA * MLA_V), BF16)],
        compiler_params=_cparams("parallel"), name="mla_prep")(z, cos_p, sin_p, qg, kvg, wuq, wuk, wuv)


def _mla_attn_kernel(q_ref, k_ref, v_ref, o_ref, *, t, nblk):
    krow = lax.broadcasted_iota(jnp.int32, (t, t), 0)
    qcol = lax.broadcasted_iota(jnp.int32, (t, t), 1)
    visible = krow <= qcol
    for i in range(nblk):
        q = q_ref[i * t:(i + 1) * t, :]
        m = jnp.full((1, t), -jnp.inf, F32)
        l = jnp.zeros((1, t), F32)
        acc = jnp.zeros((MLA_V, t), F32)
        for j in range(i + 1):
            ks = slice(j * t, (j + 1) * t)
            s = _dotg(k_ref[ks, :], q, _NT)
            if j == i:
                s = jnp.where(visible, s, -jnp.inf)
            m_new = jnp.maximum(m, jnp.max(s, axis=0, keepdims=True))
            p = jnp.exp(s - m_new)
            a = jnp.exp(m - m_new)
            l = a * l + jnp.sum(p, axis=0, keepdims=True)
            acc = a * acc + _dotg(v_ref[ks, :], p.astype(BF16), _TN)
            m = m_new
        o_ref[i * t:(i + 1) * t, :] = (acc / l).T.astype(BF16)


def _mla_attn(q, k, v, batch, seq, t=512):
    n = q.shape[0]
    return pl.pallas_call(
        functools.partial(_mla_attn_kernel, t=t, nblk=seq // t), grid=(batch, N_MLA),
        in_specs=[pl.BlockSpec((seq, MLA_QK), lambda b, h: (b, h)),
                  pl.BlockSpec((seq, MLA_QK), lambda b, h: (b, h)),
                  pl.BlockSpec((seq, MLA_V), lambda b, h: (b, h))],
        out_specs=pl.BlockSpec((seq, MLA_V), lambda b, h: (b, h)),
        out_shape=jax.ShapeDtypeStruct((n, N_MLA * MLA_V), BF16),
        compiler_params=_cparams("parallel", "parallel"), name="mla_attn")(q, k, v)


def _gla_kernel(q_ref, k_ref, v_ref, og_ref, lr_ref, wg_ref, bg_ref, ng_ref, o_ref, st_ref, *, nchunk):
    L = GLA_CHUNK
    qkw = N_GLA * GLA_DK

    @pl.when(pl.program_id(1) == 0)
    def _():
        st_ref[...] = jnp.zeros_like(st_ref)

    logits = _dot_hi(lr_ref[...], wg_ref[...]) + bg_ref[...]
    log_a = _log_sigmoid(logits) * (1.0 / GLA_TAU)
    lane = lax.broadcasted_iota(jnp.int32, (1, qkw), 1)
    masks = [((lane >= h * GLA_DK) & (lane < (h + 1) * GLA_DK)).astype(F32) for h in range(N_GLA)]
    r = lax.broadcasted_iota(jnp.int32, (L, L), 0)
    c = lax.broadcasted_iota(jnp.int32, (L, L), 1)
    causal = c <= r
    tril = jnp.where(causal, 1.0, 0.0).astype(BF16)
    ng = ng_ref[...]
    st = st_ref[...]
    for ci in range(nchunk):
        rs = slice(ci * L, (ci + 1) * L)
        b = _dot_exact_lhs(tril, log_a[rs])
        b_last = b[L - 1:L, :]
        q = q_ref[rs, :] * (GLA_DK ** -0.5)
        k = k_ref[rs, :]
        qt = q * jnp.exp(b)
        kt = (k * jnp.exp(-b)).astype(BF16)
        kd = (k * jnp.exp(b_last - b)).astype(BF16)
        qstack = jnp.concatenate([qt * masks[h] for h in range(N_GLA)], axis=0).astype(BF16)
        att = _dotg(qstack, kt, _NT)
        inter = _dotg(qstack, st.astype(BF16), _NT)
        vb = v_ref[rs, :].astype(BF16)
        for h in range(N_GLA):
            hs = slice(h * L, (h + 1) * L)
            vs = slice(h * GLA_DV, (h + 1) * GLA_DV)
            a_h = jnp.where(causal, att[hs], 0.0).astype(BF16)
            o_h = _rms_norm(_dot(a_h, vb[:, vs]) + inter[hs], ng)
            g = og_ref[rs, vs]
            o_ref[rs, vs] = (o_h * (g * _sigmoid(g))).astype(BF16)
        upd = _dotg(vb, kd, _TN)
        new = st * jnp.exp(b_last)
        for h in range(N_GLA):
            new = new + upd[h * GLA_DV:(h + 1) * GLA_DV] * masks[h]
        st = new
    st_ref[...] = st


def _gla(z, wg, bg, ng, batch, seq, tm=256):
    n = z.shape[0]
    ns = seq // tm
    qkw = N_GLA * GLA_DK
    vw = N_GLA * GLA_DV
    col = lambda w, j: pl.BlockSpec((tm, w), lambda b, s: (b * ns + s, j))
    full = lambda a: pl.BlockSpec(a.shape, lambda b, s: (0,) * a.ndim)
    bg = bg.reshape(1, qkw)
    ng = ng.reshape(1, GLA_DV)
    return pl.pallas_call(
        functools.partial(_gla_kernel, nchunk=tm // GLA_CHUNK), grid=(batch, ns),
        in_specs=[col(qkw, 0), col(qkw, 1), col(vw, 1), col(vw, 2), col(LANES, (2 * qkw + 2 * vw) // LANES),
                  full(wg), full(bg), full(ng)],
        out_specs=pl.BlockSpec((tm, vw), lambda b, s: (b * ns + s, 0)),
        out_shape=jax.ShapeDtypeStruct((n, vw), BF16),
        scratch_shapes=[pltpu.VMEM((GLA_DV, qkw), F32)],
        compiler_params=_cparams("parallel", "arbitrary"), name="gla")(z, z, z, z, z, wg, bg, ng)


def _mlstm_kernel(qk_ref, v_ref, op_ref, if_ref, ift_ref, cw_ref, cb_ref, gb_ref, gbt_ref, ng_ref,
                  o_ref, ct_ref, n_ref, mw_ref, mk_ref, tail_ref, *, nchunk):
    L = ML_CHUNK
    qkw = N_ML * ML_DK
    tm = nchunk * L

    @pl.when(pl.program_id(1) == 0)
    def _():
        ct_ref[...] = jnp.zeros_like(ct_ref)
        n_ref[...] = jnp.zeros_like(n_ref)
        mw_ref[...] = jnp.zeros_like(mw_ref)
        mk_ref[...] = jnp.zeros_like(mk_ref)
        tail_ref[...] = jnp.zeros_like(tail_ref)

    x = qk_ref[...]
    tail = tail_ref[...]
    row8 = lax.broadcasted_iota(jnp.int32, (8, 2 * qkw), 0)
    acc = x * cw_ref[3:4, :] + cb_ref[...]
    for j in range(1, 4):
        rx = pltpu.roll(x, j, 0)
        fix = jnp.where(row8 < j, pltpu.roll(tail, j, 0), rx[0:8])
        acc = acc + jnp.concatenate([fix, rx[8:]], axis=0) * cw_ref[3 - j:4 - j, :]
    tail_ref[...] = x[tm - 8:tm]
    y = acc * _sigmoid(acc)
    q = y[:, :qkw] * (ML_DK ** -0.5)
    k = y[:, qkw:]

    gates = if_ref[...] + gb_ref[...]
    fc = _log_sigmoid(gates)
    gt = ift_ref[0] + gbt_ref[...]
    fct = _log_sigmoid(gt)

    lane = lax.broadcasted_iota(jnp.int32, (1, qkw), 1)
    masks = [((lane >= h * ML_DK) & (lane < (h + 1) * ML_DK)).astype(F32) for h in range(N_ML)]
    r = lax.broadcasted_iota(jnp.int32, (L, L), 0)
    c = lax.broadcasted_iota(jnp.int32, (L, L), 1)
    causal = c <= r
    tril = jnp.where(causal, 1.0, 0.0).astype(BF16)
    triu = jnp.where(r <= c, 1.0, 0.0).astype(BF16)

    def selector(width, block, first):
        rr = lax.broadcasted_iota(jnp.int32, (LANES, width), 0)
        cc = lax.broadcasted_iota(jnp.int32, (LANES, width), 1)
        return jnp.where(rr == (cc >> int(math.log2(block))) + first, 1.0, 0.0).astype(BF16)

    wide = N_ML * L
    fcb = _dot_exact_rhs(fc, selector(wide, L, N_ML))
    icb = _dot_exact_rhs(gates, selector(wide, L, 0))
    fck = _dot_exact_rhs(fc, selector(qkw, ML_DK, N_ML))
    ick = _dot_exact_rhs(gates, selector(qkw, ML_DK, 0))

    ng = ng_ref[...]
    ct = ct_ref[...]
    nrow = n_ref[...]
    mwide = mw_ref[...]
    mk = mk_ref[...]

    for ci in range(nchunk):
        rs = slice(ci * L, (ci + 1) * L)
        bb = _dot_exact_lhs(tril, fcb[rs])
        bk = _dot_exact_lhs(tril, fck[rs])
        brow = _dot_exact_rhs(fct[:, rs], triu)
        rowterm = gt[:, rs] - pltpu.roll(brow, N_ML, 0)
        qc = q[rs]
        kc = k[rs]
        log_d = jnp.concatenate(
            [jnp.where(causal, bb[:, h * L:(h + 1) * L] + rowterm[h:h + 1, :], -jnp.inf) for h in range(N_ML)],
            axis=0)
        log_inter = jnp.concatenate(
            [bb[:, h * L:(h + 1) * L] + mwide[:, h * L:(h + 1) * L] for h in range(N_ML)], axis=0)
        m_t = jnp.maximum(log_inter, jnp.max(log_d, axis=-1, keepdims=True))
        w_inter = jnp.exp(log_inter - m_t)
        qst = jnp.concatenate([qc * masks[h] for h in range(N_ML)], axis=0)
        qsb = qst.astype(BF16)
        s_all = _dotg(qsb, kc.astype(BF16), _NT) * jnp.exp(log_d - m_t)
        sb = s_all.astype(BF16)
        vb = v_ref[rs, :].astype(BF16)
        num = jnp.concatenate(
            [_dot(sb[h * L:(h + 1) * L], vb[:, h * ML_DV:(h + 1) * ML_DV]) for h in range(N_ML)], axis=0)
        num = num + w_inter * _dotg(qsb, ct.astype(BF16), _NT)
        den = jnp.sum(s_all, axis=-1, keepdims=True) + w_inter * jnp.sum(qst * nrow, axis=-1, keepdims=True)
        hh = _rms_norm(num / jnp.maximum(jnp.abs(den), jnp.exp(-m_t)), ng)
        for h in range(N_ML):
            vs = slice(h * ML_DV, (h + 1) * ML_DV)
            o_ref[rs, vs] = (hh[h * L:(h + 1) * L] * _sigmoid(op_ref[rs, vs])).astype(BF16)
        bl_w = bb[L - 1:L, :]
        lw_w = bl_w - bb + icb[rs]
        mwide_new = jnp.maximum(bl_w + mwide, jnp.max(lw_w, axis=0, keepdims=True))
        bl_k = bk[L - 1:L, :]
        lw_k = bl_k - bk + ick[rs]
        mk_new = jnp.maximum(bl_k + mk, jnp.max(lw_k, axis=0, keepdims=True))
        decay = jnp.exp(bl_k + mk - mk_new)
        kw = kc * jnp.exp(lw_k - mk_new)
        upd = _dotg(vb, kw.astype(BF16), _TN)
        ct = ct * decay
        for h in range(N_ML):
            ct = ct + upd[h * ML_DV:(h + 1) * ML_DV] * masks[h]
        nrow = nrow * decay + jnp.sum(kw, axis=0, keepdims=True)
        mwide = mwide_new
        mk = mk_new

    ct_ref[...] = ct
    n_ref[...] = nrow
    mw_ref[...] = mwide
    mk_ref[...] = mk


def _mlstm(z, ift, cw, cb, gb, ng, batch, seq, tm=512):
    n = z.shape[0]
    ns = seq // tm
    qkw = N_ML * ML_DK
    vw = N_ML * ML_DV
    col = lambda w, j: pl.BlockSpec((tm, w), lambda b, s: (b * ns + s, j))
    full = lambda a: pl.BlockSpec(a.shape, lambda b, s: (0,) * a.ndim)
    cb = cb.reshape(1, 2 * qkw)
    gbp = jnp.pad(gb, (0, LANES - 2 * N_ML)).reshape(1, LANES)
    gbt = gb.reshape(2 * N_ML, 1)
    ng = ng.reshape(1, ML_DV)
    return pl.pallas_call(
        functools.partial(_mlstm_kernel, nchunk=tm // ML_CHUNK), grid=(batch, ns),
        in_specs=[col(2 * qkw, 0), col(vw, 1), col(vw, 2), col(LANES, (2 * qkw + 2 * vw) // LANES),
                  pl.BlockSpec((1, 2 * N_ML, tm), lambda b, s: (b, 0, s)),
                  full(cw), full(cb), full(gbp), full(gbt), full(ng)],
        out_specs=pl.BlockSpec((tm, vw), lambda b, s: (b * ns + s, 0)),
        out_shape=jax.ShapeDtypeStruct((n, vw), BF16),
        scratch_shapes=[pltpu.VMEM((ML_DV, qkw), F32), pltpu.VMEM((1, qkw), F32),
                        pltpu.VMEM((1, N_ML * ML_CHUNK), F32), pltpu.VMEM((1, qkw), F32),
                        pltpu.VMEM((8, 2 * qkw), F32)],
        compiler_params=_cparams("parallel", "arbitrary"), name="mlstm")(
            z, z, z, z, ift, cw, cb, gbp, gbt, ng)


def _merge_kernel(ya_ref, yb_ref, yc_ref, yd_ref, g_ref, wb_ref, o_ref):
    acc = None
    for i, y_ref in enumerate((ya_ref, yb_ref, yc_ref, yd_ref)):
        p = _dot(y_ref[...], wb_ref[i])
        t = g_ref[:, i * D_MODEL:(i + 1) * D_MODEL].astype(F32) * p
        acc = t if acc is None else acc + t
    o_ref[...] = acc.astype(BF16)


def _merge(ya, yb, yc, yd, gates, wb, tm=256):
    n, bw = ya.shape
    row = pl.BlockSpec((tm, bw), lambda i: (i, 0))
    return pl.pallas_call(
        _merge_kernel, grid=(n // tm,),
        in_specs=[row, row, row, row, pl.BlockSpec((tm, 4 * D_MODEL), lambda i: (i, 0)),
                  pl.BlockSpec(wb.shape, lambda i: (0, 0, 0))],
        out_specs=pl.BlockSpec((tm, D_MODEL), lambda i: (i, 0)),
        out_shape=jax.ShapeDtypeStruct((n, D_MODEL), BF16),
        compiler_params=_cparams("parallel"), name="merge")(ya, yb, yc, yd, gates, wb)


def _xattn_kernel(hb_ref, hf_ref, wq_ref, k_ref, v_ref, wo_ref, g_ref, b_ref, of_ref, ob_ref):
    q = (_dot(hb_ref[...], wq_ref[...]) * (X_HEAD ** -0.5)).astype(BF16)
    outs = []
    for h in range(N_X):
        hs = slice(h * X_HEAD, (h + 1) * X_HEAD)
        s = _dotg(q[:, hs], k_ref[:, hs], _NT)
        p = jnp.exp(s - jnp.max(s, axis=-1, keepdims=True))
        l = jnp.sum(p, axis=-1, keepdims=True)
        outs.append((_dot(p.astype(BF16), v_ref[:, hs]) / l).astype(BF16))
    y = _dot(jnp.concatenate(outs, axis=-1), wo_ref[...])
    o = _layer_norm(ALPHA * hf_ref[...] + y, g_ref[...], b_ref[...])
    of_ref[...] = o
    ob_ref[...] = o.astype(BF16)


def _xattn(hb, hf, wq, kv, wo, g, b, batch, seq, mem_len, tm=512):
    n, d = hf.shape
    ns = seq // tm
    xw = N_X * X_HEAD
    row = pl.BlockSpec((tm, d), lambda bb, s: (bb * ns + s, 0))
    par = pl.BlockSpec((1, d), lambda bb, s: (0, 0))
    return pl.pallas_call(
        _xattn_kernel, grid=(batch, ns),
        in_specs=[row, row, pl.BlockSpec((d, xw), lambda bb, s: (0, 0)),
                  pl.BlockSpec((mem_len, xw), lambda bb, s: (bb, 0)),
                  pl.BlockSpec((mem_len, xw), lambda bb, s: (bb, 1)),
                  pl.BlockSpec((xw, d), lambda bb, s: (0, 0)), par, par],
        out_specs=[row, row],
        out_shape=[jax.ShapeDtypeStruct((n, d), F32), jax.ShapeDtypeStruct((n, d), BF16)],
        compiler_params=_cparams("parallel", "parallel"), name="xattn")(
            hb, hf, wq, kv, kv, wo, g.reshape(1, d), b.reshape(1, d))


N_GROUPS = 4
PER_GROUP = N_EXPERTS // N_GROUPS
MOE_TILE = 512
PAY_W = D_MODEL + LANES


def _route_kernel(h_ref, wrt_ref, rbc_ref, pay_ref, dest_ref, cnt_ref, carry_ref, *, region):
    tm = h_ref.shape[0]

    @pl.when(pl.program_id(0) == 0)
    def _():
        carry_ref[...] = jnp.zeros_like(carry_ref)

    h = h_ref[...]
    w = wrt_ref[...]
    w1 = w.astype(BF16)
    w2 = (w - w1.astype(F32)).astype(BF16)
    h1 = h.astype(BF16)
    h2 = (h - h1.astype(F32)).astype(BF16)
    logits = _dotg(w1, h1, _NT) + _dotg(w2, h1, _NT) + _dotg(w1, h2, _NT)
    aff = _sigmoid(logits[0:N_EXPERTS])
    biased = aff + rbc_ref[...]
    row = lax.broadcasted_iota(jnp.int32, (N_EXPERTS, 1), 0).astype(F32)
    big = float(LANES)
    best = e1 = e2 = None
    for g in range(N_GROUPS):
        x = jnp.where((row >= g * PER_GROUP) & (row < (g + 1) * PER_GROUP), biased, -jnp.inf)
        m1 = jnp.max(x, axis=0, keepdims=True)
        i1 = jnp.min(jnp.where(x == m1, row, big), axis=0, keepdims=True)
        x2 = jnp.where(row == i1, -jnp.inf, x)
        m2 = jnp.max(x2, axis=0, keepdims=True)
        i2 = jnp.min(jnp.where(x2 == m2, row, big), axis=0, keepdims=True)
        score = m1 + m2
        if g == 0:
            best, e1, e2 = score, i1, i2
        else:
            better = score > best
            best = jnp.where(better, score, best)
            e1 = jnp.where(better, i1, e1)
            e2 = jnp.where(better, i2, e2)
    s1 = jnp.sum(jnp.where(row == e1, aff, 0.0), axis=0, keepdims=True)
    s2 = jnp.sum(jnp.where(row == e2, aff, 0.0), axis=0, keepdims=True)
    tot = s1 + s2
    gates_t = jnp.where(row == e1, s1 / tot, 0.0) + jnp.where(row == e2, s2 / tot, 0.0)

    grp = jnp.zeros_like(e1)
    for g in range(1, N_GROUPS):
        grp = grp + jnp.where(e1 >= g * PER_GROUP, 1.0, 0.0)
    row8 = lax.broadcasted_iota(jnp.int32, (8, 1), 0).astype(F32)
    onehot = jnp.where(row8 == grp, 1.0, 0.0)
    r = lax.broadcasted_iota(jnp.int32, (tm, tm), 0)
    c = lax.broadcasted_iota(jnp.int32, (tm, tm), 1)
    earlier = jnp.where(r < c, 1.0, 0.0).astype(BF16)
    rank_in = _dot(onehot.astype(BF16), earlier)
    carry = carry_ref[...]
    rank = jnp.sum(onehot * (rank_in + carry[:, 0:1]), axis=0, keepdims=True)
    dest_ref[...] = (grp * float(region) + rank).astype(jnp.int32)
    carry = carry + jnp.sum(onehot, axis=1, keepdims=True)
    carry_ref[...] = carry
    cnt_ref[...] = carry.astype(jnp.int32)

    gates = jnp.concatenate([gates_t, jnp.zeros((LANES - N_EXPERTS, tm), F32)], axis=0).T
    pay_ref[:, :D_MODEL] = h
    pay_ref[:, D_MODEL:] = gates


def _route(hf, wrt, rbc, region, tm=512):
    n, d = hf.shape
    return pl.pallas_call(
        functools.partial(_route_kernel, region=region), grid=(n // tm,),
        in_specs=[pl.BlockSpec((tm, d), lambda i: (i, 0)),
                  pl.BlockSpec((LANES, d), lambda i: (0, 0)),
                  pl.BlockSpec((N_EXPERTS, 1), lambda i: (0, 0))],
        out_specs=[pl.BlockSpec((tm, PAY_W), lambda i: (i, 0)),
                   pl.BlockSpec((1, tm), lambda i: (0, i)),
                   pl.BlockSpec((8, LANES), lambda i: (0, 0))],
        out_shape=[jax.ShapeDtypeStruct((n, PAY_W), F32), jax.ShapeDtypeStruct((1, n), jnp.int32),
                   jax.ShapeDtypeStruct((8, LANES), jnp.int32)],
        scratch_shapes=[pltpu.VMEM((8, LANES), F32)],
        compiler_params=_cparams("arbitrary"), name="route")(hf, wrt, rbc)


def _scatter_kernel(dest_ref, cnt_ref, pay_ref, out_ref, buf, zbuf, sem, zsem, *, region, nsteps):
    tm = pay_ref.shape[0]
    i = pl.program_id(0)
    slot = i % 2

    def wait_rows(s):
        pltpu.make_async_copy(buf.at[s], out_ref.at[pl.ds(0, tm)], sem.at[s]).wait()

    @pl.when(i == 0)
    def _():
        zbuf[...] = jnp.zeros_like(zbuf)
        copies = []
        for g in range(N_GROUPS):
            start = pl.multiple_of(g * region + (cnt_ref[g] // MOE_TILE) * MOE_TILE, MOE_TILE)
            copies.append(pltpu.make_async_copy(zbuf, out_ref.at[pl.ds(start, MOE_TILE)], zsem))
        for cp in copies:
            cp.start()
        for cp in copies:
            cp.wait()

    @pl.when(i >= 2)
    def _():
        wait_rows(slot)

    buf[slot] = pay_ref[...]

    base = i * tm
    for r in range(tm):
        d = dest_ref[base + r]
        pltpu.make_async_copy(buf.at[slot, pl.ds(r, 1)], out_ref.at[pl.ds(d, 1)],
                              sem.at[slot]).start(priority=r % 2)

    @pl.when(i == nsteps - 1)
    def _():
        wait_rows(slot)
        if nsteps >= 2:
            wait_rows(1 - slot)


def _scatter(dest, cnt, pay, region, tm=256):
    n = pay.shape[0]
    nsteps = n // tm
    return pl.pallas_call(
        functools.partial(_scatter_kernel, region=region, nsteps=nsteps),
        grid_spec=pltpu.PrefetchScalarGridSpec(
            num_scalar_prefetch=2, grid=(nsteps,),
            in_specs=[pl.BlockSpec((tm, PAY_W), lambda i, d, c: (i, 0))],
            out_specs=pl.BlockSpec(memory_space=pl.ANY),
            scratch_shapes=[pltpu.VMEM((2, tm, PAY_W), F32), pltpu.VMEM((MOE_TILE, PAY_W), F32),
                            pltpu.SemaphoreType.DMA((2,)), pltpu.SemaphoreType.DMA(())]),
        out_shape=jax.ShapeDtypeStruct((N_GROUPS * region, PAY_W), F32),
        compiler_params=_cparams("arbitrary"), name="moe_scatter")(dest, cnt, pay)


def _tile_tables(cnt, region, ntiles):
    nt = (cnt + MOE_TILE - 1) // MOE_TILE
    ends = jnp.cumsum(nt)
    starts = ends - nt
    total = ends[-1]
    i = jnp.minimum(jnp.arange(ntiles, dtype=jnp.int32), total - 1)
    g = jnp.sum((i[:, None] >= ends[None, :]).astype(jnp.int32), axis=1)
    blk = g * (region // MOE_TILE) + i - starts[g]
    return g.astype(jnp.int32), blk.astype(jnp.int32), total.reshape(1).astype(jnp.int32)


def _moe_up_kernel(tg_ref, tb_ref, nt_ref, x_ref, wg_ref, wu_ref, o_ref, wgs, wus):
    e = pl.program_id(0)
    i = pl.program_id(1)
    grp = tg_ref[i]
    changed = (i == 0) | (grp != tg_ref[jnp.maximum(i - 1, 0)])

    @pl.when(changed)
    def _():
        wgs[...] = wg_ref[...].astype(BF16)
        wus[...] = wu_ref[...].astype(BF16)

    @pl.when(i < nt_ref[0])
    def _():
        x = x_ref[:, :D_MODEL].astype(BF16)
        lane = lax.broadcasted_iota(jnp.int32, (1, LANES), 1)
        ge = jnp.sum(jnp.where(lane == grp * PER_GROUP + e, x_ref[:, D_MODEL:], 0.0), axis=-1, keepdims=True)
        a = _dot(x, wgs[...])
        u = _dot(x, wus[...])
        o_ref[...] = (a * _sigmoid(a) * u * ge).astype(BF16)


def _moe_up(tg, tb, nt, xs, wg, wu, l):
    rows = xs.shape[0]
    ntiles = tg.shape[0]
    d, de = wg.shape[2], wg.shape[3]
    wspec = pl.BlockSpec((None, None, d, de), lambda e, i, tg, tb, nt: (l, tg[i] * PER_GROUP + e, 0, 0))
    return pl.pallas_call(
        _moe_up_kernel,
        grid_spec=pltpu.PrefetchScalarGridSpec(
            num_scalar_prefetch=3, grid=(PER_GROUP, ntiles),
            in_specs=[pl.BlockSpec((MOE_TILE, PAY_W), lambda e, i, tg, tb, nt: (tb[i], 0)), wspec, wspec],
            out_specs=pl.BlockSpec((MOE_TILE, de), lambda e, i, tg, tb, nt: (tb[i], e)),
            scratch_shapes=[pltpu.VMEM((d, de), BF16), pltpu.VMEM((d, de), BF16)]),
        out_shape=jax.ShapeDtypeStruct((rows, PER_GROUP * de), BF16),
        compiler_params=_cparams("arbitrary", "arbitrary"), name="moe_up")(tg, tb, nt, xs, wg, wu)


def _moe_down_kernel(tg_ref, tb_ref, nt_ref, x_ref, w_ref, o_ref, ws):
    i = pl.program_id(1)
    changed = (i == 0) | (tg_ref[i] != tg_ref[jnp.maximum(i - 1, 0)])

    @pl.when(changed)
    def _():
        ws[...] = w_ref[...].astype(BF16)

    @pl.when(i < nt_ref[0])
    def _():
        o_ref[...] = _dot(x_ref[...], ws[...])


def _moe_down(tg, tb, nt, hid, wd, l, tn=1024):
    rows, k = hid.shape
    ntiles = tg.shape[0]
    d = wd.shape[3]
    return pl.pallas_call(
        _moe_down_kernel,
        grid_spec=pltpu.PrefetchScalarGridSpec(
            num_scalar_prefetch=3, grid=(d // tn, ntiles),
            in_specs=[pl.BlockSpec((MOE_TILE, k), lambda c, i, tg, tb, nt: (tb[i], 0)),
                      pl.BlockSpec((None, None, k, tn), lambda c, i, tg, tb, nt: (l, tg[i], 0, c))],
            out_specs=pl.BlockSpec((MOE_TILE, tn), lambda c, i, tg, tb, nt: (tb[i], c)),
            scratch_shapes=[pltpu.VMEM((k, tn), BF16)]),
        out_shape=jax.ShapeDtypeStruct((rows, d), F32),
        compiler_params=_cparams("arbitrary", "arbitrary"), name="moe_down")(tg, tb, nt, hid, wd)


def _gather_ln_kernel(dest_ref, y_ref, h_ref, g_ref, b_ref, of_ref, ob_ref, buf, sem, *, nsteps):
    tm = h_ref.shape[0]
    i = pl.program_id(0)

    def issue(step, slot):
        base = step * tm
        for r in range(tm):
            d = dest_ref[base + r]
            pltpu.make_async_copy(y_ref.at[pl.ds(d, 1)], buf.at[slot, pl.ds(r, 1)],
                                  sem.at[slot]).start(priority=r % 2)

    @pl.when(i == 0)
    def _():
        issue(0, 0)

    @pl.when(i + 1 < nsteps)
    def _():
        issue(i + 1, (i + 1) % 2)

    slot = i % 2
    pltpu.make_async_copy(y_ref.at[pl.ds(0, tm)], buf.at[slot], sem.at[slot]).wait()
    o = _layer_norm(ALPHA * h_ref[...] + buf[slot], g_ref[...], b_ref[...])
    of_ref[...] = o
    ob_ref[...] = o.astype(BF16)


def _gather_ln(dest, ys, hf, g, b, tm=256):
    n, d = hf.shape
    nsteps = n // tm
    row = pl.BlockSpec((tm, d), lambda i, dref: (i, 0))
    par = pl.BlockSpec((1, d), lambda i, dref: (0, 0))
    return pl.pallas_call(
        functools.partial(_gather_ln_kernel, nsteps=nsteps),
        grid_spec=pltpu.PrefetchScalarGridSpec(
            num_scalar_prefetch=1, grid=(nsteps,),
            in_specs=[pl.BlockSpec(memory_space=pl.ANY), row, par, par],
            out_specs=[row, row],
            scratch_shapes=[pltpu.VMEM((2, tm, d), F32), pltpu.SemaphoreType.DMA((2,))]),
        out_shape=[jax.ShapeDtypeStruct((n, d), F32), jax.ShapeDtypeStruct((n, d), BF16)],
        compiler_params=_cparams("arbitrary"), name="moe_gather_ln3")(
            dest, ys, hf, g.reshape(1, d), b.reshape(1, d))


def _moe(hf, p, w, l):
    n = hf.shape[0]
    region = n + MOE_TILE
    ntiles = n // MOE_TILE + N_GROUPS
    pay, dest, cnt = _route(hf, w["w_router_t"], w["router_bias_c"], region)
    dest = dest.reshape(n)
    cnt = cnt[:N_GROUPS, 0]
    tg, tb, nt = _tile_tables(cnt, region, ntiles)
    xs = _scatter(dest, cnt, pay, region)
    hid = _moe_up(tg, tb, nt, xs, p["moe_w_gate"], p["moe_w_up"], l)
    ys = _moe_down(tg, tb, nt, hid, w["moe_wd"], l)
    return _gather_ln(dest, ys, hf, p["ln3_g"][l], p["ln3_b"][l])


def _prep_params(p):
    L = p["w_in"].shape[0]
    half = MLA_ROPE // 2
    pad_r = LANES - MLA_ROPE

    wq = p["mla_w_uq"].reshape(L, MLA_Q_RANK, N_MLA, MLA_NOPE + MLA_ROPE)
    rq = wq[..., MLA_NOPE:]
    zq = jnp.zeros((L, MLA_Q_RANK, N_MLA, pad_r), F32)
    wuq = jnp.concatenate([wq[..., :MLA_NOPE], rq, zq, rq[..., half:], rq[..., :half], zq], axis=-1)
    wkv = p["mla_w_ukv"].reshape(L, MLA_KV_RANK, N_MLA, MLA_NOPE + MLA_V)

    bf = lambda a: a.astype(BF16)
    ne = p["moe_w_down"].shape[1]
    return dict(
        wt_in=jnp.swapaxes(p["w_in"], 1, 2),
        wuq=bf(wuq.reshape(L, MLA_Q_RANK, N_MLA * 3 * LANES)),
        wuk=bf(wkv[..., :MLA_NOPE].reshape(L, MLA_KV_RANK, N_MLA * MLA_NOPE)),
        wuv=bf(wkv[..., MLA_NOPE:].reshape(L, MLA_KV_RANK, N_MLA * MLA_V)),
        gla_wg=jnp.pad(p["gla_w_gate"], ((0, 0), (0, LANES - GLA_RANK), (0, 0))),
        w_branch=bf(p["w_branch"]), w_out=bf(p["w_out"]),
        x_w_q=bf(p["x_w_q"]), x_w_kv=bf(p["x_w_kv"]), x_w_o=bf(p["x_w_o"]),
        w_router_t=jnp.pad(p["w_router"].T, ((0, LANES - N_EXPERTS), (0, 0))),
        router_bias_c=p["router_bias"].reshape(N_EXPERTS, 1),
        moe_wd=p["moe_w_down"].reshape(L, N_GROUPS, (ne // N_GROUPS) * D_EXPERT, D_MODEL),
    )


def _mixer(hf, hb, cos_p, sin_p, p, w, l, batch, seq):
    wt = w["wt_in"]
    o_mla = 2 * D_SG
    o_gla = o_mla + MLA_Q_RANK + MLA_KV_RANK + MLA_ROPE
    gla_w = 2 * N_GLA * GLA_DK + 2 * N_GLA * GLA_DV + LANES
    o_ml = o_gla + gla_w - LANES + GLA_RANK
    ml_w = 2 * N_ML * ML_DK + 2 * N_ML * ML_DV + LANES
    o_gate = o_ml + ml_w - LANES + 2 * N_ML
    z_sg = _mm_wt(hb, wt, l, 0, 1, 2 * D_SG, F32, name="mm_sg")
    z_mla = _mm_wt(hb, wt, l, o_mla, 1, MLA_Q_RANK + MLA_KV_RANK + LANES, F32, name="mm_mla")
    z_gla = _mm_wt(hb, wt, l, o_gla, 1, gla_w, F32, name="mm_gla")
    z_ml = _mm_wt(hb, wt, l, o_ml, 1, ml_w, F32, name="mm_ml")
    gates = _mm_wt(hb, wt, l, o_gate, 4 * D_MODEL // 1024, 1024, BF16, act="sigmoid", name="mm_gate")

    y_a = _sg(z_sg, p["sg_vnorm_g"][l], p["sg_vnorm_b"][l], p["sg_w_s"][l], p["sg_b_s"][l])
    q, k, v = _mla_prep(z_mla, cos_p, sin_p, p["mla_qnorm_g"][l], p["mla_kvnorm_g"][l],
                        w["wuq"][l], w["wuk"][l], w["wuv"][l])
    y_b = _mla_attn(q, k, v, batch, seq)
    y_c = _gla(z_gla, w["gla_wg"][l], p["gla_b_gate"][l], p["gla_norm_g"][l], batch, seq)
    if_cols = z_ml[:, -LANES:-LANES + 2 * N_ML]
    ift = if_cols.reshape(batch, seq, 2 * N_ML).transpose(0, 2, 1)
    y_d = _mlstm(z_ml, ift, p["ml_conv_w"][l], p["ml_conv_b"][l], p["ml_gate_b"][l], p["ml_norm_g"][l],
                 batch, seq)
    merged = _merge(y_a, y_b, y_c, y_d, gates, w["w_branch"][l])
    return _mm_res_ln(merged, w["w_out"][l], hf, p["ln1_g"][l], p["ln1_b"][l], name="out_ln1")


def _forward(p):
    x = p["x"]
    batch, seq, d = x.shape
    n = batch * seq
    mem = p["mem"]
    mem_len = mem.shape[1]
    w = _prep_params(p)
    posb = jnp.broadcast_to(p["positions"].reshape(n, 1).astype(F32), (n, LANES))
    cos_p, sin_p = _rope_tables(posb)
    memb = mem.reshape(batch * mem_len, d).astype(BF16)
    hf, hb = _ln(x.reshape(n, d), p["ln_in_g"], p["ln_in_b"])
    for l in range(p["w_in"].shape[0]):
        hf, hb = _mixer(hf, hb, cos_p, sin_p, p, w, l, batch, seq)
        kv = _mm(memb, w["x_w_kv"][l], BF16, tm=512, tn=1024, name="mm_xkv")
        hf, hb = _xattn(hb, hf, w["x_w_q"][l], kv, w["x_w_o"][l], p["ln2_g"][l], p["ln2_b"][l],
                        batch, seq, mem_len)
        hf, hb = _moe(hf, p, w, l)
    return hf.reshape(batch, seq, d)


def kernel(x, mem, positions, ln_in_g, ln_in_b, w_in, sg_vnorm_g, sg_vnorm_b, sg_w_s, sg_b_s, mla_qnorm_g, mla_kvnorm_g, mla_w_uq, mla_w_ukv, gla_w_gate, gla_b_gate, gla_norm_g, ml_conv_w, ml_conv_b, ml_gate_b, ml_norm_g, w_branch, w_out, ln1_g, ln1_b, x_w_q, x_w_kv, x_w_o, ln2_g, ln2_b, w_router, router_bias, moe_w_gate, moe_w_up, moe_w_down, ln3_g, ln3_b):
    return _forward(dict(
        x=x, mem=mem, positions=positions, ln_in_g=ln_in_g, ln_in_b=ln_in_b, w_in=w_in,
        sg_vnorm_g=sg_vnorm_g, sg_vnorm_b=sg_vnorm_b, sg_w_s=sg_w_s, sg_b_s=sg_b_s,
        mla_qnorm_g=mla_qnorm_g, mla_kvnorm_g=mla_kvnorm_g, mla_w_uq=mla_w_uq, mla_w_ukv=mla_w_ukv,
        gla_w_gate=gla_w_gate, gla_b_gate=gla_b_gate, gla_norm_g=gla_norm_g,
        ml_conv_w=ml_conv_w, ml_conv_b=ml_conv_b, ml_gate_b=ml_gate_b, ml_norm_g=ml_norm_g,
        w_branch=w_branch, w_out=w_out, ln1_g=ln1_g, ln1_b=ln1_b,
        x_w_q=x_w_q, x_w_kv=x_w_kv, x_w_o=x_w_o, ln2_g=ln2_g, ln2_b=ln2_b,
        w_router=w_router, router_bias=router_bias,
        moe_w_gate=moe_w_gate, moe_w_up=moe_w_up, moe_w_down=moe_w_down, ln3_g=ln3_g, ln3_b=ln3_b))
```

```python
import functools
import math

import jax
import jax.numpy as jnp
from jax import lax
from jax.experimental import pallas as pl
from jax.experimental.pallas import tpu as pltpu

F32 = jnp.float32
BF16 = jnp.bfloat16

D_MODEL = 2048
DEPTH = 4
EPS = 1e-5
ALPHA = (2.0 * DEPTH) ** 0.25

SG_CHUNK = 128
N_SG = 4
D_SG = 512
N_MLA = 4
MLA_Q_RANK = 384
MLA_KV_RANK = 256
MLA_NOPE = 128
MLA_ROPE = 64
MLA_V = 128
MLA_QK = 256
ROPE_BASE = 10000.0
N_GLA = 4
GLA_DK = 64
GLA_DV = 128
GLA_RANK = 16
GLA_TAU = 16.0
GLA_CHUNK = 64
N_ML = 4
ML_DK = 64
ML_DV = 128
ML_CHUNK = 128
N_X = 4
X_HEAD = 128
N_EXPERTS = 16
D_EXPERT = 512

LANES = 128
VMEM_LIMIT = 48 * 1024 * 1024
VMEM_LIMIT_BIG = 56 * 1024 * 1024

_NT = (((1,), (1,)), ((), ()))
_TN = (((0,), (0,)), ((), ()))


def _cparams(*sem, vmem=VMEM_LIMIT):
    return pltpu.CompilerParams(dimension_semantics=sem, vmem_limit_bytes=vmem)


def _dot(a, b):
    return jnp.dot(a, b, preferred_element_type=F32)


def _dotg(a, b, dims):
    return lax.dot_general(a, b, dims, preferred_element_type=F32)


def _split3(a):
    a1 = a.astype(BF16)
    r1 = a - a1.astype(F32)
    a2 = r1.astype(BF16)
    a3 = (r1 - a2.astype(F32)).astype(BF16)
    return a1, a2, a3


def _dot_exact_rhs(a, ones_bf16):
    a1, a2, a3 = _split3(a)
    return _dot(a1, ones_bf16) + _dot(a2, ones_bf16) + _dot(a3, ones_bf16)


def _dot_exact_lhs(ones_bf16, a):
    a1, a2, a3 = _split3(a)
    return _dot(ones_bf16, a1) + _dot(ones_bf16, a2) + _dot(ones_bf16, a3)


def _dot_hi(a, b):
    a1 = a.astype(BF16)
    a2 = (a - a1.astype(F32)).astype(BF16)
    b1 = b.astype(BF16)
    b2 = (b - b1.astype(F32)).astype(BF16)
    return _dot(a1, b1) + _dot(a2, b1) + _dot(a1, b2)


def _sigmoid(x):
    return 1.0 / (1.0 + jnp.exp(-x))


def _log_sigmoid(x):
    return jnp.minimum(x, 0.0) - jnp.log(1.0 + jnp.exp(-jnp.abs(x)))


def _layer_norm(t, g, b):
    mu = jnp.mean(t, axis=-1, keepdims=True)
    c = t - mu
    var = jnp.mean(c * c, axis=-1, keepdims=True)
    return c * lax.rsqrt(var + EPS) * g + b


def _rms_norm(t, g):
    return t * lax.rsqrt(jnp.mean(t * t, axis=-1, keepdims=True) + EPS) * g


def _ln_kernel(x_ref, g_ref, b_ref, of_ref, ob_ref):
    y = _layer_norm(x_ref[...], g_ref[...], b_ref[...])
    of_ref[...] = y
    ob_ref[...] = y.astype(BF16)


def _ln(x, g, b, tm=256):
    n, d = x.shape
    row = pl.BlockSpec((tm, d), lambda i: (i, 0))
    par = pl.BlockSpec((1, d), lambda i: (0, 0))
    return pl.pallas_call(
        _ln_kernel, grid=(n // tm,), in_specs=[row, par, par], out_specs=[row, row],
        out_shape=[jax.ShapeDtypeStruct((n, d), F32), jax.ShapeDtypeStruct((n, d), BF16)],
        compiler_params=_cparams("parallel"), name="ln_in")(x, g.reshape(1, d), b.reshape(1, d))


def _mm_kernel(x_ref, w_ref, o_ref, *, act):
    acc = _dot(x_ref[...], w_ref[...])
    if act == "sigmoid":
        acc = _sigmoid(acc)
    o_ref[...] = acc.astype(o_ref.dtype)


def _mm(x, w, out_dtype, act=None, tm=512, tn=None, name="mm"):
    n, k = x.shape
    m = w.shape[1]
    tn = m if tn is None else tn
    return pl.pallas_call(
        functools.partial(_mm_kernel, act=act), grid=(n // tm, m // tn),
        in_specs=[pl.BlockSpec((tm, k), lambda i, j: (i, 0)),
                  pl.BlockSpec((k, tn), lambda i, j: (0, j))],
        out_specs=pl.BlockSpec((tm, tn), lambda i, j: (i, j)),
        out_shape=jax.ShapeDtypeStruct((n, m), out_dtype),
        compiler_params=_cparams("parallel", "parallel"), name=name)(x, w)


def _mm_wt_kernel(x_ref, w_ref, o_ref, ws, *, act):
    @pl.when(pl.program_id(1) == 0)
    def _():
        ws[...] = w_ref[0].astype(BF16)

    acc = _dotg(x_ref[...], ws[...], _NT)
    if act == "sigmoid":
        acc = _sigmoid(acc)
    o_ref[...] = acc.astype(o_ref.dtype)


def _mm_wt(x, wt_all, l, row0, nblk, tn, out_dtype, act=None, tm=1024, name="mm_wt"):
    n, k = x.shape
    assert row0 % 8 == 0
    wmode = dict(pipeline_mode=pl.Buffered(1)) if nblk == 1 else {}
    return pl.pallas_call(
        functools.partial(_mm_wt_kernel, act=act), grid=(nblk, n // tm),
        in_specs=[pl.BlockSpec((tm, k), lambda j, i: (i, 0)),
                  pl.BlockSpec((pl.Element(1), pl.Element(tn), pl.Element(k)),
                               lambda j, i: (l, pl.multiple_of(row0 + j * tn, 8), 0), **wmode)],
        out_specs=pl.BlockSpec((tm, tn), lambda j, i: (i, j)),
        out_shape=jax.ShapeDtypeStruct((n, nblk * tn), out_dtype),
        scratch_shapes=[pltpu.VMEM((tn, k), BF16)],
        compiler_params=_cparams("arbitrary", "arbitrary"), name=name)(x, wt_all)


def _sg_kernel(z_ref, vg_ref, vb_ref, ws_ref, bst_ref, o_ref, *, nchunk):
    z = z_ref[...]
    z = 0.5 * z * (1.0 + jnp.tanh(math.sqrt(2.0 / math.pi) * (z + 0.044715 * (z * z * z))))
    u = z[:, :D_SG]
    vn = _layer_norm(z[:, D_SG:], vg_ref[...], vb_ref[...]).astype(BF16)
    r = lax.broadcasted_iota(jnp.int32, (SG_CHUNK, SG_CHUNK), 0)
    c = lax.broadcasted_iota(jnp.int32, (SG_CHUNK, SG_CHUNK), 1)
    causal = c <= r
    gw = SG_CHUNK
    for g in range(N_SG):
        w = jnp.where(causal, ws_ref[g], 0.0).astype(BF16)
        bias = bst_ref[:, g:g + 1]
        for ci in range(nchunk):
            rs = slice(ci * SG_CHUNK, (ci + 1) * SG_CHUNK)
            cs = slice(g * gw, (g + 1) * gw)
            mixed = _dot(w, vn[rs, cs]) + bias
            o_ref[rs, cs] = (u[rs, cs] * mixed).astype(BF16)


def _sg(z, vg, vb, ws, bs, tm=512):
    n = z.shape[0]
    return pl.pallas_call(
        functools.partial(_sg_kernel, nchunk=tm // SG_CHUNK), grid=(n // tm,),
        in_specs=[pl.BlockSpec((tm, 2 * D_SG), lambda i: (i, 0)),
                  pl.BlockSpec((1, D_SG), lambda i: (0, 0)),
                  pl.BlockSpec((1, D_SG), lambda i: (0, 0)),
                  pl.BlockSpec((N_SG, SG_CHUNK, SG_CHUNK), lambda i: (0, 0, 0)),
                  pl.BlockSpec((SG_CHUNK, N_SG), lambda i: (0, 0))],
        out_specs=pl.BlockSpec((tm, D_SG), lambda i: (i, 0)),
        out_shape=jax.ShapeDtypeStruct((n, D_SG), BF16),
        compiler_params=_cparams("parallel"), name="sg")(
            z, vg.reshape(1, D_SG), vb.reshape(1, D_SG), ws, bs.T)


def _rope_kernel(pos_ref, cos_ref, sin_ref):
    half = MLA_ROPE // 2
    lane = lax.broadcasted_iota(jnp.int32, (1, LANES), 1)
    idx = jnp.bitwise_and(lane, half - 1).astype(F32)
    freq = jnp.exp(idx * (-math.log(ROPE_BASE) / half))
    ang = pos_ref[...] * freq
    c = jnp.cos(ang)
    s = jnp.sin(ang)
    cos_ref[...] = jnp.where(lane < MLA_ROPE, c, 0.0)
    sin_ref[...] = jnp.where(lane < half, -s, jnp.where(lane < MLA_ROPE, s, 0.0))


def _rope_tables(posb, tm=512):
    n = posb.shape[0]
    row = pl.BlockSpec((tm, LANES), lambda i: (i, 0))
    return pl.pallas_call(
        _rope_kernel, grid=(n // tm,), in_specs=[row], out_specs=[row, row],
        out_shape=[jax.ShapeDtypeStruct((n, LANES), F32)] * 2,
        compiler_params=_cparams("parallel"), name="rope_tables")(posb)


def _mla_prep_kernel(z_ref, cos_ref, sin_ref, qg_ref, kvg_ref, wuq_ref, wuk_ref, wuv_ref,
                     q_ref, k_ref, v_ref):
    z = z_ref[...]
    cq = _rms_norm(z[:, :MLA_Q_RANK], qg_ref[...]).astype(BF16)
    o1 = MLA_Q_RANK + MLA_KV_RANK
    ckv = _rms_norm(z[:, MLA_Q_RANK:o1], kvg_ref[...]).astype(BF16)
    cos_p = cos_ref[...]
    sin_p = sin_ref[...]
    kr = z[:, o1:o1 + LANES]
    half = MLA_ROPE // 2
    lane = lax.broadcasted_iota(jnp.int32, (1, LANES), 1)
    kr_swapped = jnp.where(lane < half, pltpu.roll(kr, LANES - half, 1), pltpu.roll(kr, half, 1))
    k_tail = (kr * cos_p + kr_swapped * sin_p).astype(BF16)
    qa = _dot(cq, wuq_ref[...])
    kn = _dot(ckv, wuk_ref[...])
    v_ref[...] = _dot(ckv, wuv_ref[...]).astype(BF16)
    scale = (MLA_NOPE + MLA_ROPE) ** -0.5
    for h in range(N_MLA):
        b0 = h * 3 * LANES
        q_tail = qa[:, b0 + LANES:b0 + 2 * LANES] * cos_p + qa[:, b0 + 2 * LANES:b0 + 3 * LANES] * sin_p
        q_ref[:, h * MLA_QK:h * MLA_QK + LANES] = (qa[:, b0:b0 + LANES] * scale).astype(BF16)
        q_ref[:, h * MLA_QK + LANES:(h + 1) * MLA_QK] = (q_tail * scale).astype(BF16)
        k_ref[:, h * MLA_QK:h * MLA_QK + LANES] = kn[:, h * LANES:(h + 1) * LANES].astype(BF16)
        k_ref[:, h * MLA_QK + LANES:(h + 1) * MLA_QK] = k_tail


def _mla_prep(z, cos_p, sin_p, qg, kvg, wuq, wuk, wuv, tm=512):
    n, zw = z.shape
    row = lambda w: pl.BlockSpec((tm, w), lambda i: (i, 0))
    full = lambda a: pl.BlockSpec(a.shape, lambda i: (0,) * a.ndim)
    qg = qg.reshape(1, -1)
    kvg = kvg.reshape(1, -1)
    return pl.pallas_call(
        _mla_prep_kernel, grid=(n // tm,),
        in_specs=[row(zw), row(LANES), row(LANES), full(qg), full(kvg), full(wuq), full(wuk), full(wuv)],
        out_specs=[row(N_MLA * MLA_QK), row(N_MLA * MLA_QK), row(N_MLA * MLA_V)],
        out_shape=[jax.ShapeDtypeStruct((n, N_MLA * MLA_QK), BF16),
                   jax.ShapeDtypeStruct((n, N_MLA * MLA_QK), BF16),
                   jax.ShapeDtypeStruct((n, N_MLA * MLA_V), BF16)],
        compiler_params=_cparams("parallel"), name="mla_prep")(z, cos_p, sin_p, qg, kvg, wuq, wuk, wuv)


def _mla_attn_kernel(q_ref, k_ref, v_ref, o_ref, *, t, nblk):
    krow = lax.broadcasted_iota(jnp.int32, (t, t), 0)
    qcol = lax.broadcasted_iota(jnp.int32, (t, t), 1)
    visible = krow <= qcol
    for i in range(nblk):
        q = q_ref[i * t:(i + 1) * t, :]
        m = jnp.full((1, t), -jnp.inf, F32)
        l = jnp.zeros((1, t), F32)
        acc = jnp.zeros((MLA_V, t), F32)
        for j in range(i + 1):
            ks = slice(j * t, (j + 1) * t)
            s = _dotg(k_ref[ks, :], q, _NT)
            if j == i:
                s = jnp.where(visible, s, -jnp.inf)
            m_new = jnp.maximum(m, jnp.max(s, axis=0, keepdims=True))
            p = jnp.exp(s - m_new)
            a = jnp.exp(m - m_new)
            l = a * l + jnp.sum(p, axis=0, keepdims=True)
            acc = a * acc + _dotg(v_ref[ks, :], p.astype(BF16), _TN)
            m = m_new
        o_ref[i * t:(i + 1) * t, :] = (acc / l).T.astype(BF16)


def _mla_attn(q, k, v, batch, seq, t=512):
    n = q.shape[0]
    return pl.pallas_call(
        functools.partial(_mla_attn_kernel, t=t, nblk=seq // t), grid=(batch, N_MLA),
        in_specs=[pl.BlockSpec((seq, MLA_QK), lambda b, h: (b, h)),
                  pl.BlockSpec((seq, MLA_QK), lambda b, h: (b, h)),
                  pl.BlockSpec((seq, MLA_V), lambda b, h: (b, h))],
        out_specs=pl.BlockSpec((seq, MLA_V), lambda b, h: (b, h)),
        out_shape=jax.ShapeDtypeStruct((n, N_MLA * MLA_V), BF16),
        compiler_params=_cparams("parallel", "parallel"), name="mla_attn")(q, k, v)


def _gla_tile(z, o_ref, row0, st, wg_ref, bg_ref, ng_ref, nchunk):
    L = GLA_CHUNK
    qkw = N_GLA * GLA_DK
    vw = N_GLA * GLA_DV
    c_k, c_v, c_o, c_lr = qkw, 2 * qkw, 2 * qkw + vw, 2 * qkw + 2 * vw
    logits = _dot_hi(z[:, c_lr:c_lr + LANES], wg_ref[...]) + bg_ref[...]
    log_a = _log_sigmoid(logits) * (1.0 / GLA_TAU)
    lane = lax.broadcasted_iota(jnp.int32, (1, qkw), 1)
    masks = [((lane >= h * GLA_DK) & (lane < (h + 1) * GLA_DK)).astype(F32) for h in range(N_GLA)]
    r = lax.broadcasted_iota(jnp.int32, (L, L), 0)
    c = lax.broadcasted_iota(jnp.int32, (L, L), 1)
    causal = c <= r
    tril = jnp.where(causal, 1.0, 0.0).astype(BF16)
    ng = ng_ref[...]
    for ci in range(nchunk):
        rs = slice(ci * L, (ci + 1) * L)
        ro = slice(row0 + ci * L, row0 + (ci + 1) * L)
        b = _dot_exact_lhs(tril, log_a[rs])
        b_last = b[L - 1:L, :]
        q = z[rs, 0:qkw] * (GLA_DK ** -0.5)
        k = z[rs, c_k:c_k + qkw]
        qt = q * jnp.exp(b)
        kt = (k * jnp.exp(-b)).astype(BF16)
        kd = (k * jnp.exp(b_last - b)).astype(BF16)
        qstack = jnp.concatenate([qt * masks[h] for h in range(N_GLA)], axis=0).astype(BF16)
        att = _dotg(qstack, kt, _NT)
        inter = _dotg(qstack, st.astype(BF16), _NT)
        vb = z[rs, c_v:c_v + vw].astype(BF16)
        for h in range(N_GLA):
            hs = slice(h * L, (h + 1) * L)
            vs = slice(h * GLA_DV, (h + 1) * GLA_DV)
            a_h = jnp.where(causal, att[hs], 0.0).astype(BF16)
            o_h = _rms_norm(_dot(a_h, vb[:, vs]) + inter[hs], ng)
            g = z[rs, c_o + h * GLA_DV:c_o + (h + 1) * GLA_DV]
            o_ref[ro, vs] = (o_h * (g * _sigmoid(g))).astype(BF16)
        upd = _dotg(vb, kd, _TN)
        new = st * jnp.exp(b_last)
        for h in range(N_GLA):
            new = new + upd[h * GLA_DV:(h + 1) * GLA_DV] * masks[h]
        st = new
    return st


def _gla_kernel(x_ref, xn_ref, w_ref, wg_ref, bg_ref, ng_ref, o_ref, ws, za, zb, st_ref, *, tm):
    first = pl.program_id(1) == 0

    @pl.when((pl.program_id(0) == 0) & first)
    def _():
        ws[...] = w_ref[0].astype(BF16)

    @pl.when(first)
    def _():
        st_ref[...] = jnp.zeros_like(st_ref)
        za[...] = _dotg(x_ref[0:tm, :], ws[...], _NT)

    nchunk = tm // GLA_CHUNK
    zb[...] = _dotg(x_ref[tm:2 * tm, :], ws[...], _NT)
    st = _gla_tile(za, o_ref, 0, st_ref[...], wg_ref, bg_ref, ng_ref, nchunk)
    za[...] = _dotg(xn_ref[...], ws[...], _NT)
    st_ref[...] = _gla_tile(zb, o_ref, tm, st, wg_ref, bg_ref, ng_ref, nchunk)


def _gla(hb, wt, l, row0, wg, bg, ng, batch, seq, tm=256):
    n, d = hb.shape
    ns2 = seq // (2 * tm)
    qkw = N_GLA * GLA_DK
    vw = N_GLA * GLA_DV
    zw = 2 * qkw + 2 * vw + LANES
    full = lambda a: pl.BlockSpec(a.shape, lambda b, s: (0,) * a.ndim)
    bg = bg.reshape(1, qkw)
    ng = ng.reshape(1, GLA_DV)
    last = 2 * ns2 - 1
    return pl.pallas_call(
        functools.partial(_gla_kernel, tm=tm), grid=(batch, ns2),
        in_specs=[pl.BlockSpec((2 * tm, d), lambda b, s: (b * ns2 + s, 0)),
                  pl.BlockSpec((tm, d), lambda b, s: (b * 2 * ns2 + jnp.minimum(2 * s + 2, last), 0)),
                  pl.BlockSpec((pl.Element(1), pl.Element(zw), pl.Element(d)), lambda b, s: (l, row0, 0),
                               pipeline_mode=pl.Buffered(1)),
                  full(wg), full(bg), full(ng)],
        out_specs=pl.BlockSpec((2 * tm, vw), lambda b, s: (b * ns2 + s, 0)),
        out_shape=jax.ShapeDtypeStruct((n, vw), BF16),
        scratch_shapes=[pltpu.VMEM((zw, d), BF16), pltpu.VMEM((tm, zw), F32), pltpu.VMEM((tm, zw), F32),
                        pltpu.VMEM((GLA_DV, qkw), F32)],
        compiler_params=_cparams("arbitrary", "arbitrary"), name="gla")(hb, hb, wt, wg, bg, ng)


def _mlstm_tile(z, o_ref, row0, carry, cw_ref, cb_ref, gb_ref, ng_ref, nchunk):
    L = ML_CHUNK
    qkw = N_ML * ML_DK
    vw = N_ML * ML_DV
    tm = nchunk * L
    c_v, c_o, c_if = 2 * qkw, 2 * qkw + vw, 2 * qkw + 2 * vw
    ct, nrow, mwide, mk, tail = carry

    x = z[:, 0:2 * qkw]
    row8 = lax.broadcasted_iota(jnp.int32, (8, 2 * qkw), 0)
    acc = x * cw_ref[3:4, :] + cb_ref[...]
    for j in range(1, 4):
        rx = pltpu.roll(x, j, 0)
        fix = jnp.where(row8 < j, pltpu.roll(tail, j, 0), rx[0:8])
        acc = acc + jnp.concatenate([fix, rx[8:]], axis=0) * cw_ref[3 - j:4 - j, :]
    tail = x[tm - 8:tm]
    y = acc * _sigmoid(acc)
    q = y[:, :qkw] * (ML_DK ** -0.5)
    k = y[:, qkw:]

    gates = z[:, c_if:c_if + LANES] + gb_ref[...]
    fc = _log_sigmoid(gates)
    gt = gates.T[0:2 * N_ML]
    fct = _log_sigmoid(gt)

    lane = lax.broadcasted_iota(jnp.int32, (1, qkw), 1)
    masks = [((lane >= h * ML_DK) & (lane < (h + 1) * ML_DK)).astype(F32) for h in range(N_ML)]
    r = lax.broadcasted_iota(jnp.int32, (L, L), 0)
    c = lax.broadcasted_iota(jnp.int32, (L, L), 1)
    causal = c <= r
    tril = jnp.where(causal, 1.0, 0.0).astype(BF16)
    triu = jnp.where(r <= c, 1.0, 0.0).astype(BF16)

    def selector(width, block, first):
        rr = lax.broadcasted_iota(jnp.int32, (LANES, width), 0)
        cc = lax.broadcasted_iota(jnp.int32, (LANES, width), 1)
        return jnp.where(rr == (cc >> int(math.log2(block))) + first, 1.0, 0.0).astype(BF16)

    wide = N_ML * L
    fcb = _dot_exact_rhs(fc, selector(wide, L, N_ML))
    icb = _dot_exact_rhs(gates, selector(wide, L, 0))
    fck = _dot_exact_rhs(fc, selector(qkw, ML_DK, N_ML))
    ick = _dot_exact_rhs(gates, selector(qkw, ML_DK, 0))

    ng = ng_ref[...]

    for ci in range(nchunk):
        rs = slice(ci * L, (ci + 1) * L)
        ro = slice(row0 + ci * L, row0 + (ci + 1) * L)
        bb = _dot_exact_lhs(tril, fcb[rs])
        bk = _dot_exact_lhs(tril, fck[rs])
        brow = _dot_exact_rhs(fct[:, rs], triu)
        rowterm = gt[:, rs] - pltpu.roll(brow, N_ML, 0)
        qc = q[rs]
        kc = k[rs]
        log_d = jnp.concatenate(
            [jnp.where(causal, bb[:, h * L:(h + 1) * L] + rowterm[h:h + 1, :], -jnp.inf) for h in range(N_ML)],
            axis=0)
        log_inter = jnp.concatenate(
            [bb[:, h * L:(h + 1) * L] + mwide[:, h * L:(h + 1) * L] for h in range(N_ML)], axis=0)
        m_t = jnp.maximum(log_inter, jnp.max(log_d, axis=-1, keepdims=True))
        w_inter = jnp.exp(log_inter - m_t)
        qst = jnp.concatenate([qc * masks[h] for h in range(N_ML)], axis=0)
        qsb = qst.astype(BF16)
        s_all = _dotg(qsb, kc.astype(BF16), _NT) * jnp.exp(log_d - m_t)
        sb = s_all.astype(BF16)
        vb = z[rs, c_v:c_v + vw].astype(BF16)
        num = jnp.concatenate(
            [_dot(sb[h * L:(h + 1) * L], vb[:, h * ML_DV:(h + 1) * ML_DV]) for h in range(N_ML)], axis=0)
        num = num + w_inter * _dotg(qsb, ct.astype(BF16), _NT)
        den = jnp.sum(s_all, axis=-1, keepdims=True) + w_inter * jnp.sum(qst * nrow, axis=-1, keepdims=True)
        hh = _rms_norm(num / jnp.maximum(jnp.abs(den), jnp.exp(-m_t)), ng)
        for h in range(N_ML):
            vs = slice(h * ML_DV, (h + 1) * ML_DV)
            gate_o = _sigmoid(z[rs, c_o + h * ML_DV:c_o + (h + 1) * ML_DV])
            o_ref[ro, vs] = (hh[h * L:(h + 1) * L] * gate_o).astype(BF16)
        bl_w = bb[L - 1:L, :]
        lw_w = bl_w - bb + icb[rs]
        mwide_new = jnp.maximum(bl_w + mwide, jnp.max(lw_w, axis=0, keepdims=True))
        bl_k = bk[L - 1:L, :]
        lw_k = bl_k - bk + ick[rs]
        mk_new = jnp.maximum(bl_k + mk, jnp.max(lw_k, axis=0, keepdims=True))
        decay = jnp.exp(bl_k + mk - mk_new)
        kw = kc * jnp.exp(lw_k - mk_new)
        upd = _dotg(vb, kw.astype(BF16), _TN)
        ct = ct * decay
        for h in range(N_ML):
            ct = ct + upd[h * ML_DV:(h + 1) * ML_DV] * masks[h]
        nrow = nrow * decay + jnp.sum(kw, axis=0, keepdims=True)
        mwide = mwide_new
        mk = mk_new

    return ct, nrow, mwide, mk, tail


def _mlstm_kernel(x_ref, xn_ref, w_ref, cw_ref, cb_ref, gb_ref, ng_ref, o_ref,
                  ws, za, zb, ct_ref, n_ref, mw_ref, mk_ref, tail_ref, *, tm):
    first = pl.program_id(1) == 0

    @pl.when((pl.program_id(0) == 0) & first)
    def _():
        ws[...] = w_ref[0].astype(BF16)

    @pl.when(first)
    def _():
        for ref in (ct_ref, n_ref, mw_ref, mk_ref, tail_ref):
            ref[...] = jnp.zeros_like(ref)
        za[...] = _dotg(x_ref[0:tm, :], ws[...], _NT)

    nchunk = tm // ML_CHUNK
    consts = (cw_ref, cb_ref, gb_ref, ng_ref, nchunk)
    zb[...] = _dotg(x_ref[tm:2 * tm, :], ws[...], _NT)
    carry = (ct_ref[...], n_ref[...], mw_ref[...], mk_ref[...], tail_ref[...])
    carry = _mlstm_tile(za, o_ref, 0, carry, *consts)
    za[...] = _dotg(xn_ref[...], ws[...], _NT)
    carry = _mlstm_tile(zb, o_ref, tm, carry, *consts)
    for ref, val in zip((ct_ref, n_ref, mw_ref, mk_ref, tail_ref), carry):
        ref[...] = val


def _mlstm(hb, wt, l, row0, cw, cb, gb, ng, batch, seq, tm=512):
    n, d = hb.shape
    ns2 = seq // (2 * tm)
    qkw = N_ML * ML_DK
    vw = N_ML * ML_DV
    zw = 2 * qkw + 2 * vw + LANES
    full = lambda a: pl.BlockSpec(a.shape, lambda b, s: (0,) * a.ndim)
    cb = cb.reshape(1, 2 * qkw)
    gbp = jnp.pad(gb, (0, LANES - 2 * N_ML)).reshape(1, LANES)
    ng = ng.reshape(1, ML_DV)
    last = 2 * ns2 - 1
    return pl.pallas_call(
        functools.partial(_mlstm_kernel, tm=tm), grid=(batch, ns2),
        in_specs=[pl.BlockSpec((2 * tm, d), lambda b, s: (b * ns2 + s, 0)),
                  pl.BlockSpec((tm, d), lambda b, s: (b * 2 * ns2 + jnp.minimum(2 * s + 2, last), 0)),
                  pl.BlockSpec((pl.Element(1), pl.Element(zw), pl.Element(d)), lambda b, s: (l, row0, 0),
                               pipeline_mode=pl.Buffered(1)),
                  full(cw), full(cb), full(gbp), full(ng)],
        out_specs=pl.BlockSpec((2 * tm, vw), lambda b, s: (b * ns2 + s, 0)),
        out_shape=jax.ShapeDtypeStruct((n, vw), BF16),
        scratch_shapes=[pltpu.VMEM((zw, d), BF16), pltpu.VMEM((tm, zw), F32), pltpu.VMEM((tm, zw), F32),
                        pltpu.VMEM((ML_DV, qkw), F32), pltpu.VMEM((1, qkw), F32),
                        pltpu.VMEM((1, N_ML * ML_CHUNK), F32), pltpu.VMEM((1, qkw), F32),
                        pltpu.VMEM((8, 2 * qkw), F32)],
        compiler_params=_cparams("arbitrary", "arbitrary", vmem=VMEM_LIMIT_BIG), name="mlstm")(
            hb, hb, wt, cw, cb, gbp, ng)


def _merge_out_kernel(ya_ref, yb_ref, yc_ref, yd_ref, g_ref, wb_ref, wo_ref, h_ref, lg_ref, lb_ref,
                      of_ref, ob_ref):
    acc = None
    for i, y_ref in enumerate((ya_ref, yb_ref, yc_ref, yd_ref)):
        p = _dot(y_ref[...], wb_ref[i])
        t = g_ref[:, i * D_MODEL:(i + 1) * D_MODEL].astype(F32) * p
        acc = t if acc is None else acc + t
    y = _dot(acc.astype(BF16), wo_ref[...])
    o = _layer_norm(ALPHA * h_ref[...] + y, lg_ref[...], lb_ref[...])
    of_ref[...] = o
    ob_ref[...] = o.astype(BF16)


def _merge_out(ya, yb, yc, yd, gates, wb, wo, hf, lg, lb, tm=256):
    n, bw = ya.shape
    d = hf.shape[1]
    row = pl.BlockSpec((tm, bw), lambda i: (i, 0))
    hrow = pl.BlockSpec((tm, d), lambda i: (i, 0))
    par = pl.BlockSpec((1, d), lambda i: (0, 0))
    once = dict(pipeline_mode=pl.Buffered(1))
    return pl.pallas_call(
        _merge_out_kernel, grid=(n // tm,),
        in_specs=[row, row, row, row, pl.BlockSpec((tm, 4 * d), lambda i: (i, 0)),
                  pl.BlockSpec(wb.shape, lambda i: (0, 0, 0), **once),
                  pl.BlockSpec(wo.shape, lambda i: (0, 0), **once), hrow, par, par],
        out_specs=[hrow, hrow],
        out_shape=[jax.ShapeDtypeStruct((n, d), F32), jax.ShapeDtypeStruct((n, d), BF16)],
        compiler_params=_cparams("parallel"), name="merge_out_ln1")(
            ya, yb, yc, yd, gates, wb, wo, hf, lg.reshape(1, d), lb.reshape(1, d))


def _xattn_kernel(hb_ref, hf_ref, wq_ref, k_ref, v_ref, wo_ref, g_ref, b_ref, of_ref, ob_ref):
    q = (_dot(hb_ref[...], wq_ref[...]) * (X_HEAD ** -0.5)).astype(BF16)
    outs = []
    for h in range(N_X):
        hs = slice(h * X_HEAD, (h + 1) * X_HEAD)
        s = _dotg(q[:, hs], k_ref[:, hs], _NT)
        p = jnp.exp(s - jnp.max(s, axis=-1, keepdims=True))
        l = jnp.sum(p, axis=-1, keepdims=True)
        outs.append((_dot(p.astype(BF16), v_ref[:, hs]) / l).astype(BF16))
    y = _dot(jnp.concatenate(outs, axis=-1), wo_ref[...])
    o = _layer_norm(ALPHA * hf_ref[...] + y, g_ref[...], b_ref[...])
    of_ref[...] = o
    ob_ref[...] = o.astype(BF16)


def _xattn(hb, hf, wq, kv, wo, g, b, batch, seq, mem_len, tm=512):
    n, d = hf.shape
    ns = seq // tm
    xw = N_X * X_HEAD
    row = pl.BlockSpec((tm, d), lambda bb, s: (bb * ns + s, 0))
    par = pl.BlockSpec((1, d), lambda bb, s: (0, 0))
    return pl.pallas_call(
        _xattn_kernel, grid=(batch, ns),
        in_specs=[row, row, pl.BlockSpec((d, xw), lambda bb, s: (0, 0)),
                  pl.BlockSpec((mem_len, xw), lambda bb, s: (bb, 0)),
                  pl.BlockSpec((mem_len, xw), lambda bb, s: (bb, 1)),
                  pl.BlockSpec((xw, d), lambda bb, s: (0, 0)), par, par],
        out_specs=[row, row],
        out_shape=[jax.ShapeDtypeStruct((n, d), F32), jax.ShapeDtypeStruct((n, d), BF16)],
        compiler_params=_cparams("parallel", "parallel"), name="xattn")(
            hb, hf, wq, kv, kv, wo, g.reshape(1, d), b.reshape(1, d))


N_GROUPS = 4
PER_GROUP = N_EXPERTS // N_GROUPS
MOE_TILE = 512
PAY_W = D_MODEL + LANES


def _route_kernel(h_ref, wrt_ref, rbc_ref, pay_ref, dest_ref, cnt_ref, carry_ref, *, region):
    tm = h_ref.shape[0]

    @pl.when(pl.program_id(0) == 0)
    def _():
        carry_ref[...] = jnp.zeros_like(carry_ref)

    h = h_ref[...]
    w = wrt_ref[...]
    w1 = w.astype(BF16)
    w2 = (w - w1.astype(F32)).astype(BF16)
    h1 = h.astype(BF16)
    h2 = (h - h1.astype(F32)).astype(BF16)
    logits = _dotg(w1, h1, _NT) + _dotg(w2, h1, _NT) + _dotg(w1, h2, _NT)
    aff = _sigmoid(logits[0:N_EXPERTS])
    biased = aff + rbc_ref[...]
    row = lax.broadcasted_iota(jnp.int32, (N_EXPERTS, 1), 0).astype(F32)
    big = float(LANES)
    best = e1 = e2 = None
    for g in range(N_GROUPS):
        x = jnp.where((row >= g * PER_GROUP) & (row < (g + 1) * PER_GROUP), biased, -jnp.inf)
        m1 = jnp.max(x, axis=0, keepdims=True)
        i1 = jnp.min(jnp.where(x == m1, row, big), axis=0, keepdims=True)
        x2 = jnp.where(row == i1, -jnp.inf, x)
        m2 = jnp.max(x2, axis=0, keepdims=True)
        i2 = jnp.min(jnp.where(x2 == m2, row, big), axis=0, keepdims=True)
        score = m1 + m2
        if g == 0:
            best, e1, e2 = score, i1, i2
        else:
            better = score > best
            best = jnp.where(better, score, best)
            e1 = jnp.where(better, i1, e1)
            e2 = jnp.where(better, i2, e2)
    s1 = jnp.sum(jnp.where(row == e1, aff, 0.0), axis=0, keepdims=True)
    s2 = jnp.sum(jnp.where(row == e2, aff, 0.0), axis=0, keepdims=True)
    tot = s1 + s2
    gates_t = jnp.where(row == e1, s1 / tot, 0.0) + jnp.where(row == e2, s2 / tot, 0.0)

    grp = jnp.zeros_like(e1)
    for g in range(1, N_GROUPS):
        grp = grp + jnp.where(e1 >= g * PER_GROUP, 1.0, 0.0)
    row8 = lax.broadcasted_iota(jnp.int32, (8, 1), 0).astype(F32)
    onehot = jnp.where(row8 == grp, 1.0, 0.0)
    r = lax.broadcasted_iota(jnp.int32, (tm, tm), 0)
    c = lax.broadcasted_iota(jnp.int32, (tm, tm), 1)
    earlier = jnp.where(r < c, 1.0, 0.0).astype(BF16)
    rank_in = _dot(onehot.astype(BF16), earlier)
    carry = carry_ref[...]
    rank = jnp.sum(onehot * (rank_in + carry[:, 0:1]), axis=0, keepdims=True)
    dest_ref[...] = (grp * float(region) + rank).astype(jnp.int32)
    carry = carry + jnp.sum(onehot, axis=1, keepdims=True)
    carry_ref[...] = carry
    cnt_ref[...] = carry.astype(jnp.int32)

    gates = jnp.concatenate([gates_t, jnp.zeros((LANES - N_EXPERTS, tm), F32)], axis=0).T
    pay_ref[:, :D_MODEL] = h
    pay_ref[:, D_MODEL:] = gates


def _route(hf, wrt, rbc, region, tm=512):
    n, d = hf.shape
    return pl.pallas_call(
        functools.partial(_route_kernel, region=region), grid=(n // tm,),
        in_specs=[pl.BlockSpec((tm, d), lambda i: (i, 0)),
                  pl.BlockSpec((LANES, d), lambda i: (0, 0)),
                  pl.BlockSpec((N_EXPERTS, 1), lambda i: (0, 0))],
        out_specs=[pl.BlockSpec((tm, PAY_W), lambda i: (i, 0)),
                   pl.BlockSpec((1, tm), lambda i: (0, i)),
                   pl.BlockSpec((8, LANES), lambda i: (0, 0))],
        out_shape=[jax.ShapeDtypeStruct((n, PAY_W), F32), jax.ShapeDtypeStruct((1, n), jnp.int32),
                   jax.ShapeDtypeStruct((8, LANES), jnp.int32)],
        scratch_shapes=[pltpu.VMEM((8, LANES), F32)],
        compiler_params=_cparams("arbitrary"), name="route")(hf, wrt, rbc)


def _scatter_kernel(dest_ref, cnt_ref, pay_ref, out_ref, buf, zbuf, sem, zsem, *, region, nsteps):
    tm = pay_ref.shape[0]
    i = pl.program_id(0)
    slot = i % 2

    def wait_rows(s):
        pltpu.make_async_copy(buf.at[s], out_ref.at[pl.ds(0, tm)], sem.at[s]).wait()

    @pl.when(i == 0)
    def _():
        zbuf[...] = jnp.zeros_like(zbuf)
        copies = []
        for g in range(N_GROUPS):
            start = pl.multiple_of(g * region + (cnt_ref[g] // MOE_TILE) * MOE_TILE, MOE_TILE)
            copies.append(pltpu.make_async_copy(zbuf, out_ref.at[pl.ds(start, MOE_TILE)], zsem))
        for cp in copies:
            cp.start()
        for cp in copies:
            cp.wait()

    @pl.when(i >= 2)
    def _():
        wait_rows(slot)

    buf[slot] = pay_ref[...]

    base = i * tm
    for r in range(tm):
        d = dest_ref[base + r]
        pltpu.make_async_copy(buf.at[slot, pl.ds(r, 1)], out_ref.at[pl.ds(d, 1)],
                              sem.at[slot]).start(priority=r % 2)

    @pl.when(i == nsteps - 1)
    def _():
        wait_rows(slot)
        if nsteps >= 2:
            wait_rows(1 - slot)


def _scatter(dest, cnt, pay, region, tm=256):
    n = pay.shape[0]
    nsteps = n // tm
    return pl.pallas_call(
        functools.partial(_scatter_kernel, region=region, nsteps=nsteps),
        grid_spec=pltpu.PrefetchScalarGridSpec(
            num_scalar_prefetch=2, grid=(nsteps,),
            in_specs=[pl.BlockSpec((tm, PAY_W), lambda i, d, c: (i, 0))],
            out_specs=pl.BlockSpec(memory_space=pl.ANY),
            scratch_shapes=[pltpu.VMEM((2, tm, PAY_W), F32), pltpu.VMEM((MOE_TILE, PAY_W), F32),
                            pltpu.SemaphoreType.DMA((2,)), pltpu.SemaphoreType.DMA(())]),
        out_shape=jax.ShapeDtypeStruct((N_GROUPS * region, PAY_W), F32),
        compiler_params=_cparams("arbitrary"), name="moe_scatter")(dest, cnt, pay)


def _tile_tables(cnt, region, ntiles):
    nt = (cnt + MOE_TILE - 1) // MOE_TILE
    ends = jnp.cumsum(nt)
    starts = ends - nt
    total = ends[-1]
    i = jnp.minimum(jnp.arange(ntiles, dtype=jnp.int32), total - 1)
    g = jnp.sum((i[:, None] >= ends[None, :]).astype(jnp.int32), axis=1)
    blk = g * (region // MOE_TILE) + i - starts[g]
    return g.astype(jnp.int32), blk.astype(jnp.int32), total.reshape(1).astype(jnp.int32)


def _moe_up_kernel(tg_ref, tb_ref, nt_ref, x_ref, wg_ref, wu_ref, o_ref, wgs, wus):
    e = pl.program_id(0)
    i = pl.program_id(1)
    grp = tg_ref[i]
    changed = (i == 0) | (grp != tg_ref[jnp.maximum(i - 1, 0)])

    @pl.when(changed)
    def _():
        wgs[...] = wg_ref[...].astype(BF16)
        wus[...] = wu_ref[...].astype(BF16)

    @pl.when(i < nt_ref[0])
    def _():
        x = x_ref[:, :D_MODEL].astype(BF16)
        lane = lax.broadcasted_iota(jnp.int32, (1, LANES), 1)
        ge = jnp.sum(jnp.where(lane == grp * PER_GROUP + e, x_ref[:, D_MODEL:], 0.0), axis=-1, keepdims=True)
        a = _dot(x, wgs[...])
        u = _dot(x, wus[...])
        o_ref[...] = (a * _sigmoid(a) * u * ge).astype(BF16)


def _moe_up(tg, tb, nt, xs, wg, wu, l):
    rows = xs.shape[0]
    ntiles = tg.shape[0]
    d, de = wg.shape[2], wg.shape[3]
    wspec = pl.BlockSpec((None, None, d, de), lambda e, i, tg, tb, nt: (l, tg[i] * PER_GROUP + e, 0, 0))
    return pl.pallas_call(
        _moe_up_kernel,
        grid_spec=pltpu.PrefetchScalarGridSpec(
            num_scalar_prefetch=3, grid=(PER_GROUP, ntiles),
            in_specs=[pl.BlockSpec((MOE_TILE, PAY_W), lambda e, i, tg, tb, nt: (tb[i], 0)), wspec, wspec],
            out_specs=pl.BlockSpec((MOE_TILE, de), lambda e, i, tg, tb, nt: (tb[i], e)),
            scratch_shapes=[pltpu.VMEM((d, de), BF16), pltpu.VMEM((d, de), BF16)]),
        out_shape=jax.ShapeDtypeStruct((rows, PER_GROUP * de), BF16),
        compiler_params=_cparams("arbitrary", "arbitrary"), name="moe_up")(tg, tb, nt, xs, wg, wu)


def _moe_down_kernel(tg_ref, tb_ref, nt_ref, x_ref, w_ref, o_ref, ws):
    i = pl.program_id(1)
    changed = (i == 0) | (tg_ref[i] != tg_ref[jnp.maximum(i - 1, 0)])

    @pl.when(changed)
    def _():
        ws[...] = w_ref[...].astype(BF16)

    @pl.when(i < nt_ref[0])
    def _():
        o_ref[...] = _dot(x_ref[...], ws[...])


def _moe_down(tg, tb, nt, hid, wd, l, tn=1024):
    rows, k = hid.shape
    ntiles = tg.shape[0]
    d = wd.shape[3]
    return pl.pallas_call(
        _moe_down_kernel,
        grid_spec=pltpu.PrefetchScalarGridSpec(
            num_scalar_prefetch=3, grid=(d // tn, ntiles),
            in_specs=[pl.BlockSpec((MOE_TILE, k), lambda c, i, tg, tb, nt: (tb[i], 0)),
                      pl.BlockSpec((None, None, k, tn), lambda c, i, tg, tb, nt: (l, tg[i], 0, c))],
            out_specs=pl.BlockSpec((MOE_TILE, tn), lambda c, i, tg, tb, nt: (tb[i], c)),
            scratch_shapes=[pltpu.VMEM((k, tn), BF16)]),
        out_shape=jax.ShapeDtypeStruct((rows, d), F32),
        compiler_params=_cparams("arbitrary", "arbitrary"), name="moe_down")(tg, tb, nt, hid, wd)


def _gather_ln_kernel(dest_ref, y_ref, h_ref, g_ref, b_ref, of_ref, ob_ref, buf, sem, *, nsteps):
    tm = h_ref.shape[0]
    i = pl.program_id(0)

    def issue(step, slot):
        base = step * tm
        for r in range(tm):
            d = dest_ref[base + r]
            pltpu.make_async_copy(y_ref.at[pl.ds(d, 1)], buf.at[slot, pl.ds(r, 1)],
                                  sem.at[slot]).start(priority=r % 2)

    @pl.when(i == 0)
    def _():
        issue(0, 0)

    @pl.when(i + 1 < nsteps)
    def _():
        issue(i + 1, (i + 1) % 2)

    slot = i % 2
    pltpu.make_async_copy(y_ref.at[pl.ds(0, tm)], buf.at[slot], sem.at[slot]).wait()
    o = _layer_norm(ALPHA * h_ref[...] + buf[slot], g_ref[...], b_ref[...])
    of_ref[...] = o
    ob_ref[...] = o.astype(BF16)


def _gather_ln(dest, ys, hf, g, b, tm=256):
    n, d = hf.shape
    nsteps = n // tm
    row = pl.BlockSpec((tm, d), lambda i, dref: (i, 0))
    par = pl.BlockSpec((1, d), lambda i, dref: (0, 0))
    return pl.pallas_call(
        functools.partial(_gather_ln_kernel, nsteps=nsteps),
        grid_spec=pltpu.PrefetchScalarGridSpec(
            num_scalar_prefetch=1, grid=(nsteps,),
            in_specs=[pl.BlockSpec(memory_space=pl.ANY), row, par, par],
            out_specs=[row, row],
            scratch_shapes=[pltpu.VMEM((2, tm, d), F32), pltpu.SemaphoreType.DMA((2,))]),
        out_shape=[jax.ShapeDtypeStruct((n, d), F32), jax.ShapeDtypeStruct((n, d), BF16)],
        compiler_params=_cparams("arbitrary"), name="moe_gather_ln3")(
            dest, ys, hf, g.reshape(1, d), b.reshape(1, d))


def _moe(hf, p, w, l):
    n = hf.shape[0]
    region = n + MOE_TILE
    ntiles = n // MOE_TILE + N_GROUPS
    pay, dest, cnt = _route(hf, w["w_router_t"], w["router_bias_c"], region)
    dest = dest.reshape(n)
    cnt = cnt[:N_GROUPS, 0]
    tg, tb, nt = _tile_tables(cnt, region, ntiles)
    xs = _scatter(dest, cnt, pay, region)
    hid = _moe_up(tg, tb, nt, xs, p["moe_w_gate"], p["moe_w_up"], l)
    ys = _moe_down(tg, tb, nt, hid, w["moe_wd"], l)
    return _gather_ln(dest, ys, hf, p["ln3_g"][l], p["ln3_b"][l])


def _prep_params(p):
    L = p["w_in"].shape[0]
    half = MLA_ROPE // 2
    pad_r = LANES - MLA_ROPE

    wq = p["mla_w_uq"].reshape(L, MLA_Q_RANK, N_MLA, MLA_NOPE + MLA_ROPE)
    rq = wq[..., MLA_NOPE:]
    zq = jnp.zeros((L, MLA_Q_RANK, N_MLA, pad_r), F32)
    wuq = jnp.concatenate([wq[..., :MLA_NOPE], rq, zq, rq[..., half:], rq[..., :half], zq], axis=-1)
    wkv = p["mla_w_ukv"].reshape(L, MLA_KV_RANK, N_MLA, MLA_NOPE + MLA_V)

    bf = lambda a: a.astype(BF16)
    ne = p["moe_w_down"].shape[1]
    return dict(
        wt_in=jnp.swapaxes(p["w_in"], 1, 2),
        wuq=bf(wuq.reshape(L, MLA_Q_RANK, N_MLA * 3 * LANES)),
        wuk=bf(wkv[..., :MLA_NOPE].reshape(L, MLA_KV_RANK, N_MLA * MLA_NOPE)),
        wuv=bf(wkv[..., MLA_NOPE:].reshape(L, MLA_KV_RANK, N_MLA * MLA_V)),
        gla_wg=jnp.pad(p["gla_w_gate"], ((0, 0), (0, LANES - GLA_RANK), (0, 0))),
        w_branch=bf(p["w_branch"]), w_out=bf(p["w_out"]),
        x_w_q=bf(p["x_w_q"]), x_w_kv=bf(p["x_w_kv"]), x_w_o=bf(p["x_w_o"]),
        w_router_t=jnp.pad(p["w_router"].T, ((0, LANES - N_EXPERTS), (0, 0))),
        router_bias_c=p["router_bias"].reshape(N_EXPERTS, 1),
        moe_wd=p["moe_w_down"].reshape(L, N_GROUPS, (ne // N_GROUPS) * D_EXPERT, D_MODEL),
    )


def _mixer(hf, hb, cos_p, sin_p, p, w, l, batch, seq):
    wt = w["wt_in"]
    o_mla = 2 * D_SG
    o_gla = o_mla + MLA_Q_RANK + MLA_KV_RANK + MLA_ROPE
    o_ml = o_gla + 2 * N_GLA * GLA_DK + 2 * N_GLA * GLA_DV + GLA_RANK
    o_gate = o_ml + 2 * N_ML * ML_DK + 2 * N_ML * ML_DV + 2 * N_ML
    z_sg = _mm_wt(hb, wt, l, 0, 1, 2 * D_SG, F32, name="mm_sg")
    z_mla = _mm_wt(hb, wt, l, o_mla, 1, MLA_Q_RANK + MLA_KV_RANK + LANES, F32, name="mm_mla")
    gates = _mm_wt(hb, wt, l, o_gate, 4 * D_MODEL // 1024, 1024, BF16, act="sigmoid", name="mm_gate")

    y_a = _sg(z_sg, p["sg_vnorm_g"][l], p["sg_vnorm_b"][l], p["sg_w_s"][l], p["sg_b_s"][l])
    q, k, v = _mla_prep(z_mla, cos_p, sin_p, p["mla_qnorm_g"][l], p["mla_kvnorm_g"][l],
                        w["wuq"][l], w["wuk"][l], w["wuv"][l])
    y_b = _mla_attn(q, k, v, batch, seq)
    y_c = _gla(hb, wt, l, o_gla, w["gla_wg"][l], p["gla_b_gate"][l], p["gla_norm_g"][l], batch, seq)
    y_d = _mlstm(hb, wt, l, o_ml, p["ml_conv_w"][l], p["ml_conv_b"][l], p["ml_gate_b"][l], p["ml_norm_g"][l],
                 batch, seq)
    return _merge_out(y_a, y_b, y_c, y_d, gates, w["w_branch"][l], w["w_out"][l], hf,
                      p["ln1_g"][l], p["ln1_b"][l])


def _forward(p):
    x = p["x"]
    batch, seq, d = x.shape
    n = batch * seq
    mem = p["mem"]
    mem_len = mem.shape[1]
    w = _prep_params(p)
    posb = jnp.broadcast_to(p["positions"].reshape(n, 1).astype(F32), (n, LANES))
    cos_p, sin_p = _rope_tables(posb)
    memb = mem.reshape(batch * mem_len, d).astype(BF16)
    hf, hb = _ln(x.reshape(n, d), p["ln_in_g"], p["ln_in_b"])
    for l in range(p["w_in"].shape[0]):
        hf, hb = _mixer(hf, hb, cos_p, sin_p, p, w, l, batch, seq)
        kv = _mm(memb, w["x_w_kv"][l], BF16, tm=512, tn=1024, name="mm_xkv")
        hf, hb = _xattn(hb, hf, w["x_w_q"][l], kv, w["x_w_o"][l], p["ln2_g"][l], p["ln2_b"][l],
                        batch, seq, mem_len)
        hf, hb = _moe(hf, p, w, l)
    return hf.reshape(batch, seq, d)


def kernel(x, mem, positions, ln_in_g, ln_in_b, w_in, sg_vnorm_g, sg_vnorm_b, sg_w_s, sg_b_s, mla_qnorm_g, mla_kvnorm_g, mla_w_uq, mla_w_ukv, gla_w_gate, gla_b_gate, gla_norm_g, ml_conv_w, ml_conv_b, ml_gate_b, ml_norm_g, w_branch, w_out, ln1_g, ln1_b, x_w_q, x_w_kv, x_w_o, ln2_g, ln2_b, w_router, router_bias, moe_w_gate, moe_w_up, moe_w_down, ln3_g, ln3_b):
    return _forward(dict(
        x=x, mem=mem, positions=positions, ln_in_g=ln_in_g, ln_in_b=ln_in_b, w_in=w_in,
        sg_vnorm_g=sg_vnorm_g, sg_vnorm_b=sg_vnorm_b, sg_w_s=sg_w_s, sg_b_s=sg_b_s,
        mla_qnorm_g=mla_qnorm_g, mla_kvnorm_g=mla_kvnorm_g, mla_w_uq=mla_w_uq, mla_w_ukv=mla_w_ukv,
        gla_w_gate=gla_w_gate, gla_b_gate=gla_b_gate, gla_norm_g=gla_norm_g,
        ml_conv_w=ml_conv_w, ml_conv_b=ml_conv_b, ml_gate_b=ml_gate_b, ml_norm_g=ml_norm_g,
        w_branch=w_branch, w_out=w_out, ln1_g=ln1_g, ln1_b=ln1_b,
        x_w_q=x_w_q, x_w_kv=x_w_kv, x_w_o=x_w_o, ln2_g=ln2_g, ln2_b=ln2_b,
        w_router=w_router, router_bias=router_bias,
        moe_w_gate=moe_w_gate, moe_w_up=moe_w_up, moe_w_down=moe_w_down, ln3_g=ln3_g, ln3_b=ln3_b))
```

```python
import functools
import math

import jax
import jax.numpy as jnp
from jax import lax
from jax.experimental import pallas as pl
from jax.experimental.pallas import tpu as pltpu

F32 = jnp.float32
BF16 = jnp.bfloat16

D_MODEL = 2048
DEPTH = 4
EPS = 1e-5
ALPHA = (2.0 * DEPTH) ** 0.25

SG_CHUNK = 128
N_SG = 4
D_SG = 512
N_MLA = 4
MLA_Q_RANK = 384
MLA_KV_RANK = 256
MLA_NOPE = 128
MLA_ROPE = 64
MLA_V = 128
MLA_QK = 256
ROPE_BASE = 10000.0
N_GLA = 4
GLA_DK = 64
GLA_DV = 128
GLA_RANK = 16
GLA_TAU = 16.0
GLA_CHUNK = 64
N_ML = 4
ML_DK = 64
ML_DV = 128
ML_CHUNK = 128
N_X = 4
X_HEAD = 128
N_EXPERTS = 16
D_EXPERT = 512

LANES = 128
VMEM_LIMIT = 48 * 1024 * 1024
VMEM_LIMIT_BIG = 56 * 1024 * 1024

_NT = (((1,), (1,)), ((), ()))
_TN = (((0,), (0,)), ((), ()))


def _cparams(*sem, vmem=VMEM_LIMIT):
    return pltpu.CompilerParams(dimension_semantics=sem, vmem_limit_bytes=vmem)


def _dot(a, b):
    return jnp.dot(a, b, preferred_element_type=F32)


def _dotg(a, b, dims):
    return lax.dot_general(a, b, dims, preferred_element_type=F32)


def _split3(a):
    a1 = a.astype(BF16)
    r1 = a - a1.astype(F32)
    a2 = r1.astype(BF16)
    a3 = (r1 - a2.astype(F32)).astype(BF16)
    return a1, a2, a3


def _dot_exact_rhs(a, ones_bf16):
    a1, a2, a3 = _split3(a)
    return _dot(a1, ones_bf16) + _dot(a2, ones_bf16) + _dot(a3, ones_bf16)


def _dot_exact_lhs(ones_bf16, a):
    a1, a2, a3 = _split3(a)
    return _dot(ones_bf16, a1) + _dot(ones_bf16, a2) + _dot(ones_bf16, a3)


def _dot_hi(a, b):
    a1 = a.astype(BF16)
    a2 = (a - a1.astype(F32)).astype(BF16)
    b1 = b.astype(BF16)
    b2 = (b - b1.astype(F32)).astype(BF16)
    return _dot(a1, b1) + _dot(a2, b1) + _dot(a1, b2)


def _sigmoid(x):
    return 1.0 / (1.0 + jnp.exp(-x))


def _log_sigmoid(x):
    return jnp.minimum(x, 0.0) - jnp.log(1.0 + jnp.exp(-jnp.abs(x)))


def _layer_norm(t, g, b):
    mu = jnp.mean(t, axis=-1, keepdims=True)
    c = t - mu
    var = jnp.mean(c * c, axis=-1, keepdims=True)
    return c * lax.rsqrt(var + EPS) * g + b


def _rms_norm(t, g):
    return t * lax.rsqrt(jnp.mean(t * t, axis=-1, keepdims=True) + EPS) * g


def _ln_kernel(x_ref, g_ref, b_ref, of_ref, ob_ref):
    y = _layer_norm(x_ref[...], g_ref[...], b_ref[...])
    of_ref[...] = y
    ob_ref[...] = y.astype(BF16)


def _ln(x, g, b, tm=256):
    n, d = x.shape
    row = pl.BlockSpec((tm, d), lambda i: (i, 0))
    par = pl.BlockSpec((1, d), lambda i: (0, 0))
    return pl.pallas_call(
        _ln_kernel, grid=(n // tm,), in_specs=[row, par, par], out_specs=[row, row],
        out_shape=[jax.ShapeDtypeStruct((n, d), F32), jax.ShapeDtypeStruct((n, d), BF16)],
        compiler_params=_cparams("parallel"), name="ln_in")(x, g.reshape(1, d), b.reshape(1, d))


def _mm_kernel(x_ref, w_ref, o_ref, *, act):
    acc = _dot(x_ref[...], w_ref[...])
    if act == "sigmoid":
        acc = _sigmoid(acc)
    o_ref[...] = acc.astype(o_ref.dtype)


def _mm(x, w, out_dtype, act=None, tm=512, tn=None, name="mm"):
    n, k = x.shape
    m = w.shape[1]
    tn = m if tn is None else tn
    tm = min(tm, n)
    assert n % tm == 0 and m % tn == 0
    return pl.pallas_call(
        functools.partial(_mm_kernel, act=act), grid=(n // tm, m // tn),
        in_specs=[pl.BlockSpec((tm, k), lambda i, j: (i, 0)),
                  pl.BlockSpec((k, tn), lambda i, j: (0, j))],
        out_specs=pl.BlockSpec((tm, tn), lambda i, j: (i, j)),
        out_shape=jax.ShapeDtypeStruct((n, m), out_dtype),
        compiler_params=_cparams("parallel", "parallel"), name=name)(x, w)


def _mm_wt_kernel(x_ref, w_ref, o_ref, ws, *, act):
    @pl.when(pl.program_id(1) == 0)
    def _():
        ws[...] = w_ref[0].astype(BF16)

    acc = _dotg(x_ref[...], ws[...], _NT)
    if act == "sigmoid":
        acc = _sigmoid(acc)
    o_ref[...] = acc.astype(o_ref.dtype)


def _mm_wt(x, wt_all, l, row0, nblk, tn, out_dtype, act=None, tm=1024, name="mm_wt"):
    n, k = x.shape
    assert row0 % 8 == 0
    wmode = dict(pipeline_mode=pl.Buffered(1)) if nblk == 1 else {}
    return pl.pallas_call(
        functools.partial(_mm_wt_kernel, act=act), grid=(nblk, n // tm),
        in_specs=[pl.BlockSpec((tm, k), lambda j, i: (i, 0)),
                  pl.BlockSpec((pl.Element(1), pl.Element(tn), pl.Element(k)),
                               lambda j, i: (l, pl.multiple_of(row0 + j * tn, 8), 0), **wmode)],
        out_specs=pl.BlockSpec((tm, tn), lambda j, i: (i, j)),
        out_shape=jax.ShapeDtypeStruct((n, nblk * tn), out_dtype),
        scratch_shapes=[pltpu.VMEM((tn, k), BF16)],
        compiler_params=_cparams("arbitrary", "arbitrary"), name=name)(x, wt_all)


def _sg_kernel(z_ref, vg_ref, vb_ref, ws_ref, bst_ref, o_ref, *, nchunk):
    z = z_ref[...]
    z = 0.5 * z * (1.0 + jnp.tanh(math.sqrt(2.0 / math.pi) * (z + 0.044715 * (z * z * z))))
    u = z[:, :D_SG]
    vn = _layer_norm(z[:, D_SG:], vg_ref[...], vb_ref[...]).astype(BF16)
    r = lax.broadcasted_iota(jnp.int32, (SG_CHUNK, SG_CHUNK), 0)
    c = lax.broadcasted_iota(jnp.int32, (SG_CHUNK, SG_CHUNK), 1)
    causal = c <= r
    gw = SG_CHUNK
    for g in range(N_SG):
        w = jnp.where(causal, ws_ref[g], 0.0).astype(BF16)
        bias = bst_ref[:, g:g + 1]
        for ci in range(nchunk):
            rs = slice(ci * SG_CHUNK, (ci + 1) * SG_CHUNK)
            cs = slice(g * gw, (g + 1) * gw)
            mixed = _dot(w, vn[rs, cs]) + bias
            o_ref[rs, cs] = (u[rs, cs] * mixed).astype(BF16)


def _sg(z, vg, vb, ws, bs, tm=512):
    n = z.shape[0]
    return pl.pallas_call(
        functools.partial(_sg_kernel, nchunk=tm // SG_CHUNK), grid=(n // tm,),
        in_specs=[pl.BlockSpec((tm, 2 * D_SG), lambda i: (i, 0)),
                  pl.BlockSpec((1, D_SG), lambda i: (0, 0)),
                  pl.BlockSpec((1, D_SG), lambda i: (0, 0)),
                  pl.BlockSpec((N_SG, SG_CHUNK, SG_CHUNK), lambda i: (0, 0, 0)),
                  pl.BlockSpec((SG_CHUNK, N_SG), lambda i: (0, 0))],
        out_specs=pl.BlockSpec((tm, D_SG), lambda i: (i, 0)),
        out_shape=jax.ShapeDtypeStruct((n, D_SG), BF16),
        compiler_params=_cparams("parallel"), name="sg")(
            z, vg.reshape(1, D_SG), vb.reshape(1, D_SG), ws, bs.T)


def _rope_kernel(pos_ref, cos_ref, sin_ref):
    half = MLA_ROPE // 2
    lane = lax.broadcasted_iota(jnp.int32, (1, LANES), 1)
    idx = jnp.bitwise_and(lane, half - 1).astype(F32)
    freq = jnp.exp(idx * (-math.log(ROPE_BASE) / half))
    ang = pos_ref[...] * freq
    c = jnp.cos(ang)
    s = jnp.sin(ang)
    cos_ref[...] = jnp.where(lane < MLA_ROPE, c, 0.0)
    sin_ref[...] = jnp.where(lane < half, -s, jnp.where(lane < MLA_ROPE, s, 0.0))


def _rope_tables(posb, tm=512):
    n = posb.shape[0]
    row = pl.BlockSpec((tm, LANES), lambda i: (i, 0))
    return pl.pallas_call(
        _rope_kernel, grid=(n // tm,), in_specs=[row], out_specs=[row, row],
        out_shape=[jax.ShapeDtypeStruct((n, LANES), F32)] * 2,
        compiler_params=_cparams("parallel"), name="rope_tables")(posb)


def _mla_prep_kernel(z_ref, cos_ref, sin_ref, qg_ref, kvg_ref, wuq_ref, wuk_ref, wuv_ref,
                     q_ref, k_ref, v_ref):
    z = z_ref[...]
    cq = _rms_norm(z[:, :MLA_Q_RANK], qg_ref[...]).astype(BF16)
    o1 = MLA_Q_RANK + MLA_KV_RANK
    ckv = _rms_norm(z[:, MLA_Q_RANK:o1], kvg_ref[...]).astype(BF16)
    cos_p = cos_ref[...]
    sin_p = sin_ref[...]
    kr = z[:, o1:o1 + LANES]
    half = MLA_ROPE // 2
    lane = lax.broadcasted_iota(jnp.int32, (1, LANES), 1)
    kr_swapped = jnp.where(lane < half, pltpu.roll(kr, LANES - half, 1), pltpu.roll(kr, half, 1))
    k_tail = (kr * cos_p + kr_swapped * sin_p).astype(BF16)
    qa = _dot(cq, wuq_ref[...])
    kn = _dot(ckv, wuk_ref[...])
    v_ref[...] = _dot(ckv, wuv_ref[...]).astype(BF16)
    scale = (MLA_NOPE + MLA_ROPE) ** -0.5
    for h in range(N_MLA):
        b0 = h * 3 * LANES
        q_tail = qa[:, b0 + LANES:b0 + 2 * LANES] * cos_p + qa[:, b0 + 2 * LANES:b0 + 3 * LANES] * sin_p
        q_ref[:, h * MLA_QK:h * MLA_QK + LANES] = (qa[:, b0:b0 + LANES] * scale).astype(BF16)
        q_ref[:, h * MLA_QK + LANES:(h + 1) * MLA_QK] = (q_tail * scale).astype(BF16)
        k_ref[:, h * MLA_QK:h * MLA_QK + LANES] = kn[:, h * LANES:(h + 1) * LANES].astype(BF16)
        k_ref[:, h * MLA_QK + LANES:(h + 1) * MLA_QK] = k_tail


def _mla_prep(z, cos_p, sin_p, qg, kvg, wuq, wuk, wuv, tm=512):
    n, zw = z.shape
    row = lambda w: pl.BlockSpec((tm, w), lambda i: (i, 0))
    full = lambda a: pl.BlockSpec(a.shape, lambda i: (0,) * a.ndim)
    qg = qg.reshape(1, -1)
    kvg = kvg.reshape(1, -1)
    return pl.pallas_call(
        _mla_prep_kernel, grid=(n // tm,),
        in_specs=[row(zw), row(LANES), row(LANES), full(qg), full(kvg), full(wuq), full(wuk), full(wuv)],
        out_specs=[row(N_MLA * MLA_QK), row(N_MLA * MLA_QK), row(N_MLA * MLA_V)],
        out_shape=[jax.ShapeDtypeStruct((n, N_MLA * MLA_QK), BF16),
                   jax.ShapeDtypeStruct((n, N_MLA * MLA_QK), BF16),
                   jax.ShapeDtypeStruct((n, N_MLA * MLA_V), BF16)],
        compiler_params=_cparams("parallel"), name="mla_prep")(z, cos_p, sin_p, qg, kvg, wuq, wuk, wuv)


def _mla_attn_kernel(q_ref, k_ref, v_ref, o_ref, *, t, nblk):
    krow = lax.broadcasted_iota(jnp.int32, (t, t), 0)
    qcol = lax.broadcasted_iota(jnp.int32, (t, t), 1)
    visible = krow <= qcol
    for i in range(nblk):
        q = q_ref[i * t:(i + 1) * t, :]
        m = jnp.full((1, t), -jnp.inf, F32)
        l = jnp.zeros((1, t), F32)
        acc = jnp.zeros((MLA_V, t), F32)
        for j in range(i + 1):
            ks = slice(j * t, (j + 1) * t)
            s = _dotg(k_ref[ks, :], q, _NT)
            if j == i:
                s = jnp.where(visible, s, -jnp.inf)
            m_new = jnp.maximum(m, jnp.max(s, axis=0, keepdims=True))
            p = jnp.exp(s - m_new)
            a = jnp.exp(m - m_new)
            l = a * l + jnp.sum(p, axis=0, keepdims=True)
            acc = a * acc + _dotg(v_ref[ks, :], p.astype(BF16), _TN)
            m = m_new
        o_ref[i * t:(i + 1) * t, :] = (acc / l).T.astype(BF16)


def _mla_attn(q, k, v, batch, seq, t=512):
    n = q.shape[0]
    return pl.pallas_call(
        functools.partial(_mla_attn_kernel, t=t, nblk=seq // t), grid=(batch, N_MLA),
        in_specs=[pl.BlockSpec((seq, MLA_QK), lambda b, h: (b, h)),
                  pl.BlockSpec((seq, MLA_QK), lambda b, h: (b, h)),
                  pl.BlockSpec((seq, MLA_V), lambda b, h: (b, h))],
        out_specs=pl.BlockSpec((seq, MLA_V), lambda b, h: (b, h)),
        out_shape=jax.ShapeDtypeStruct((n, N_MLA * MLA_V), BF16),
        compiler_params=_cparams("parallel", "parallel"), name="mla_attn")(q, k, v)


def _gla_tile(z, o_ref, row0, st, wg_ref, bg_ref, ng_ref, nchunk):
    L = GLA_CHUNK
    qkw = N_GLA * GLA_DK
    vw = N_GLA * GLA_DV
    c_k, c_v, c_o, c_lr = qkw, 2 * qkw, 2 * qkw + vw, 2 * qkw + 2 * vw
    logits = _dot_hi(z[:, c_lr:c_lr + LANES], wg_ref[...]) + bg_ref[...]
    log_a = _log_sigmoid(logits) * (1.0 / GLA_TAU)
    lane = lax.broadcasted_iota(jnp.int32, (1, qkw), 1)
    masks = [((lane >= h * GLA_DK) & (lane < (h + 1) * GLA_DK)).astype(F32) for h in range(N_GLA)]
    r = lax.broadcasted_iota(jnp.int32, (L, L), 0)
    c = lax.broadcasted_iota(jnp.int32, (L, L), 1)
    causal = c <= r
    tril = jnp.where(causal, 1.0, 0.0).astype(BF16)
    ng = ng_ref[...]
    for ci in range(nchunk):
        rs = slice(ci * L, (ci + 1) * L)
        ro = slice(row0 + ci * L, row0 + (ci + 1) * L)
        b = _dot_exact_lhs(tril, log_a[rs])
        b_last = b[L - 1:L, :]
        q = z[rs, 0:qkw] * (GLA_DK ** -0.5)
        k = z[rs, c_k:c_k + qkw]
        qt = q * jnp.exp(b)
        kt = (k * jnp.exp(-b)).astype(BF16)
        kd = (k * jnp.exp(b_last - b)).astype(BF16)
        qstack = jnp.concatenate([qt * masks[h] for h in range(N_GLA)], axis=0).astype(BF16)
        att = _dotg(qstack, kt, _NT)
        inter = _dotg(qstack, st.astype(BF16), _NT)
        vb = z[rs, c_v:c_v + vw].astype(BF16)
        for h in range(N_GLA):
            hs = slice(h * L, (h + 1) * L)
            vs = slice(h * GLA_DV, (h + 1) * GLA_DV)
            a_h = jnp.where(causal, att[hs], 0.0).astype(BF16)
            o_h = _rms_norm(_dot(a_h, vb[:, vs]) + inter[hs], ng)
            g = z[rs, c_o + h * GLA_DV:c_o + (h + 1) * GLA_DV]
            o_ref[ro, vs] = (o_h * (g * _sigmoid(g))).astype(BF16)
        upd = _dotg(vb, kd, _TN)
        new = st * jnp.exp(b_last)
        for h in range(N_GLA):
            new = new + upd[h * GLA_DV:(h + 1) * GLA_DV] * masks[h]
        st = new
    return st


def _gla_kernel(x_ref, xn_ref, w_ref, wg_ref, bg_ref, ng_ref, o_ref, ws, za, zb, st_ref, *, tm):
    first = pl.program_id(1) == 0

    @pl.when((pl.program_id(0) == 0) & first)
    def _():
        ws[...] = w_ref[0].astype(BF16)

    @pl.when(first)
    def _():
        st_ref[...] = jnp.zeros_like(st_ref)
        za[...] = _dotg(x_ref[0:tm, :], ws[...], _NT)

    nchunk = tm // GLA_CHUNK
    zb[...] = _dotg(x_ref[tm:2 * tm, :], ws[...], _NT)
    st = _gla_tile(za, o_ref, 0, st_ref[...], wg_ref, bg_ref, ng_ref, nchunk)
    za[...] = _dotg(xn_ref[...], ws[...], _NT)
    st_ref[...] = _gla_tile(zb, o_ref, tm, st, wg_ref, bg_ref, ng_ref, nchunk)


def _gla(hb, wt, l, row0, wg, bg, ng, batch, seq, tm=256):
    n, d = hb.shape
    ns2 = seq // (2 * tm)
    qkw = N_GLA * GLA_DK
    vw = N_GLA * GLA_DV
    zw = 2 * qkw + 2 * vw + LANES
    full = lambda a: pl.BlockSpec(a.shape, lambda b, s: (0,) * a.ndim)
    bg = bg.reshape(1, qkw)
    ng = ng.reshape(1, GLA_DV)
    last = 2 * ns2 - 1
    return pl.pallas_call(
        functools.partial(_gla_kernel, tm=tm), grid=(batch, ns2),
        in_specs=[pl.BlockSpec((2 * tm, d), lambda b, s: (b * ns2 + s, 0)),
                  pl.BlockSpec((tm, d), lambda b, s: (b * 2 * ns2 + jnp.minimum(2 * s + 2, last), 0)),
                  pl.BlockSpec((pl.Element(1), pl.Element(zw), pl.Element(d)), lambda b, s: (l, row0, 0),
                               pipeline_mode=pl.Buffered(1)),
                  full(wg), full(bg), full(ng)],
        out_specs=pl.BlockSpec((2 * tm, vw), lambda b, s: (b * ns2 + s, 0)),
        out_shape=jax.ShapeDtypeStruct((n, vw), BF16),
        scratch_shapes=[pltpu.VMEM((zw, d), BF16), pltpu.VMEM((tm, zw), F32), pltpu.VMEM((tm, zw), F32),
                        pltpu.VMEM((GLA_DV, qkw), F32)],
        compiler_params=_cparams("arbitrary", "arbitrary"), name="gla")(hb, hb, wt, wg, bg, ng)


def _mlstm_tile(z, o_ref, row0, carry, cw_ref, cb_ref, gb_ref, ng_ref, nchunk):
    L = ML_CHUNK
    qkw = N_ML * ML_DK
    vw = N_ML * ML_DV
    tm = nchunk * L
    c_v, c_o, c_if = 2 * qkw, 2 * qkw + vw, 2 * qkw + 2 * vw
    ct, nrow, mwide, mk, tail = carry

    x = z[:, 0:2 * qkw]
    row8 = lax.broadcasted_iota(jnp.int32, (8, 2 * qkw), 0)
    acc = x * cw_ref[3:4, :] + cb_ref[...]
    for j in range(1, 4):
        rx = pltpu.roll(x, j, 0)
        fix = jnp.where(row8 < j, pltpu.roll(tail, j, 0), rx[0:8])
        acc = acc + jnp.concatenate([fix, rx[8:]], axis=0) * cw_ref[3 - j:4 - j, :]
    tail = x[tm - 8:tm]
    y = acc * _sigmoid(acc)
    q = y[:, :qkw] * (ML_DK ** -0.5)
    k = y[:, qkw:]

    gates = z[:, c_if:c_if + LANES] + gb_ref[...]
    fc = _log_sigmoid(gates)
    gt = gates.T[0:2 * N_ML]
    fct = _log_sigmoid(gt)

    lane = lax.broadcasted_iota(jnp.int32, (1, qkw), 1)
    masks = [((lane >= h * ML_DK) & (lane < (h + 1) * ML_DK)).astype(F32) for h in range(N_ML)]
    r = lax.broadcasted_iota(jnp.int32, (L, L), 0)
    c = lax.broadcasted_iota(jnp.int32, (L, L), 1)
    causal = c <= r
    tril = jnp.where(causal, 1.0, 0.0).astype(BF16)
    triu = jnp.where(r <= c, 1.0, 0.0).astype(BF16)

    def selector(width, block, first):
        rr = lax.broadcasted_iota(jnp.int32, (LANES, width), 0)
        cc = lax.broadcasted_iota(jnp.int32, (LANES, width), 1)
        return jnp.where(rr == (cc >> int(math.log2(block))) + first, 1.0, 0.0).astype(BF16)

    wide = N_ML * L
    fcb = _dot_exact_rhs(fc, selector(wide, L, N_ML))
    icb = _dot_exact_rhs(gates, selector(wide, L, 0))
    fck = _dot_exact_rhs(fc, selector(qkw, ML_DK, N_ML))
    ick = _dot_exact_rhs(gates, selector(qkw, ML_DK, 0))

    ng = ng_ref[...]

    for ci in range(nchunk):
        rs = slice(ci * L, (ci + 1) * L)
        ro = slice(row0 + ci * L, row0 + (ci + 1) * L)
        bb = _dot_exact_lhs(tril, fcb[rs])
        bk = _dot_exact_lhs(tril, fck[rs])
        brow = _dot_exact_rhs(fct[:, rs], triu)
        rowterm = gt[:, rs] - pltpu.roll(brow, N_ML, 0)
        qc = q[rs]
        kc = k[rs]
        log_d = jnp.concatenate(
            [jnp.where(causal, bb[:, h * L:(h + 1) * L] + rowterm[h:h + 1, :], -jnp.inf) for h in range(N_ML)],
            axis=0)
        log_inter = jnp.concatenate(
            [bb[:, h * L:(h + 1) * L] + mwide[:, h * L:(h + 1) * L] for h in range(N_ML)], axis=0)
        m_t = jnp.maximum(log_inter, jnp.max(log_d, axis=-1, keepdims=True))
        w_inter = jnp.exp(log_inter - m_t)
        qst = jnp.concatenate([qc * masks[h] for h in range(N_ML)], axis=0)
        qsb = qst.astype(BF16)
        s_all = _dotg(qsb, kc.astype(BF16), _NT) * jnp.exp(log_d - m_t)
        sb = s_all.astype(BF16)
        vb = z[rs, c_v:c_v + vw].astype(BF16)
        num = jnp.concatenate(
            [_dot(sb[h * L:(h + 1) * L], vb[:, h * ML_DV:(h + 1) * ML_DV]) for h in range(N_ML)], axis=0)
        num = num + w_inter * _dotg(qsb, ct.astype(BF16), _NT)
        den = jnp.sum(s_all, axis=-1, keepdims=True) + w_inter * jnp.sum(qst * nrow, axis=-1, keepdims=True)
        hh = _rms_norm(num / jnp.maximum(jnp.abs(den), jnp.exp(-m_t)), ng)
        for h in range(N_ML):
            vs = slice(h * ML_DV, (h + 1) * ML_DV)
            gate_o = _sigmoid(z[rs, c_o + h * ML_DV:c_o + (h + 1) * ML_DV])
            o_ref[ro, vs] = (hh[h * L:(h + 1) * L] * gate_o).astype(BF16)
        bl_w = bb[L - 1:L, :]
        lw_w = bl_w - bb + icb[rs]
        mwide_new = jnp.maximum(bl_w + mwide, jnp.max(lw_w, axis=0, keepdims=True))
        bl_k = bk[L - 1:L, :]
        lw_k = bl_k - bk + ick[rs]
        mk_new = jnp.maximum(bl_k + mk, jnp.max(lw_k, axis=0, keepdims=True))
        decay = jnp.exp(bl_k + mk - mk_new)
        kw = kc * jnp.exp(lw_k - mk_new)
        upd = _dotg(vb, kw.astype(BF16), _TN)
        ct = ct * decay
        for h in range(N_ML):
            ct = ct + upd[h * ML_DV:(h + 1) * ML_DV] * masks[h]
        nrow = nrow * decay + jnp.sum(kw, axis=0, keepdims=True)
        mwide = mwide_new
        mk = mk_new

    return ct, nrow, mwide, mk, tail


def _mlstm_kernel(x_ref, xn_ref, w_ref, cw_ref, cb_ref, gb_ref, ng_ref, o_ref,
                  ws, za, zb, ct_ref, n_ref, mw_ref, mk_ref, tail_ref, *, tm):
    first = pl.program_id(1) == 0

    @pl.when((pl.program_id(0) == 0) & first)
    def _():
        ws[...] = w_ref[0].astype(BF16)

    @pl.when(first)
    def _():
        for ref in (ct_ref, n_ref, mw_ref, mk_ref, tail_ref):
            ref[...] = jnp.zeros_like(ref)
        za[...] = _dotg(x_ref[0:tm, :], ws[...], _NT)

    nchunk = tm // ML_CHUNK
    consts = (cw_ref, cb_ref, gb_ref, ng_ref, nchunk)
    zb[...] = _dotg(x_ref[tm:2 * tm, :], ws[...], _NT)
    carry = (ct_ref[...], n_ref[...], mw_ref[...], mk_ref[...], tail_ref[...])
    carry = _mlstm_tile(za, o_ref, 0, carry, *consts)
    za[...] = _dotg(xn_ref[...], ws[...], _NT)
    carry = _mlstm_tile(zb, o_ref, tm, carry, *consts)
    for ref, val in zip((ct_ref, n_ref, mw_ref, mk_ref, tail_ref), carry):
        ref[...] = val


def _mlstm(hb, wt, l, row0, cw, cb, gb, ng, batch, seq, tm=512):
    n, d = hb.shape
    ns2 = seq // (2 * tm)
    qkw = N_ML * ML_DK
    vw = N_ML * ML_DV
    zw = 2 * qkw + 2 * vw + LANES
    full = lambda a: pl.BlockSpec(a.shape, lambda b, s: (0,) * a.ndim)
    cb = cb.reshape(1, 2 * qkw)
    gbp = jnp.pad(gb, (0, LANES - 2 * N_ML)).reshape(1, LANES)
    ng = ng.reshape(1, ML_DV)
    last = 2 * ns2 - 1
    return pl.pallas_call(
        functools.partial(_mlstm_kernel, tm=tm), grid=(batch, ns2),
        in_specs=[pl.BlockSpec((2 * tm, d), lambda b, s: (b * ns2 + s, 0)),
                  pl.BlockSpec((tm, d), lambda b, s: (b * 2 * ns2 + jnp.minimum(2 * s + 2, last), 0)),
                  pl.BlockSpec((pl.Element(1), pl.Element(zw), pl.Element(d)), lambda b, s: (l, row0, 0),
                               pipeline_mode=pl.Buffered(1)),
                  full(cw), full(cb), full(gbp), full(ng)],
        out_specs=pl.BlockSpec((2 * tm, vw), lambda b, s: (b * ns2 + s, 0)),
        out_shape=jax.ShapeDtypeStruct((n, vw), BF16),
        scratch_shapes=[pltpu.VMEM((zw, d), BF16), pltpu.VMEM((tm, zw), F32), pltpu.VMEM((tm, zw), F32),
                        pltpu.VMEM((ML_DV, qkw), F32), pltpu.VMEM((1, qkw), F32),
                        pltpu.VMEM((1, N_ML * ML_CHUNK), F32), pltpu.VMEM((1, qkw), F32),
                        pltpu.VMEM((8, 2 * qkw), F32)],
        compiler_params=_cparams("arbitrary", "arbitrary", vmem=VMEM_LIMIT_BIG), name="mlstm")(
            hb, hb, wt, cw, cb, gbp, ng)


def _merge_out_kernel(ya_ref, yb_ref, yc_ref, yd_ref, g_ref, wb_ref, wo_ref, h_ref, lg_ref, lb_ref,
                      of_ref, ob_ref):
    acc = None
    for i, y_ref in enumerate((ya_ref, yb_ref, yc_ref, yd_ref)):
        p = _dot(y_ref[...], wb_ref[i])
        t = g_ref[:, i * D_MODEL:(i + 1) * D_MODEL].astype(F32) * p
        acc = t if acc is None else acc + t
    y = _dot(acc.astype(BF16), wo_ref[...])
    o = _layer_norm(ALPHA * h_ref[...] + y, lg_ref[...], lb_ref[...])
    of_ref[...] = o
    ob_ref[...] = o.astype(BF16)


def _merge_out(ya, yb, yc, yd, gates, wb_all, wo_all, l, hf, lg, lb, tm=256):
    n, bw = ya.shape
    d = hf.shape[1]
    row = pl.BlockSpec((tm, bw), lambda i: (i, 0))
    hrow = pl.BlockSpec((tm, d), lambda i: (i, 0))
    par = pl.BlockSpec((1, d), lambda i: (0, 0))
    once = dict(pipeline_mode=pl.Buffered(1))
    return pl.pallas_call(
        _merge_out_kernel, grid=(n // tm,),
        in_specs=[row, row, row, row, pl.BlockSpec((tm, 4 * d), lambda i: (i, 0)),
                  pl.BlockSpec((None,) + wb_all.shape[1:], lambda i: (l, 0, 0, 0), **once),
                  pl.BlockSpec((None,) + wo_all.shape[1:], lambda i: (l, 0, 0), **once), hrow, par, par],
        out_specs=[hrow, hrow],
        out_shape=[jax.ShapeDtypeStruct((n, d), F32), jax.ShapeDtypeStruct((n, d), BF16)],
        compiler_params=_cparams("parallel"), name="merge_out_ln1")(
            ya, yb, yc, yd, gates, wb_all, wo_all, hf, lg.reshape(1, d), lb.reshape(1, d))


def _xattn_kernel(hb_ref, hf_ref, wq_ref, k_ref, v_ref, wo_ref, g_ref, b_ref, wrt_ref, rbc_ref,
                  of_ref, pay_ref, dest_ref, cnt_ref, carry_ref, *, region):
    q = (_dot(hb_ref[...], wq_ref[...]) * (X_HEAD ** -0.5)).astype(BF16)
    outs = []
    for h in range(N_X):
        hs = slice(h * X_HEAD, (h + 1) * X_HEAD)
        s = _dotg(q[:, hs], k_ref[:, hs], _NT)
        p = jnp.exp(s - jnp.max(s, axis=-1, keepdims=True))
        l = jnp.sum(p, axis=-1, keepdims=True)
        outs.append((_dot(p.astype(BF16), v_ref[:, hs]) / l).astype(BF16))
    y = _dot(jnp.concatenate(outs, axis=-1), wo_ref[...])
    o = _layer_norm(ALPHA * hf_ref[...] + y, g_ref[...], b_ref[...])
    of_ref[...] = o
    is_first = (pl.program_id(0) == 0) & (pl.program_id(1) == 0)
    _route_tile(o, is_first, wrt_ref, rbc_ref, pay_ref, dest_ref, cnt_ref, carry_ref, region)


def _xattn_route(hb, hf, wq_all, kv, wo_all, l, g, b, wrt, rbc, region, batch, seq, mem_len, tm=512):
    n, d = hf.shape
    ns = seq // tm
    xw = N_X * X_HEAD
    tile = lambda bb, s: bb * ns + s
    row = pl.BlockSpec((tm, d), lambda bb, s: (tile(bb, s), 0))
    par = pl.BlockSpec((1, d), lambda bb, s: (0, 0))
    return pl.pallas_call(
        functools.partial(_xattn_kernel, region=region), grid=(batch, ns),
        in_specs=[row, row, pl.BlockSpec((None, d, xw), lambda bb, s: (l, 0, 0)),
                  pl.BlockSpec((mem_len, xw), lambda bb, s: (bb, 0)),
                  pl.BlockSpec((mem_len, xw), lambda bb, s: (bb, 1)),
                  pl.BlockSpec((None, xw, d), lambda bb, s: (l, 0, 0)), par, par,
                  pl.BlockSpec((LANES, d), lambda bb, s: (0, 0)),
                  pl.BlockSpec((N_EXPERTS, 1), lambda bb, s: (0, 0))],
        out_specs=[row, pl.BlockSpec((tm, PAY_W), lambda bb, s: (tile(bb, s), 0)),
                   pl.BlockSpec((1, tm), lambda bb, s: (0, tile(bb, s))),
                   pl.BlockSpec((8, LANES), lambda bb, s: (0, 0))],
        out_shape=[jax.ShapeDtypeStruct((n, d), F32), jax.ShapeDtypeStruct((n, PAY_W), U32),
                   jax.ShapeDtypeStruct((1, n), jnp.int32), jax.ShapeDtypeStruct((8, LANES), jnp.int32)],
        scratch_shapes=[pltpu.VMEM((8, LANES), F32)],
        compiler_params=_cparams("arbitrary", "arbitrary"), name="xattn_route")(
            hb, hf, wq_all, kv, kv, wo_all, g.reshape(1, d), b.reshape(1, d), wrt, rbc)


N_GROUPS = 4
PER_GROUP = N_EXPERTS // N_GROUPS
MOE_TILE = 512
PAY_X = D_MODEL // 2
PAY_W = PAY_X + LANES
U32 = jnp.uint32
_HI16 = 0xFFFF0000


def _pack_bf16_pairs(x):
    u = lax.bitcast_convert_type(x.astype(BF16).astype(F32), U32)
    w = x.shape[1] // 2
    return (u[:, :w] >> 16) | (u[:, w:] & jnp.uint32(_HI16))


def _unpack_bf16_pairs(words):
    lo = lax.bitcast_convert_type(words << 16, F32)
    hi = lax.bitcast_convert_type(words & jnp.uint32(_HI16), F32)
    return jnp.concatenate([lo, hi], axis=1).astype(BF16)


def _route_tile(h, is_first, wrt_ref, rbc_ref, pay_ref, dest_ref, cnt_ref, carry_ref, region):
    tm = h.shape[0]

    @pl.when(is_first)
    def _():
        carry_ref[...] = jnp.zeros_like(carry_ref)

    w = wrt_ref[...]
    w1 = w.astype(BF16)
    w2 = (w - w1.astype(F32)).astype(BF16)
    h1 = h.astype(BF16)
    h2 = (h - h1.astype(F32)).astype(BF16)
    logits = _dotg(w1, h1, _NT) + _dotg(w2, h1, _NT) + _dotg(w1, h2, _NT)
    aff = _sigmoid(logits[0:N_EXPERTS])
    biased = aff + rbc_ref[...]
    row = lax.broadcasted_iota(jnp.int32, (N_EXPERTS, 1), 0).astype(F32)
    big = float(LANES)
    best = e1 = e2 = None
    for g in range(N_GROUPS):
        x = jnp.where((row >= g * PER_GROUP) & (row < (g + 1) * PER_GROUP), biased, -jnp.inf)
        m1 = jnp.max(x, axis=0, keepdims=True)
        i1 = jnp.min(jnp.where(x == m1, row, big), axis=0, keepdims=True)
        x2 = jnp.where(row == i1, -jnp.inf, x)
        m2 = jnp.max(x2, axis=0, keepdims=True)
        i2 = jnp.min(jnp.where(x2 == m2, row, big), axis=0, keepdims=True)
        score = m1 + m2
        if g == 0:
            best, e1, e2 = score, i1, i2
        else:
            better = score > best
            best = jnp.where(better, score, best)
            e1 = jnp.where(better, i1, e1)
            e2 = jnp.where(better, i2, e2)
    s1 = jnp.sum(jnp.where(row == e1, aff, 0.0), axis=0, keepdims=True)
    s2 = jnp.sum(jnp.where(row == e2, aff, 0.0), axis=0, keepdims=True)
    tot = s1 + s2
    gates_t = jnp.where(row == e1, s1 / tot, 0.0) + jnp.where(row == e2, s2 / tot, 0.0)

    grp = jnp.zeros_like(e1)
    for g in range(1, N_GROUPS):
        grp = grp + jnp.where(e1 >= g * PER_GROUP, 1.0, 0.0)
    row8 = lax.broadcasted_iota(jnp.int32, (8, 1), 0).astype(F32)
    onehot = jnp.where(row8 == grp, 1.0, 0.0)
    r = lax.broadcasted_iota(jnp.int32, (tm, tm), 0)
    c = lax.broadcasted_iota(jnp.int32, (tm, tm), 1)
    earlier = jnp.where(r < c, 1.0, 0.0).astype(BF16)
    rank_in = _dot(onehot.astype(BF16), earlier)
    carry = carry_ref[...]
    rank = jnp.sum(onehot * (rank_in + carry[:, 0:1]), axis=0, keepdims=True)
    dest_ref[...] = (grp * float(region) + rank).astype(jnp.int32)
    carry = carry + jnp.sum(onehot, axis=1, keepdims=True)
    carry_ref[...] = carry
    cnt_ref[...] = carry.astype(jnp.int32)

    gates = jnp.concatenate([gates_t, jnp.zeros((LANES - N_EXPERTS, tm), F32)], axis=0).T
    pay_ref[:, :PAY_X] = _pack_bf16_pairs(h)
    pay_ref[:, PAY_X:] = lax.bitcast_convert_type(gates, U32)


def _scatter_kernel(dest_ref, cnt_ref, pay_ref, out_ref, buf, zbuf, sem, zsem, *, region, nsteps):
    tm = pay_ref.shape[0]
    i = pl.program_id(0)
    slot = i % 2

    def wait_rows(s):
        pltpu.make_async_copy(buf.at[s], out_ref.at[pl.ds(0, tm)], sem.at[s]).wait()

    @pl.when(i == 0)
    def _():
        zbuf[...] = jnp.zeros_like(zbuf)
        copies = []
        for g in range(N_GROUPS):
            start = pl.multiple_of(g * region + (cnt_ref[g] // MOE_TILE) * MOE_TILE, MOE_TILE)
            copies.append(pltpu.make_async_copy(zbuf, out_ref.at[pl.ds(start, MOE_TILE)], zsem))
        for cp in copies:
            cp.start()
        for cp in copies:
            cp.wait()

    @pl.when(i >= 2)
    def _():
        wait_rows(slot)

    buf[slot] = pay_ref[...]

    base = i * tm
    for r in range(tm):
        d = dest_ref[base + r]
        pltpu.make_async_copy(buf.at[slot, pl.ds(r, 1)], out_ref.at[pl.ds(d, 1)],
                              sem.at[slot]).start(priority=r % 2)

    @pl.when(i == nsteps - 1)
    def _():
        wait_rows(slot)
        if nsteps >= 2:
            wait_rows(1 - slot)


def _scatter(dest, cnt, pay, region, tm=256):
    n = pay.shape[0]
    nsteps = n // tm
    return pl.pallas_call(
        functools.partial(_scatter_kernel, region=region, nsteps=nsteps),
        grid_spec=pltpu.PrefetchScalarGridSpec(
            num_scalar_prefetch=2, grid=(nsteps,),
            in_specs=[pl.BlockSpec((tm, PAY_W), lambda i, d, c: (i, 0))],
            out_specs=pl.BlockSpec(memory_space=pl.ANY),
            scratch_shapes=[pltpu.VMEM((2, tm, PAY_W), U32), pltpu.VMEM((MOE_TILE, PAY_W), U32),
                            pltpu.SemaphoreType.DMA((2,)), pltpu.SemaphoreType.DMA(())]),
        out_shape=jax.ShapeDtypeStruct((N_GROUPS * region, PAY_W), U32),
        compiler_params=_cparams("arbitrary"), name="moe_scatter")(dest, cnt, pay)


def _tile_tables(cnt, region, ntiles):
    nt = (cnt + MOE_TILE - 1) // MOE_TILE
    ends = jnp.cumsum(nt)
    starts = ends - nt
    total = ends[-1]
    i = jnp.minimum(jnp.arange(ntiles, dtype=jnp.int32), total - 1)
    g = jnp.sum((i[:, None] >= ends[None, :]).astype(jnp.int32), axis=1)
    blk = g * (region // MOE_TILE) + i - starts[g]
    return g.astype(jnp.int32), blk.astype(jnp.int32), total.reshape(1).astype(jnp.int32)


def _moe_up_kernel(tg_ref, tb_ref, nt_ref, x_ref, wg_ref, wu_ref, o_ref, wgs, wus):
    e = pl.program_id(0)
    i = pl.program_id(1)
    grp = tg_ref[i]
    changed = (i == 0) | (grp != tg_ref[jnp.maximum(i - 1, 0)])

    @pl.when(changed)
    def _():
        wgs[...] = wg_ref[...].astype(BF16)
        wus[...] = wu_ref[...].astype(BF16)

    @pl.when(i < nt_ref[0])
    def _():
        x = _unpack_bf16_pairs(x_ref[:, :PAY_X])
        gates = lax.bitcast_convert_type(x_ref[:, PAY_X:], F32)
        lane = lax.broadcasted_iota(jnp.int32, (1, LANES), 1)
        ge = jnp.sum(jnp.where(lane == grp * PER_GROUP + e, gates, 0.0), axis=-1, keepdims=True)
        a = _dot(x, wgs[...])
        u = _dot(x, wus[...])
        o_ref[...] = (a * _sigmoid(a) * u * ge).astype(BF16)


def _moe_up(tg, tb, nt, xs, wg, wu, l):
    rows = xs.shape[0]
    ntiles = tg.shape[0]
    d, de = wg.shape[2], wg.shape[3]
    wspec = pl.BlockSpec((None, None, d, de), lambda e, i, tg, tb, nt: (l, tg[i] * PER_GROUP + e, 0, 0))
    return pl.pallas_call(
        _moe_up_kernel,
        grid_spec=pltpu.PrefetchScalarGridSpec(
            num_scalar_prefetch=3, grid=(PER_GROUP, ntiles),
            in_specs=[pl.BlockSpec((MOE_TILE, PAY_W), lambda e, i, tg, tb, nt: (tb[i], 0)), wspec, wspec],
            out_specs=pl.BlockSpec((MOE_TILE, de), lambda e, i, tg, tb, nt: (tb[i], e)),
            scratch_shapes=[pltpu.VMEM((d, de), BF16), pltpu.VMEM((d, de), BF16)]),
        out_shape=jax.ShapeDtypeStruct((rows, PER_GROUP * de), BF16),
        compiler_params=_cparams("arbitrary", "arbitrary"), name="moe_up")(tg, tb, nt, xs, wg, wu)


def _moe_down_kernel(tg_ref, tb_ref, nt_ref, x_ref, w_ref, o_ref, ws):
    i = pl.program_id(1)
    changed = (i == 0) | (tg_ref[i] != tg_ref[jnp.maximum(i - 1, 0)])

    @pl.when(changed)
    def _():
        ws[...] = w_ref[...].astype(BF16)

    @pl.when(i < nt_ref[0])
    def _():
        o_ref[...] = _dot(x_ref[...], ws[...])


def _moe_down(tg, tb, nt, hid, wd, l, tn=1024):
    rows, k = hid.shape
    ntiles = tg.shape[0]
    d = wd.shape[3]
    return pl.pallas_call(
        _moe_down_kernel,
        grid_spec=pltpu.PrefetchScalarGridSpec(
            num_scalar_prefetch=3, grid=(d // tn, ntiles),
            in_specs=[pl.BlockSpec((MOE_TILE, k), lambda c, i, tg, tb, nt: (tb[i], 0)),
                      pl.BlockSpec((None, None, k, tn), lambda c, i, tg, tb, nt: (l, tg[i], 0, c))],
            out_specs=pl.BlockSpec((MOE_TILE, tn), lambda c, i, tg, tb, nt: (tb[i], c)),
            scratch_shapes=[pltpu.VMEM((k, tn), BF16)]),
        out_shape=jax.ShapeDtypeStruct((rows, d), F32),
        compiler_params=_cparams("arbitrary", "arbitrary"), name="moe_down")(tg, tb, nt, hid, wd)


def _gather_ln_kernel(dest_ref, y_ref, h_ref, g_ref, b_ref, of_ref, ob_ref, buf, sem, *, nsteps):
    tm = h_ref.shape[0]
    i = pl.program_id(0)

    def issue(step, slot):
        base = step * tm
        for r in range(tm):
            d = dest_ref[base + r]
            pltpu.make_async_copy(y_ref.at[pl.ds(d, 1)], buf.at[slot, pl.ds(r, 1)],
                                  sem.at[slot]).start(priority=r % 2)

    @pl.when(i == 0)
    def _():
        issue(0, 0)

    @pl.when(i + 1 < nsteps)
    def _():
        issue(i + 1, (i + 1) % 2)

    slot = i % 2
    pltpu.make_async_copy(y_ref.at[pl.ds(0, tm)], buf.at[slot], sem.at[slot]).wait()
    o = _layer_norm(ALPHA * h_ref[...] + buf[slot], g_ref[...], b_ref[...])
    of_ref[...] = o
    ob_ref[...] = o.astype(BF16)


def _gather_ln(dest, ys, hf, g, b, tm=256):
    n, d = hf.shape
    nsteps = n // tm
    row = pl.BlockSpec((tm, d), lambda i, dref: (i, 0))
    par = pl.BlockSpec((1, d), lambda i, dref: (0, 0))
    return pl.pallas_call(
        functools.partial(_gather_ln_kernel, nsteps=nsteps),
        grid_spec=pltpu.PrefetchScalarGridSpec(
            num_scalar_prefetch=1, grid=(nsteps,),
            in_specs=[pl.BlockSpec(memory_space=pl.ANY), row, par, par],
            out_specs=[row, row],
            scratch_shapes=[pltpu.VMEM((2, tm, d), F32), pltpu.SemaphoreType.DMA((2,))]),
        out_shape=[jax.ShapeDtypeStruct((n, d), F32), jax.ShapeDtypeStruct((n, d), BF16)],
        compiler_params=_cparams("arbitrary"), name="moe_gather_ln3")(
            dest, ys, hf, g.reshape(1, d), b.reshape(1, d))


def _moe_region(n):
    return n + MOE_TILE


def _moe(hf, pay, dest, cnt, p, w, l):
    n = hf.shape[0]
    region = _moe_region(n)
    ntiles = n // MOE_TILE + N_GROUPS
    dest = dest.reshape(n)
    cnt = cnt[:N_GROUPS, 0]
    tg, tb, nt = _tile_tables(cnt, region, ntiles)
    xs = _scatter(dest, cnt, pay, region)
    hid = _moe_up(tg, tb, nt, xs, p["moe_w_gate"], p["moe_w_up"], l)
    ys = _moe_down(tg, tb, nt, hid, w["moe_wd"], l)
    return _gather_ln(dest, ys, hf, p["ln3_g"][l], p["ln3_b"][l])


def _prep_params(p):
    L = p["w_in"].shape[0]
    half = MLA_ROPE // 2
    pad_r = LANES - MLA_ROPE

    wq = p["mla_w_uq"].reshape(L, MLA_Q_RANK, N_MLA, MLA_NOPE + MLA_ROPE)
    rq = wq[..., MLA_NOPE:]
    zq = jnp.zeros((L, MLA_Q_RANK, N_MLA, pad_r), F32)
    wuq = jnp.concatenate([wq[..., :MLA_NOPE], rq, zq, rq[..., half:], rq[..., :half], zq], axis=-1)
    wkv = p["mla_w_ukv"].reshape(L, MLA_KV_RANK, N_MLA, MLA_NOPE + MLA_V)

    bf = lambda a: a.astype(BF16)
    ne = p["moe_w_down"].shape[1]
    return dict(
        wt_in=jnp.swapaxes(p["w_in"], 1, 2),
        wuq=bf(wuq.reshape(L, MLA_Q_RANK, N_MLA * 3 * LANES)),
        wuk=bf(wkv[..., :MLA_NOPE].reshape(L, MLA_KV_RANK, N_MLA * MLA_NOPE)),
        wuv=bf(wkv[..., MLA_NOPE:].reshape(L, MLA_KV_RANK, N_MLA * MLA_V)),
        gla_wg=jnp.pad(p["gla_w_gate"], ((0, 0), (0, LANES - GLA_RANK), (0, 0))),
        w_branch=bf(p["w_branch"]), w_out=bf(p["w_out"]),
        x_w_q=bf(p["x_w_q"]), x_w_kv=bf(p["x_w_kv"]), x_w_o=bf(p["x_w_o"]),
        w_router_t=jnp.pad(p["w_router"].T, ((0, LANES - N_EXPERTS), (0, 0))),
        router_bias_c=p["router_bias"].reshape(N_EXPERTS, 1),
        moe_wd=p["moe_w_down"].reshape(L, N_GROUPS, (ne // N_GROUPS) * D_EXPERT, D_MODEL),
    )


def _mixer(hf, hb, cos_p, sin_p, p, w, l, batch, seq):
    wt = w["wt_in"]
    o_mla = 2 * D_SG
    o_gla = o_mla + MLA_Q_RANK + MLA_KV_RANK + MLA_ROPE
    o_ml = o_gla + 2 * N_GLA * GLA_DK + 2 * N_GLA * GLA_DV + GLA_RANK
    o_gate = o_ml + 2 * N_ML * ML_DK + 2 * N_ML * ML_DV + 2 * N_ML
    z_sg = _mm_wt(hb, wt, l, 0, 1, 2 * D_SG, F32, name="mm_sg")
    z_mla = _mm_wt(hb, wt, l, o_mla, 1, MLA_Q_RANK + MLA_KV_RANK + LANES, F32, name="mm_mla")
    gates = _mm_wt(hb, wt, l, o_gate, 4 * D_MODEL // 1024, 1024, BF16, act="sigmoid", name="mm_gate")

    y_a = _sg(z_sg, p["sg_vnorm_g"][l], p["sg_vnorm_b"][l], p["sg_w_s"][l], p["sg_b_s"][l])
    q, k, v = _mla_prep(z_mla, cos_p, sin_p, p["mla_qnorm_g"][l], p["mla_kvnorm_g"][l],
                        w["wuq"][l], w["wuk"][l], w["wuv"][l])
    y_b = _mla_attn(q, k, v, batch, seq)
    y_c = _gla(hb, wt, l, o_gla, w["gla_wg"][l], p["gla_b_gate"][l], p["gla_norm_g"][l], batch, seq)
    y_d = _mlstm(hb, wt, l, o_ml, p["ml_conv_w"][l], p["ml_conv_b"][l], p["ml_gate_b"][l], p["ml_norm_g"][l],
                 batch, seq)
    return _merge_out(y_a, y_b, y_c, y_d, gates, w["w_branch"], w["w_out"], l, hf,
                      p["ln1_g"][l], p["ln1_b"][l])


def _forward(p):
    x = p["x"]
    batch, seq, d = x.shape
    n = batch * seq
    mem = p["mem"]
    mem_len = mem.shape[1]
    w = _prep_params(p)
    posb = jnp.broadcast_to(p["positions"].reshape(n, 1).astype(F32), (n, LANES))
    cos_p, sin_p = _rope_tables(posb)
    memb = mem.reshape(batch * mem_len, d).astype(BF16)
    hf, hb = _ln(x.reshape(n, d), p["ln_in_g"], p["ln_in_b"])
    for l in range(p["w_in"].shape[0]):
        hf, hb = _mixer(hf, hb, cos_p, sin_p, p, w, l, batch, seq)
        kv = _mm(memb, w["x_w_kv"][l], BF16, tm=512, tn=1024, name="mm_xkv")
        hf, pay, dest, cnt = _xattn_route(hb, hf, w["x_w_q"], kv, w["x_w_o"], l, p["ln2_g"][l], p["ln2_b"][l],
                                          w["w_router_t"], w["router_bias_c"], _moe_region(n),
                                          batch, seq, mem_len)
        hf, hb = _moe(hf, pay, dest, cnt, p, w, l)
    return hf.reshape(batch, seq, d)


def kernel(x, mem, positions, ln_in_g, ln_in_b, w_in, sg_vnorm_g, sg_vnorm_b, sg_w_s, sg_b_s, mla_qnorm_g, mla_kvnorm_g, mla_w_uq, mla_w_ukv, gla_w_gate, gla_b_gate, gla_norm_g, ml_conv_w, ml_conv_b, ml_gate_b, ml_norm_g, w_branch, w_out, ln1_g, ln1_b, x_w_q, x_w_kv, x_w_o, ln2_g, ln2_b, w_router, router_bias, moe_w_gate, moe_w_up, moe_w_down, ln3_g, ln3_b):
    return _forward(dict(
        x=x, mem=mem, positions=positions, ln_in_g=ln_in_g, ln_in_b=ln_in_b, w_in=w_in,
        sg_vnorm_g=sg_vnorm_g, sg_vnorm_b=sg_vnorm_b, sg_w_s=sg_w_s, sg_b_s=sg_b_s,
        mla_qnorm_g=mla_qnorm_g, mla_kvnorm_g=mla_kvnorm_g, mla_w_uq=mla_w_uq, mla_w_ukv=mla_w_ukv,
        gla_w_gate=gla_w_gate, gla_b_gate=gla_b_gate, gla_norm_g=gla_norm_g,
        ml_conv_w=ml_conv_w, ml_conv_b=ml_conv_b, ml_gate_b=ml_gate_b, ml_norm_g=ml_norm_g,
        w_branch=w_branch, w_out=w_out, ln1_g=ln1_g, ln1_b=ln1_b,
        x_w_q=x_w_q, x_w_kv=x_w_kv, x_w_o=x_w_o, ln2_g=ln2_g, ln2_b=ln2_b,
        w_router=w_router, router_bias=router_bias,
        moe_w_gate=moe_w_gate, moe_w_up=moe_w_up, moe_w_down=moe_w_down, ln3_g=ln3_g, ln3_b=ln3_b))
```

```python
import functools
import math

import jax
import jax.numpy as jnp
from jax import lax
from jax.experimental import pallas as pl
from jax.experimental.pallas import tpu as pltpu

F32 = jnp.float32
BF16 = jnp.bfloat16

D_MODEL = 2048
DEPTH = 4
EPS = 1e-5
ALPHA = (2.0 * DEPTH) ** 0.25

SG_CHUNK = 128
N_SG = 4
D_SG = 512
N_MLA = 4
MLA_Q_RANK = 384
MLA_KV_RANK = 256
MLA_NOPE = 128
MLA_ROPE = 64
MLA_V = 128
MLA_QK = 256
ROPE_BASE = 10000.0
N_GLA = 4
GLA_DK = 64
GLA_DV = 128
GLA_RANK = 16
GLA_TAU = 16.0
GLA_CHUNK = 64
N_ML = 4
ML_DK = 64
ML_DV = 128
ML_CHUNK = 128
N_X = 4
X_HEAD = 128
N_EXPERTS = 16
D_EXPERT = 512

LANES = 128
VMEM_LIMIT = 48 * 1024 * 1024
VMEM_LIMIT_BIG = 56 * 1024 * 1024

_NT = (((1,), (1,)), ((), ()))
_TN = (((0,), (0,)), ((), ()))


def _cparams(*sem, vmem=VMEM_LIMIT):
    return pltpu.CompilerParams(dimension_semantics=sem, vmem_limit_bytes=vmem)


def _dot(a, b):
    return jnp.dot(a, b, preferred_element_type=F32)


def _dotg(a, b, dims):
    return lax.dot_general(a, b, dims, preferred_element_type=F32)


def _split3(a):
    a1 = a.astype(BF16)
    r1 = a - a1.astype(F32)
    a2 = r1.astype(BF16)
    a3 = (r1 - a2.astype(F32)).astype(BF16)
    return a1, a2, a3


def _dot_exact_rhs(a, ones_bf16):
    a1, a2, a3 = _split3(a)
    return _dot(a1, ones_bf16) + _dot(a2, ones_bf16) + _dot(a3, ones_bf16)


def _dot_exact_lhs(ones_bf16, a):
    a1, a2, a3 = _split3(a)
    return _dot(ones_bf16, a1) + _dot(ones_bf16, a2) + _dot(ones_bf16, a3)


def _dot_hi(a, b):
    a1 = a.astype(BF16)
    a2 = (a - a1.astype(F32)).astype(BF16)
    b1 = b.astype(BF16)
    b2 = (b - b1.astype(F32)).astype(BF16)
    return _dot(a1, b1) + _dot(a2, b1) + _dot(a1, b2)


def _sigmoid(x):
    return 1.0 / (1.0 + jnp.exp(-x))


def _log_sigmoid(x):
    return jnp.minimum(x, 0.0) - jnp.log(1.0 + jnp.exp(-jnp.abs(x)))


def _layer_norm(t, g, b):
    mu = jnp.mean(t, axis=-1, keepdims=True)
    c = t - mu
    var = jnp.mean(c * c, axis=-1, keepdims=True)
    return c * lax.rsqrt(var + EPS) * g + b


def _rms_norm(t, g):
    return t * lax.rsqrt(jnp.mean(t * t, axis=-1, keepdims=True) + EPS) * g


def _ln_kernel(x_ref, g_ref, b_ref, of_ref, ob_ref):
    y = _layer_norm(x_ref[...], g_ref[...], b_ref[...])
    of_ref[...] = y
    ob_ref[...] = y.astype(BF16)


def _ln(x, g, b, tm=256):
    n, d = x.shape
    row = pl.BlockSpec((tm, d), lambda i: (i, 0))
    par = pl.BlockSpec((1, d), lambda i: (0, 0))
    return pl.pallas_call(
        _ln_kernel, grid=(n // tm,), in_specs=[row, par, par], out_specs=[row, row],
        out_shape=[jax.ShapeDtypeStruct((n, d), F32), jax.ShapeDtypeStruct((n, d), BF16)],
        compiler_params=_cparams("parallel"), name="ln_in")(x, g.reshape(1, d), b.reshape(1, d))


def _mm_kernel(x_ref, w_ref, o_ref, *, act):
    acc = _dot(x_ref[...], w_ref[...])
    if act == "sigmoid":
        acc = _sigmoid(acc)
    o_ref[...] = acc.astype(o_ref.dtype)


def _mm(x, w, out_dtype, act=None, tm=512, tn=None, name="mm"):
    n, k = x.shape
    m = w.shape[1]
    tn = m if tn is None else tn
    tm = min(tm, n)
    assert n % tm == 0 and m % tn == 0
    return pl.pallas_call(
        functools.partial(_mm_kernel, act=act), grid=(n // tm, m // tn),
        in_specs=[pl.BlockSpec((tm, k), lambda i, j: (i, 0)),
                  pl.BlockSpec((k, tn), lambda i, j: (0, j))],
        out_specs=pl.BlockSpec((tm, tn), lambda i, j: (i, j)),
        out_shape=jax.ShapeDtypeStruct((n, m), out_dtype),
        compiler_params=_cparams("parallel", "parallel"), name=name)(x, w)


def _mm_wt_kernel(x_ref, w_ref, o_ref, ws, *, act):
    @pl.when(pl.program_id(1) == 0)
    def _():
        ws[...] = w_ref[0].astype(BF16)

    acc = _dotg(x_ref[...], ws[...], _NT)
    if act == "sigmoid":
        acc = _sigmoid(acc)
    o_ref[...] = acc.astype(o_ref.dtype)


def _mm_wt(x, wt_all, l, row0, nblk, tn, out_dtype, act=None, tm=1024, name="mm_wt"):
    n, k = x.shape
    assert row0 % 8 == 0
    wmode = dict(pipeline_mode=pl.Buffered(1)) if nblk == 1 else {}
    return pl.pallas_call(
        functools.partial(_mm_wt_kernel, act=act), grid=(nblk, n // tm),
        in_specs=[pl.BlockSpec((tm, k), lambda j, i: (i, 0)),
                  pl.BlockSpec((pl.Element(1), pl.Element(tn), pl.Element(k)),
                               lambda j, i: (l, pl.multiple_of(row0 + j * tn, 8), 0), **wmode)],
        out_specs=pl.BlockSpec((tm, tn), lambda j, i: (i, j)),
        out_shape=jax.ShapeDtypeStruct((n, nblk * tn), out_dtype),
        scratch_shapes=[pltpu.VMEM((tn, k), BF16)],
        compiler_params=_cparams("arbitrary", "arbitrary"), name=name)(x, wt_all)


def _project_tile(x_ref, w_ref, wbf):
    @pl.when(pl.program_id(0) == 0)
    def _():
        wbf[...] = w_ref[0].astype(BF16)

    return _dotg(x_ref[...], wbf[...], _NT)


def _proj_specs(d, l, row0, width):
    wspec = pl.BlockSpec((pl.Element(1), pl.Element(width), pl.Element(d)), lambda i: (l, row0, 0),
                         pipeline_mode=pl.Buffered(1))
    return wspec, pltpu.VMEM((width, d), BF16)


def _sg_kernel(x_ref, w_ref, vg_ref, vb_ref, ws_ref, bst_ref, o_ref, wbf, *, nchunk):
    z = _project_tile(x_ref, w_ref, wbf)
    z = 0.5 * z * (1.0 + jnp.tanh(math.sqrt(2.0 / math.pi) * (z + 0.044715 * (z * z * z))))
    u = z[:, :D_SG]
    vn = _layer_norm(z[:, D_SG:], vg_ref[...], vb_ref[...]).astype(BF16)
    r = lax.broadcasted_iota(jnp.int32, (SG_CHUNK, SG_CHUNK), 0)
    c = lax.broadcasted_iota(jnp.int32, (SG_CHUNK, SG_CHUNK), 1)
    causal = c <= r
    gw = SG_CHUNK
    for g in range(N_SG):
        w = jnp.where(causal, ws_ref[g], 0.0).astype(BF16)
        bias = bst_ref[:, g:g + 1]
        for ci in range(nchunk):
            rs = slice(ci * SG_CHUNK, (ci + 1) * SG_CHUNK)
            cs = slice(g * gw, (g + 1) * gw)
            mixed = _dot(w, vn[rs, cs]) + bias
            o_ref[rs, cs] = (u[rs, cs] * mixed).astype(BF16)


def _sg(hb, wt, l, row0, vg, vb, ws, bs, tm=512):
    n, d = hb.shape
    wspec, wscratch = _proj_specs(d, l, row0, 2 * D_SG)
    return pl.pallas_call(
        functools.partial(_sg_kernel, nchunk=tm // SG_CHUNK), grid=(n // tm,),
        in_specs=[pl.BlockSpec((tm, d), lambda i: (i, 0)), wspec,
                  pl.BlockSpec((1, D_SG), lambda i: (0, 0)),
                  pl.BlockSpec((1, D_SG), lambda i: (0, 0)),
                  pl.BlockSpec((N_SG, SG_CHUNK, SG_CHUNK), lambda i: (0, 0, 0)),
                  pl.BlockSpec((SG_CHUNK, N_SG), lambda i: (0, 0))],
        out_specs=pl.BlockSpec((tm, D_SG), lambda i: (i, 0)),
        out_shape=jax.ShapeDtypeStruct((n, D_SG), BF16),
        scratch_shapes=[wscratch],
        compiler_params=_cparams("arbitrary"), name="sg")(
            hb, wt, vg.reshape(1, D_SG), vb.reshape(1, D_SG), ws, bs.T)


def _rope_kernel(pos_ref, cos_ref, sin_ref):
    half = MLA_ROPE // 2
    lane = lax.broadcasted_iota(jnp.int32, (1, LANES), 1)
    idx = jnp.bitwise_and(lane, half - 1).astype(F32)
    freq = jnp.exp(idx * (-math.log(ROPE_BASE) / half))
    ang = pos_ref[...] * freq
    c = jnp.cos(ang)
    s = jnp.sin(ang)
    cos_ref[...] = jnp.where(lane < MLA_ROPE, c, 0.0)
    sin_ref[...] = jnp.where(lane < half, -s, jnp.where(lane < MLA_ROPE, s, 0.0))


def _rope_tables(posb, tm=512):
    n = posb.shape[0]
    row = pl.BlockSpec((tm, LANES), lambda i: (i, 0))
    return pl.pallas_call(
        _rope_kernel, grid=(n // tm,), in_specs=[row], out_specs=[row, row],
        out_shape=[jax.ShapeDtypeStruct((n, LANES), F32)] * 2,
        compiler_params=_cparams("parallel"), name="rope_tables")(posb)


def _mla_prep_kernel(x_ref, w_ref, cos_ref, sin_ref, qg_ref, kvg_ref, wuq_ref, wuk_ref, wuv_ref,
                     q_ref, k_ref, v_ref, wbf):
    z = _project_tile(x_ref, w_ref, wbf)
    cq =_rms_norm(z[:, :MLA_Q_RANK], qg_ref[...]).astype(BF16)
    o1 = MLA_Q_RANK + MLA_KV_RANK
    ckv = _rms_norm(z[:, MLA_Q_RANK:o1], kvg_ref[...]).astype(BF16)
    cos_p = cos_ref[...]
    sin_p = sin_ref[...]
    kr = z[:, o1:o1 + LANES]
    half = MLA_ROPE // 2
    lane = lax.broadcasted_iota(jnp.int32, (1, LANES), 1)
    kr_swapped = jnp.where(lane < half, pltpu.roll(kr, LANES - half, 1), pltpu.roll(kr, half, 1))
    k_tail = (kr * cos_p + kr_swapped * sin_p).astype(BF16)
    qa = _dot(cq, wuq_ref[...])
    kn = _dot(ckv, wuk_ref[...])
    v_ref[...] = _dot(ckv, wuv_ref[...]).astype(BF16)
    scale = (MLA_NOPE + MLA_ROPE) ** -0.5
    for h in range(N_MLA):
        b0 = h * 3 * LANES
        q_tail = qa[:, b0 + LANES:b0 + 2 * LANES] * cos_p + qa[:, b0 + 2 * LANES:b0 + 3 * LANES] * sin_p
        q_ref[:, h * MLA_QK:h * MLA_QK + LANES] = (qa[:, b0:b0 + LANES] * scale).astype(BF16)
        q_ref[:, h * MLA_QK + LANES:(h + 1) * MLA_QK] = (q_tail * scale).astype(BF16)
        k_ref[:, h * MLA_QK:h * MLA_QK + LANES] = kn[:, h * LANES:(h + 1) * LANES].astype(BF16)
        k_ref[:, h * MLA_QK + LANES:(h + 1) * MLA_QK] = k_tail


def _mla_prep(hb, wt, l, row0, cos_p, sin_p, qg, kvg, wuq, wuk, wuv, tm=512):
    n, d = hb.shape
    row = lambda w: pl.BlockSpec((tm, w), lambda i: (i, 0))
    full = lambda a: pl.BlockSpec(a.shape, lambda i: (0,) * a.ndim)
    qg = qg.reshape(1, -1)
    kvg = kvg.reshape(1, -1)
    wspec, wscratch = _proj_specs(d, l, row0, MLA_Q_RANK + MLA_KV_RANK + LANES)
    return pl.pallas_call(
        _mla_prep_kernel, grid=(n // tm,),
        in_specs=[row(d), wspec, row(LANES), row(LANES), full(qg), full(kvg), full(wuq), full(wuk), full(wuv)],
        out_specs=[row(N_MLA * MLA_QK), row(N_MLA * MLA_QK), row(N_MLA * MLA_V)],
        out_shape=[jax.ShapeDtypeStruct((n, N_MLA * MLA_QK), BF16),
                   jax.ShapeDtypeStruct((n, N_MLA * MLA_QK), BF16),
                   jax.ShapeDtypeStruct((n, N_MLA * MLA_V), BF16)],
        scratch_shapes=[wscratch],
        compiler_params=_cparams("arbitrary"), name="mla_prep")(
            hb, wt, cos_p, sin_p, qg, kvg, wuq, wuk, wuv)


def _mla_attn_kernel(q_ref, k_ref, v_ref, o_ref, *, t, nblk):
    krow = lax.broadcasted_iota(jnp.int32, (t, t), 0)
    qcol = lax.broadcasted_iota(jnp.int32, (t, t), 1)
    visible = krow <= qcol
    for i in range(nblk):
        q = q_ref[i * t:(i + 1) * t, :]
        m = jnp.full((1, t), -jnp.inf, F32)
        l = jnp.zeros((1, t), F32)
        acc = jnp.zeros((MLA_V, t), F32)
        for j in range(i + 1):
            ks = slice(j * t, (j + 1) * t)
            s = _dotg(k_ref[ks, :], q, _NT)
            if j == i:
                s = jnp.where(visible, s, -jnp.inf)
            m_new = jnp.maximum(m, jnp.max(s, axis=0, keepdims=True))
            p = jnp.exp(s - m_new)
            a = jnp.exp(m - m_new)
            l = a * l + jnp.sum(p, axis=0, keepdims=True)
            acc = a * acc + _dotg(v_ref[ks, :], p.astype(BF16), _TN)
            m = m_new
        o_ref[i * t:(i + 1) * t, :] = (acc / l).T.astype(BF16)


def _mla_attn(q, k, v, batch, seq, t=512):
    n = q.shape[0]
    return pl.pallas_call(
        functools.partial(_mla_attn_kernel, t=t, nblk=seq // t), grid=(batch, N_MLA),
        in_specs=[pl.BlockSpec((seq, MLA_QK), lambda b, h: (b, h)),
                  pl.BlockSpec((seq, MLA_QK), lambda b, h: (b, h)),
                  pl.BlockSpec((seq, MLA_V), lambda b, h: (b, h))],
        out_specs=pl.BlockSpec((seq, MLA_V), lambda b, h: (b, h)),
        out_shape=jax.ShapeDtypeStruct((n, N_MLA * MLA_V), BF16),
        compiler_params=_cparams("parallel", "parallel"), name="mla_attn")(q, k, v)


def _gla_tile(z, o_ref, row0, st, wg_ref, bg_ref, ng_ref, nchunk):
    L = GLA_CHUNK
    qkw = N_GLA * GLA_DK
    vw = N_GLA * GLA_DV
    c_k, c_v, c_o, c_lr = qkw, 2 * qkw, 2 * qkw + vw, 2 * qkw + 2 * vw
    logits = _dot_hi(z[:, c_lr:c_lr + LANES], wg_ref[...]) + bg_ref[...]
    log_a = _log_sigmoid(logits) * (1.0 / GLA_TAU)
    lane = lax.broadcasted_iota(jnp.int32, (1, qkw), 1)
    masks = [((lane >= h * GLA_DK) & (lane < (h + 1) * GLA_DK)).astype(F32) for h in range(N_GLA)]
    r = lax.broadcasted_iota(jnp.int32, (L, L), 0)
    c = lax.broadcasted_iota(jnp.int32, (L, L), 1)
    causal = c <= r
    tril = jnp.where(causal, 1.0, 0.0).astype(BF16)
    ng = ng_ref[...]
    for ci in range(nchunk):
        rs = slice(ci * L, (ci + 1) * L)
        ro = slice(row0 + ci * L, row0 + (ci + 1) * L)
        b = _dot_exact_lhs(tril, log_a[rs])
        b_last = b[L - 1:L, :]
        q = z[rs, 0:qkw] * (GLA_DK ** -0.5)
        k = z[rs, c_k:c_k + qkw]
        qt = q * jnp.exp(b)
        kt = (k * jnp.exp(-b)).astype(BF16)
        kd = (k * jnp.exp(b_last - b)).astype(BF16)
        qstack = jnp.concatenate([qt * masks[h] for h in range(N_GLA)], axis=0).astype(BF16)
        att = _dotg(qstack, kt, _NT)
        inter = _dotg(qstack, st.astype(BF16), _NT)
        vb = z[rs, c_v:c_v + vw].astype(BF16)
        for h in range(N_GLA):
            hs = slice(h * L, (h + 1) * L)
            vs = slice(h * GLA_DV, (h + 1) * GLA_DV)
            a_h = jnp.where(causal, att[hs], 0.0).astype(BF16)
            o_h = _rms_norm(_dot(a_h, vb[:, vs]) + inter[hs], ng)
            g = z[rs, c_o + h * GLA_DV:c_o + (h + 1) * GLA_DV]
            o_ref[ro, vs] = (o_h * (g * _sigmoid(g))).astype(BF16)
        upd = _dotg(vb, kd, _TN)
        new = st * jnp.exp(b_last)
        for h in range(N_GLA):
            new = new + upd[h * GLA_DV:(h + 1) * GLA_DV] * masks[h]
        st = new
    return st


def _gla_kernel(x_ref, xn_ref, w_ref, wg_ref, bg_ref, ng_ref, o_ref, ws, za, zb, st_ref, *, tm):
    first = pl.program_id(1) == 0

    @pl.when((pl.program_id(0) == 0) & first)
    def _():
        ws[...] = w_ref[0].astype(BF16)

    @pl.when(first)
    def _():
        st_ref[...] = jnp.zeros_like(st_ref)
        za[...] = _dotg(x_ref[0:tm, :], ws[...], _NT)

    nchunk = tm // GLA_CHUNK
    zb[...] = _dotg(x_ref[tm:2 * tm, :], ws[...], _NT)
    st = _gla_tile(za, o_ref, 0, st_ref[...], wg_ref, bg_ref, ng_ref, nchunk)
    za[...] = _dotg(xn_ref[...], ws[...], _NT)
    st_ref[...] = _gla_tile(zb, o_ref, tm, st, wg_ref, bg_ref, ng_ref, nchunk)


def _gla(hb, wt, l, row0, wg, bg, ng, batch, seq, tm=256):
    n, d = hb.shape
    ns2 = seq // (2 * tm)
    qkw = N_GLA * GLA_DK
    vw = N_GLA * GLA_DV
    zw = 2 * qkw + 2 * vw + LANES
    full = lambda a: pl.BlockSpec(a.shape, lambda b, s: (0,) * a.ndim)
    bg = bg.reshape(1, qkw)
    ng = ng.reshape(1, GLA_DV)
    last = 2 * ns2 - 1
    return pl.pallas_call(
        functools.partial(_gla_kernel, tm=tm), grid=(batch, ns2),
        in_specs=[pl.BlockSpec((2 * tm, d), lambda b, s: (b * ns2 + s, 0)),
                  pl.BlockSpec((tm, d), lambda b, s: (b * 2 * ns2 + jnp.minimum(2 * s + 2, last), 0)),
                  pl.BlockSpec((pl.Element(1), pl.Element(zw), pl.Element(d)), lambda b, s: (l, row0, 0),
                               pipeline_mode=pl.Buffered(1)),
                  full(wg), full(bg), full(ng)],
        out_specs=pl.BlockSpec((2 * tm, vw), lambda b, s: (b * ns2 + s, 0)),
        out_shape=jax.ShapeDtypeStruct((n, vw), BF16),
        scratch_shapes=[pltpu.VMEM((zw, d), BF16), pltpu.VMEM((tm, zw), F32), pltpu.VMEM((tm, zw), F32),
                        pltpu.VMEM((GLA_DV, qkw), F32)],
        compiler_params=_cparams("arbitrary", "arbitrary"), name="gla")(hb, hb, wt, wg, bg, ng)


def _mlstm_tile(z, o_ref, row0, carry, cw_ref, cb_ref, gb_ref, ng_ref, nchunk):
    L = ML_CHUNK
    qkw = N_ML * ML_DK
    vw = N_ML * ML_DV
    tm = nchunk * L
    c_v, c_o, c_if = 2 * qkw, 2 * qkw + vw, 2 * qkw + 2 * vw
    ct, nrow, mwide, mk, tail = carry

    x = z[:, 0:2 * qkw]
    row8 = lax.broadcasted_iota(jnp.int32, (8, 2 * qkw), 0)
    acc = x * cw_ref[3:4, :] + cb_ref[...]
    for j in range(1, 4):
        rx = pltpu.roll(x, j, 0)
        fix = jnp.where(row8 < j, pltpu.roll(tail, j, 0), rx[0:8])
        acc = acc + jnp.concatenate([fix, rx[8:]], axis=0) * cw_ref[3 - j:4 - j, :]
    tail = x[tm - 8:tm]
    y = acc * _sigmoid(acc)
    q = y[:, :qkw] * (ML_DK ** -0.5)
    k = y[:, qkw:]

    gates = z[:, c_if:c_if + LANES] + gb_ref[...]
    fc = _log_sigmoid(gates)
    gt = gates.T[0:2 * N_ML]
    fct = _log_sigmoid(gt)

    lane = lax.broadcasted_iota(jnp.int32, (1, qkw), 1)
    masks = [((lane >= h * ML_DK) & (lane < (h + 1) * ML_DK)).astype(F32) for h in range(N_ML)]
    r = lax.broadcasted_iota(jnp.int32, (L, L), 0)
    c = lax.broadcasted_iota(jnp.int32, (L, L), 1)
    causal = c <= r
    tril = jnp.where(causal, 1.0, 0.0).astype(BF16)
    triu = jnp.where(r <= c, 1.0, 0.0).astype(BF16)

    def selector(width, block, first):
        rr = lax.broadcasted_iota(jnp.int32, (LANES, width), 0)
        cc = lax.broadcasted_iota(jnp.int32, (LANES, width), 1)
        return jnp.where(rr == (cc >> int(math.log2(block))) + first, 1.0, 0.0).astype(BF16)

    wide = N_ML * L
    fcb = _dot_exact_rhs(fc, selector(wide, L, N_ML))
    icb = _dot_exact_rhs(gates, selector(wide, L, 0))
    fck = _dot_exact_rhs(fc, selector(qkw, ML_DK, N_ML))
    ick = _dot_exact_rhs(gates, selector(qkw, ML_DK, 0))

    ng = ng_ref[...]

    for ci in range(nchunk):
        rs = slice(ci * L, (ci + 1) * L)
        ro = slice(row0 + ci * L, row0 + (ci + 1) * L)
        bb = _dot_exact_lhs(tril, fcb[rs])
        bk = _dot_exact_lhs(tril, fck[rs])
        brow = _dot_exact_rhs(fct[:, rs], triu)
        rowterm = gt[:, rs] - pltpu.roll(brow, N_ML, 0)
        qc = q[rs]
        kc = k[rs]
        log_d = jnp.concatenate(
            [jnp.where(causal, bb[:, h * L:(h + 1) * L] + rowterm[h:h + 1, :], -jnp.inf) for h in range(N_ML)],
            axis=0)
        log_inter = jnp.concatenate(
            [bb[:, h * L:(h + 1) * L] + mwide[:, h * L:(h + 1) * L] for h in range(N_ML)], axis=0)
        m_t = jnp.maximum(log_inter, jnp.max(log_d, axis=-1, keepdims=True))
        w_inter = jnp.exp(log_inter - m_t)
        qst = jnp.concatenate([qc * masks[h] for h in range(N_ML)], axis=0)
        qsb = qst.astype(BF16)
        s_all = _dotg(qsb, kc.astype(BF16), _NT) * jnp.exp(log_d - m_t)
        sb = s_all.astype(BF16)
        vb = z[rs, c_v:c_v + vw].astype(BF16)
        num = jnp.concatenate(
            [_dot(sb[h * L:(h + 1) * L], vb[:, h * ML_DV:(h + 1) * ML_DV]) for h in range(N_ML)], axis=0)
        num = num + w_inter * _dotg(qsb, ct.astype(BF16), _NT)
        den = jnp.sum(s_all, axis=-1, keepdims=True) + w_inter * jnp.sum(qst * nrow, axis=-1, keepdims=True)
        hh = _rms_norm(num / jnp.maximum(jnp.abs(den), jnp.exp(-m_t)), ng)
        for h in range(N_ML):
            vs = slice(h * ML_DV, (h + 1) * ML_DV)
            gate_o = _sigmoid(z[rs, c_o + h * ML_DV:c_o + (h + 1) * ML_DV])
            o_ref[ro, vs] = (hh[h * L:(h + 1) * L] * gate_o).astype(BF16)
        bl_w = bb[L - 1:L, :]
        lw_w = bl_w - bb + icb[rs]
        mwide_new = jnp.maximum(bl_w + mwide, jnp.max(lw_w, axis=0, keepdims=True))
        bl_k = bk[L - 1:L, :]
        lw_k = bl_k - bk + ick[rs]
        mk_new = jnp.maximum(bl_k + mk, jnp.max(lw_k, axis=0, keepdims=True))
        decay = jnp.exp(bl_k + mk - mk_new)
        kw = kc * jnp.exp(lw_k - mk_new)
        upd = _dotg(vb, kw.astype(BF16), _TN)
        ct = ct * decay
        for h in range(N_ML):
            ct = ct + upd[h * ML_DV:(h + 1) * ML_DV] * masks[h]
        nrow = nrow * decay + jnp.sum(kw, axis=0, keepdims=True)
        mwide = mwide_new
        mk = mk_new

    return ct, nrow, mwide, mk, tail


def _mlstm_kernel(x_ref, xn_ref, w_ref, cw_ref, cb_ref, gb_ref, ng_ref, o_ref,
                  ws, za, zb, ct_ref, n_ref, mw_ref, mk_ref, tail_ref, *, tm):
    first = pl.program_id(1) == 0

    @pl.when((pl.program_id(0) == 0) & first)
    def _():
        ws[...] = w_ref[0].astype(BF16)

    @pl.when(first)
    def _():
        for ref in (ct_ref, n_ref, mw_ref, mk_ref, tail_ref):
            ref[...] = jnp.zeros_like(ref)
        za[...] = _dotg(x_ref[0:tm, :], ws[...], _NT)

    nchunk = tm // ML_CHUNK
    consts = (cw_ref, cb_ref, gb_ref, ng_ref, nchunk)
    zb[...] = _dotg(x_ref[tm:2 * tm, :], ws[...], _NT)
    carry = (ct_ref[...], n_ref[...], mw_ref[...], mk_ref[...], tail_ref[...])
    carry = _mlstm_tile(za, o_ref, 0, carry, *consts)
    za[...] = _dotg(xn_ref[...], ws[...], _NT)
    carry = _mlstm_tile(zb, o_ref, tm, carry, *consts)
    for ref, val in zip((ct_ref, n_ref, mw_ref, mk_ref, tail_ref), carry):
        ref[...] = val


def _mlstm(hb, wt, l, row0, cw, cb, gb, ng, batch, seq, tm=512):
    n, d = hb.shape
    ns2 = seq // (2 * tm)
    qkw = N_ML * ML_DK
    vw = N_ML * ML_DV
    zw = 2 * qkw + 2 * vw + LANES
    full = lambda a: pl.BlockSpec(a.shape, lambda b, s: (0,) * a.ndim)
    cb = cb.reshape(1, 2 * qkw)
    gbp = jnp.pad(gb, (0, LANES - 2 * N_ML)).reshape(1, LANES)
    ng = ng.reshape(1, ML_DV)
    last = 2 * ns2 - 1
    return pl.pallas_call(
        functools.partial(_mlstm_kernel, tm=tm), grid=(batch, ns2),
        in_specs=[pl.BlockSpec((2 * tm, d), lambda b, s: (b * ns2 + s, 0)),
                  pl.BlockSpec((tm, d), lambda b, s: (b * 2 * ns2 + jnp.minimum(2 * s + 2, last), 0)),
                  pl.BlockSpec((pl.Element(1), pl.Element(zw), pl.Element(d)), lambda b, s: (l, row0, 0),
                               pipeline_mode=pl.Buffered(1)),
                  full(cw), full(cb), full(gbp), full(ng)],
        out_specs=pl.BlockSpec((2 * tm, vw), lambda b, s: (b * ns2 + s, 0)),
        out_shape=jax.ShapeDtypeStruct((n, vw), BF16),
        scratch_shapes=[pltpu.VMEM((zw, d), BF16), pltpu.VMEM((tm, zw), F32), pltpu.VMEM((tm, zw), F32),
                        pltpu.VMEM((ML_DV, qkw), F32), pltpu.VMEM((1, qkw), F32),
                        pltpu.VMEM((1, N_ML * ML_CHUNK), F32), pltpu.VMEM((1, qkw), F32),
                        pltpu.VMEM((8, 2 * qkw), F32)],
        compiler_params=_cparams("arbitrary", "arbitrary", vmem=VMEM_LIMIT_BIG), name="mlstm")(
            hb, hb, wt, cw, cb, gbp, ng)


def _merge_out_kernel(ya_ref, yb_ref, yc_ref, yd_ref, g_ref, wb_ref, wo_ref, h_ref, lg_ref, lb_ref,
                      of_ref, ob_ref):
    acc = None
    for i, y_ref in enumerate((ya_ref, yb_ref, yc_ref, yd_ref)):
        p = _dot(y_ref[...], wb_ref[i])
        t = g_ref[:, i * D_MODEL:(i + 1) * D_MODEL].astype(F32) * p
        acc = t if acc is None else acc + t
    y = _dot(acc.astype(BF16), wo_ref[...])
    o = _layer_norm(ALPHA * h_ref[...] + y, lg_ref[...], lb_ref[...])
    of_ref[...] = o
    ob_ref[...] = o.astype(BF16)


def _merge_out(ya, yb, yc, yd, gates, wb_all, wo_all, l, hf, lg, lb, tm=256):
    n, bw = ya.shape
    d = hf.shape[1]
    row = pl.BlockSpec((tm, bw), lambda i: (i, 0))
    hrow = pl.BlockSpec((tm, d), lambda i: (i, 0))
    par = pl.BlockSpec((1, d), lambda i: (0, 0))
    once = dict(pipeline_mode=pl.Buffered(1))
    return pl.pallas_call(
        _merge_out_kernel, grid=(n // tm,),
        in_specs=[row, row, row, row, pl.BlockSpec((tm, 4 * d), lambda i: (i, 0)),
                  pl.BlockSpec((None,) + wb_all.shape[1:], lambda i: (l, 0, 0, 0), **once),
                  pl.BlockSpec((None,) + wo_all.shape[1:], lambda i: (l, 0, 0), **once), hrow, par, par],
        out_specs=[hrow, hrow],
        out_shape=[jax.ShapeDtypeStruct((n, d), F32), jax.ShapeDtypeStruct((n, d), BF16)],
        compiler_params=_cparams("parallel"), name="merge_out_ln1")(
            ya, yb, yc, yd, gates, wb_all, wo_all, hf, lg.reshape(1, d), lb.reshape(1, d))


def _xattn_kernel(hb_ref, hf_ref, wq_ref, k_ref, v_ref, wo_ref, g_ref, b_ref, wrt_ref, rbc_ref,
                  of_ref, pay_ref, dest_ref, cnt_ref, carry_ref, *, region):
    q = (_dot(hb_ref[...], wq_ref[...]) * (X_HEAD ** -0.5)).astype(BF16)
    outs = []
    for h in range(N_X):
        hs = slice(h * X_HEAD, (h + 1) * X_HEAD)
        s = _dotg(q[:, hs], k_ref[:, hs], _NT)
        p = jnp.exp(s - jnp.max(s, axis=-1, keepdims=True))
        l = jnp.sum(p, axis=-1, keepdims=True)
        outs.append((_dot(p.astype(BF16), v_ref[:, hs]) / l).astype(BF16))
    y = _dot(jnp.concatenate(outs, axis=-1), wo_ref[...])
    o = _layer_norm(ALPHA * hf_ref[...] + y, g_ref[...], b_ref[...])
    of_ref[...] = o
    is_first = (pl.program_id(0) == 0) & (pl.program_id(1) == 0)
    _route_tile(o, is_first, wrt_ref, rbc_ref, pay_ref, dest_ref, cnt_ref, carry_ref, region)


def _xattn_route(hb, hf, wq_all, kv, wo_all, l, g, b, wrt, rbc, region, batch, seq, mem_len, tm=512):
    n, d = hf.shape
    ns = seq // tm
    xw = N_X * X_HEAD
    tile = lambda bb, s: bb * ns + s
    row = pl.BlockSpec((tm, d), lambda bb, s: (tile(bb, s), 0))
    par = pl.BlockSpec((1, d), lambda bb, s: (0, 0))
    return pl.pallas_call(
        functools.partial(_xattn_kernel, region=region), grid=(batch, ns),
        in_specs=[row, row, pl.BlockSpec((None, d, xw), lambda bb, s: (l, 0, 0)),
                  pl.BlockSpec((mem_len, xw), lambda bb, s: (bb, 0)),
                  pl.BlockSpec((mem_len, xw), lambda bb, s: (bb, 1)),
                  pl.BlockSpec((None, xw, d), lambda bb, s: (l, 0, 0)), par, par,
                  pl.BlockSpec((LANES, d), lambda bb, s: (0, 0)),
                  pl.BlockSpec((N_EXPERTS, 1), lambda bb, s: (0, 0))],
        out_specs=[row, pl.BlockSpec((tm, PAY_W), lambda bb, s: (tile(bb, s), 0)),
                   pl.BlockSpec((1, tm), lambda bb, s: (0, tile(bb, s))),
                   pl.BlockSpec((8, LANES), lambda bb, s: (0, 0))],
        out_shape=[jax.ShapeDtypeStruct((n, d), F32), jax.ShapeDtypeStruct((n, PAY_W), U32),
                   jax.ShapeDtypeStruct((1, n), jnp.int32), jax.ShapeDtypeStruct((8, LANES), jnp.int32)],
        scratch_shapes=[pltpu.VMEM((8, LANES), F32)],
        compiler_params=_cparams("arbitrary", "arbitrary"), name="xattn_route")(
            hb, hf, wq_all, kv, kv, wo_all, g.reshape(1, d), b.reshape(1, d), wrt, rbc)


N_GROUPS = 4
PER_GROUP = N_EXPERTS // N_GROUPS
MOE_TILE = 512
PAY_X = D_MODEL // 2
PAY_W = PAY_X + LANES
U32 = jnp.uint32
_HI16 = 0xFFFF0000


def _pack_bf16_pairs(x):
    u = lax.bitcast_convert_type(x.astype(BF16).astype(F32), U32)
    w = x.shape[1] // 2
    return (u[:, :w] >> 16) | (u[:, w:] & jnp.uint32(_HI16))


def _unpack_bf16_pairs(words):
    lo = lax.bitcast_convert_type(words << 16, F32)
    hi = lax.bitcast_convert_type(words & jnp.uint32(_HI16), F32)
    return jnp.concatenate([lo, hi], axis=1).astype(BF16)


def _route_tile(h, is_first, wrt_ref, rbc_ref, pay_ref, dest_ref, cnt_ref, carry_ref, region):
    tm = h.shape[0]

    @pl.when(is_first)
    def _():
        carry_ref[...] = jnp.zeros_like(carry_ref)

    w = wrt_ref[...]
    w1 = w.astype(BF16)
    w2 = (w - w1.astype(F32)).astype(BF16)
    h1 = h.astype(BF16)
    h2 = (h - h1.astype(F32)).astype(BF16)
    logits = _dotg(w1, h1, _NT) + _dotg(w2, h1, _NT) + _dotg(w1, h2, _NT)
    aff = _sigmoid(logits[0:N_EXPERTS])
    biased = aff + rbc_ref[...]
    row = lax.broadcasted_iota(jnp.int32, (N_EXPERTS, 1), 0).astype(F32)
    big = float(LANES)
    best = e1 = e2 = None
    for g in range(N_GROUPS):
        x = jnp.where((row >= g * PER_GROUP) & (row < (g + 1) * PER_GROUP), biased, -jnp.inf)
        m1 = jnp.max(x, axis=0, keepdims=True)
        i1 = jnp.min(jnp.where(x == m1, row, big), axis=0, keepdims=True)
        x2 = jnp.where(row == i1, -jnp.inf, x)
        m2 = jnp.max(x2, axis=0, keepdims=True)
        i2 = jnp.min(jnp.where(x2 == m2, row, big), axis=0, keepdims=True)
        score = m1 + m2
        if g == 0:
            best, e1, e2 = score, i1, i2
        else:
            better = score > best
            best = jnp.where(better, score, best)
            e1 = jnp.where(better, i1, e1)
            e2 = jnp.where(better, i2, e2)
    s1 = jnp.sum(jnp.where(row == e1, aff, 0.0), axis=0, keepdims=True)
    s2 = jnp.sum(jnp.where(row == e2, aff, 0.0), axis=0, keepdims=True)
    tot = s1 + s2
    gates_t = jnp.where(row == e1, s1 / tot, 0.0) + jnp.where(row == e2, s2 / tot, 0.0)

    grp = jnp.zeros_like(e1)
    for g in range(1, N_GROUPS):
        grp = grp + jnp.where(e1 >= g * PER_GROUP, 1.0, 0.0)
    row8 = lax.broadcasted_iota(jnp.int32, (8, 1), 0).astype(F32)
    onehot = jnp.where(row8 == grp, 1.0, 0.0)
    r = lax.broadcasted_iota(jnp.int32, (tm, tm), 0)
    c = lax.broadcasted_iota(jnp.int32, (tm, tm), 1)
    earlier = jnp.where(r < c, 1.0, 0.0).astype(BF16)
    rank_in = _dot(onehot.astype(BF16), earlier)
    carry = carry_ref[...]
    rank = jnp.sum(onehot * (rank_in + carry[:, 0:1]), axis=0, keepdims=True)
    dest_ref[...] = (grp * float(region) + rank).astype(jnp.int32)
    carry = carry + jnp.sum(onehot, axis=1, keepdims=True)
    carry_ref[...] = carry
    cnt_ref[...] = carry.astype(jnp.int32)

    gates = jnp.concatenate([gates_t, jnp.zeros((LANES - N_EXPERTS, tm), F32)], axis=0).T
    pay_ref[:, :PAY_X] = _pack_bf16_pairs(h)
    pay_ref[:, PAY_X:] = lax.bitcast_convert_type(gates, U32)


def _scatter_kernel(dest_ref, cnt_ref, pay_ref, out_ref, buf, zbuf, sem, zsem, *, region, nsteps):
    tm = pay_ref.shape[0]
    i = pl.program_id(0)
    slot = i % 2

    def wait_rows(s):
        pltpu.make_async_copy(buf.at[s], out_ref.at[pl.ds(0, tm)], sem.at[s]).wait()

    @pl.when(i == 0)
    def _():
        zbuf[...] = jnp.zeros_like(zbuf)
        copies = []
        for g in range(N_GROUPS):
            start = pl.multiple_of(g * region + (cnt_ref[g] // MOE_TILE) * MOE_TILE, MOE_TILE)
            copies.append(pltpu.make_async_copy(zbuf, out_ref.at[pl.ds(start, MOE_TILE)], zsem))
        for cp in copies:
            cp.start()
        for cp in copies:
            cp.wait()

    @pl.when(i >= 2)
    def _():
        wait_rows(slot)

    buf[slot] = pay_ref[...]

    base = i * tm
    for r in range(tm):
        d = dest_ref[base + r]
        pltpu.make_async_copy(buf.at[slot, pl.ds(r, 1)], out_ref.at[pl.ds(d, 1)],
                              sem.at[slot]).start(priority=r % 2)

    @pl.when(i == nsteps - 1)
    def _():
        wait_rows(slot)
        if nsteps >= 2:
            wait_rows(1 - slot)


def _scatter(dest, cnt, pay, region, tm=512):
    n = pay.shape[0]
    nsteps = n // tm
    return pl.pallas_call(
        functools.partial(_scatter_kernel, region=region, nsteps=nsteps),
        grid_spec=pltpu.PrefetchScalarGridSpec(
            num_scalar_prefetch=2, grid=(nsteps,),
            in_specs=[pl.BlockSpec((tm, PAY_W), lambda i, d, c: (i, 0))],
            out_specs=pl.BlockSpec(memory_space=pl.ANY),
            scratch_shapes=[pltpu.VMEM((2, tm, PAY_W), U32), pltpu.VMEM((MOE_TILE, PAY_W), U32),
                            pltpu.SemaphoreType.DMA((2,)), pltpu.SemaphoreType.DMA(())]),
        out_shape=jax.ShapeDtypeStruct((N_GROUPS * region, PAY_W), U32),
        compiler_params=_cparams("arbitrary"), name="moe_scatter")(dest, cnt, pay)


def _tile_tables(cnt, region, ntiles):
    nt = (cnt + MOE_TILE - 1) // MOE_TILE
    ends = jnp.cumsum(nt)
    starts = ends - nt
    total = ends[-1]
    i = jnp.minimum(jnp.arange(ntiles, dtype=jnp.int32), total - 1)
    g = jnp.sum((i[:, None] >= ends[None, :]).astype(jnp.int32), axis=1)
    blk = g * (region // MOE_TILE) + i - starts[g]
    return g.astype(jnp.int32), blk.astype(jnp.int32), total.reshape(1).astype(jnp.int32)


def _moe_up_kernel(tg_ref, tb_ref, nt_ref, x_ref, wg_ref, wu_ref, o_ref, wgs, wus):
    e = pl.program_id(0)
    i = pl.program_id(1)
    grp = tg_ref[i]
    changed = (i == 0) | (grp != tg_ref[jnp.maximum(i - 1, 0)])

    @pl.when(changed)
    def _():
        wgs[...] = wg_ref[...].astype(BF16)
        wus[...] = wu_ref[...].astype(BF16)

    @pl.when(i < nt_ref[0])
    def _():
        x = _unpack_bf16_pairs(x_ref[:, :PAY_X])
        gates = lax.bitcast_convert_type(x_ref[:, PAY_X:], F32)
        lane = lax.broadcasted_iota(jnp.int32, (1, LANES), 1)
        ge = jnp.sum(jnp.where(lane == grp * PER_GROUP + e, gates, 0.0), axis=-1, keepdims=True)
        a = _dot(x, wgs[...])
        u = _dot(x, wus[...])
        o_ref[...] = (a * _sigmoid(a) * u * ge).astype(BF16)


def _moe_up(tg, tb, nt, xs, wg, wu, l):
    rows = xs.shape[0]
    ntiles = tg.shape[0]
    d, de = wg.shape[2], wg.shape[3]
    wspec = pl.BlockSpec((None, None, d, de), lambda e, i, tg, tb, nt: (l, tg[i] * PER_GROUP + e, 0, 0))
    return pl.pallas_call(
        _moe_up_kernel,
        grid_spec=pltpu.PrefetchScalarGridSpec(
            num_scalar_prefetch=3, grid=(PER_GROUP, ntiles),
            in_specs=[pl.BlockSpec((MOE_TILE, PAY_W), lambda e, i, tg, tb, nt: (tb[i], 0)), wspec, wspec],
            out_specs=pl.BlockSpec((MOE_TILE, de), lambda e, i, tg, tb, nt: (tb[i], e)),
            scratch_shapes=[pltpu.VMEM((d, de), BF16), pltpu.VMEM((d, de), BF16)]),
        out_shape=jax.ShapeDtypeStruct((rows, PER_GROUP * de), BF16),
        compiler_params=_cparams("arbitrary", "arbitrary"), name="moe_up")(tg, tb, nt, xs, wg, wu)


def _moe_down_kernel(tg_ref, tb_ref, nt_ref, x_ref, w_ref, o_ref, ws):
    i = pl.program_id(1)
    changed = (i == 0) | (tg_ref[i] != tg_ref[jnp.maximum(i - 1, 0)])

    @pl.when(changed)
    def _():
        ws[...] = w_ref[...].astype(BF16)

    @pl.when(i < nt_ref[0])
    def _():
        o_ref[...] = _dot(x_ref[...], ws[...])


def _moe_down(tg, tb, nt, hid, wd, l, tn=1024):
    rows, k = hid.shape
    ntiles = tg.shape[0]
    d = wd.shape[3]
    return pl.pallas_call(
        _moe_down_kernel,
        grid_spec=pltpu.PrefetchScalarGridSpec(
            num_scalar_prefetch=3, grid=(d // tn, ntiles),
            in_specs=[pl.BlockSpec((MOE_TILE, k), lambda c, i, tg, tb, nt: (tb[i], 0)),
                      pl.BlockSpec((None, None, k, tn), lambda c, i, tg, tb, nt: (l, tg[i], 0, c))],
            out_specs=pl.BlockSpec((MOE_TILE, tn), lambda c, i, tg, tb, nt: (tb[i], c)),
            scratch_shapes=[pltpu.VMEM((k, tn), BF16)]),
        out_shape=jax.ShapeDtypeStruct((rows, d), F32),
        compiler_params=_cparams("arbitrary", "arbitrary"), name="moe_down")(tg, tb, nt, hid, wd)


def _gather_ln_kernel(dest_ref, y_ref, h_ref, g_ref, b_ref, of_ref, ob_ref, buf, sem, *, nsteps):
    tm = h_ref.shape[0]
    i = pl.program_id(0)

    def issue(step, slot):
        base = step * tm
        for r in range(tm):
            d = dest_ref[base + r]
            pltpu.make_async_copy(y_ref.at[pl.ds(d, 1)], buf.at[slot, pl.ds(r, 1)],
                                  sem.at[slot]).start(priority=r % 2)

    @pl.when(i == 0)
    def _():
        issue(0, 0)

    @pl.when(i + 1 < nsteps)
    def _():
        issue(i + 1, (i + 1) % 2)

    slot = i % 2
    pltpu.make_async_copy(y_ref.at[pl.ds(0, tm)], buf.at[slot], sem.at[slot]).wait()
    o = _layer_norm(ALPHA * h_ref[...] + buf[slot], g_ref[...], b_ref[...])
    of_ref[...] = o
    ob_ref[...] = o.astype(BF16)


def _gather_ln(dest, ys, hf, g, b, tm=512):
    n, d = hf.shape
    nsteps = n // tm
    row = pl.BlockSpec((tm, d), lambda i, dref: (i, 0))
    par = pl.BlockSpec((1, d), lambda i, dref: (0, 0))
    return pl.pallas_call(
        functools.partial(_gather_ln_kernel, nsteps=nsteps),
        grid_spec=pltpu.PrefetchScalarGridSpec(
            num_scalar_prefetch=1, grid=(nsteps,),
            in_specs=[pl.BlockSpec(memory_space=pl.ANY), row, par, par],
            out_specs=[row, row],
            scratch_shapes=[pltpu.VMEM((2, tm, d), F32), pltpu.SemaphoreType.DMA((2,))]),
        out_shape=[jax.ShapeDtypeStruct((n, d), F32), jax.ShapeDtypeStruct((n, d), BF16)],
        compiler_params=_cparams("arbitrary"), name="moe_gather_ln3")(
            dest, ys, hf, g.reshape(1, d), b.reshape(1, d))


def _moe_region(n):
    return n + MOE_TILE


def _moe(hf, pay, dest, cnt, p, w, l):
    n = hf.shape[0]
    region = _moe_region(n)
    ntiles = n // MOE_TILE + N_GROUPS
    dest = dest.reshape(n)
    cnt = cnt[:N_GROUPS, 0]
    tg, tb, nt = _tile_tables(cnt, region, ntiles)
    xs = _scatter(dest, cnt, pay, region)
    hid = _moe_up(tg, tb, nt, xs, p["moe_w_gate"], p["moe_w_up"], l)
    ys = _moe_down(tg, tb, nt, hid, w["moe_wd"], l)
    return _gather_ln(dest, ys, hf, p["ln3_g"][l], p["ln3_b"][l])


def _prep_params(p):
    L = p["w_in"].shape[0]
    half = MLA_ROPE // 2
    pad_r = LANES - MLA_ROPE

    wq = p["mla_w_uq"].reshape(L, MLA_Q_RANK, N_MLA, MLA_NOPE + MLA_ROPE)
    rq = wq[..., MLA_NOPE:]
    zq = jnp.zeros((L, MLA_Q_RANK, N_MLA, pad_r), F32)
    wuq = jnp.concatenate([wq[..., :MLA_NOPE], rq, zq, rq[..., half:], rq[..., :half], zq], axis=-1)
    wkv = p["mla_w_ukv"].reshape(L, MLA_KV_RANK, N_MLA, MLA_NOPE + MLA_V)

    bf = lambda a: a.astype(BF16)
    ne = p["moe_w_down"].shape[1]
    return dict(
        wt_in=jnp.swapaxes(p["w_in"], 1, 2),
        wuq=bf(wuq.reshape(L, MLA_Q_RANK, N_MLA * 3 * LANES)),
        wuk=bf(wkv[..., :MLA_NOPE].reshape(L, MLA_KV_RANK, N_MLA * MLA_NOPE)),
        wuv=bf(wkv[..., MLA_NOPE:].reshape(L, MLA_KV_RANK, N_MLA * MLA_V)),
        gla_wg=jnp.pad(p["gla_w_gate"], ((0, 0), (0, LANES - GLA_RANK), (0, 0))),
        w_branch=bf(p["w_branch"]), w_out=bf(p["w_out"]),
        x_w_q=bf(p["x_w_q"]), x_w_kv=bf(p["x_w_kv"]), x_w_o=bf(p["x_w_o"]),
        w_router_t=jnp.pad(p["w_router"].T, ((0, LANES - N_EXPERTS), (0, 0))),
        router_bias_c=p["router_bias"].reshape(N_EXPERTS, 1),
        moe_wd=p["moe_w_down"].reshape(L, N_GROUPS, (ne // N_GROUPS) * D_EXPERT, D_MODEL),
    )


def _mixer(hf, hb, cos_p, sin_p, p, w, l, batch, seq):
    wt = w["wt_in"]
    o_mla = 2 * D_SG
    o_gla = o_mla + MLA_Q_RANK + MLA_KV_RANK + MLA_ROPE
    o_ml = o_gla + 2 * N_GLA * GLA_DK + 2 * N_GLA * GLA_DV + GLA_RANK
    o_gate = o_ml + 2 * N_ML * ML_DK + 2 * N_ML * ML_DV + 2 * N_ML
    gates = _mm_wt(hb, wt, l, o_gate, 4 * D_MODEL // 1024, 1024, BF16, act="sigmoid", name="mm_gate")

    y_a = _sg(hb, wt, l, 0, p["sg_vnorm_g"][l], p["sg_vnorm_b"][l], p["sg_w_s"][l], p["sg_b_s"][l])
    q, k, v = _mla_prep(hb, wt, l, o_mla, cos_p, sin_p, p["mla_qnorm_g"][l], p["mla_kvnorm_g"][l],
                        w["wuq"][l], w["wuk"][l], w["wuv"][l])
    y_b = _mla_attn(q, k, v, batch, seq)
    y_c = _gla(hb, wt, l, o_gla, w["gla_wg"][l], p["gla_b_gate"][l], p["gla_norm_g"][l], batch, seq)
    y_d = _mlstm(hb, wt, l, o_ml, p["ml_conv_w"][l], p["ml_conv_b"][l], p["ml_gate_b"][l], p["ml_norm_g"][l],
                 batch, seq)
    return _merge_out(y_a, y_b, y_c, y_d, gates, w["w_branch"], w["w_out"], l, hf,
                      p["ln1_g"][l], p["ln1_b"][l])


def _forward(p):
    x = p["x"]
    batch, seq, d = x.shape
    n = batch * seq
    mem = p["mem"]
    mem_len = mem.shape[1]
    w = _prep_params(p)
    posb = jnp.broadcast_to(p["positions"].reshape(n, 1).astype(F32), (n, LANES))
    cos_p, sin_p = _rope_tables(posb)
    memb = mem.reshape(batch * mem_len, d).astype(BF16)
    hf, hb = _ln(x.reshape(n, d), p["ln_in_g"], p["ln_in_b"])
    for l in range(p["w_in"].shape[0]):
        hf, hb = _mixer(hf, hb, cos_p, sin_p, p, w, l, batch, seq)
        kv = _mm(memb, w["x_w_kv"][l], BF16, tm=512, tn=1024, name="mm_xkv")
        hf, pay, dest, cnt = _xattn_route(hb, hf, w["x_w_q"], kv, w["x_w_o"], l, p["ln2_g"][l], p["ln2_b"][l],
                                          w["w_router_t"], w["router_bias_c"], _moe_region(n),
                                          batch, seq, mem_len)
        hf, hb = _moe(hf, pay, dest, cnt, p, w, l)
    return hf.reshape(batch, seq, d)


def kernel(x, mem, positions, ln_in_g, ln_in_b, w_in, sg_vnorm_g, sg_vnorm_b, sg_w_s, sg_b_s, mla_qnorm_g, mla_kvnorm_g, mla_w_uq, mla_w_ukv, gla_w_gate, gla_b_gate, gla_norm_g, ml_conv_w, ml_conv_b, ml_gate_b, ml_norm_g, w_branch, w_out, ln1_g, ln1_b, x_w_q, x_w_kv, x_w_o, ln2_g, ln2_b, w_router, router_bias, moe_w_gate, moe_w_up, moe_w_down, ln3_g, ln3_b):
    return _forward(dict(
        x=x, mem=mem, positions=positions, ln_in_g=ln_in_g, ln_in_b=ln_in_b, w_in=w_in,
        sg_vnorm_g=sg_vnorm_g, sg_vnorm_b=sg_vnorm_b, sg_w_s=sg_w_s, sg_b_s=sg_b_s,
        mla_qnorm_g=mla_qnorm_g, mla_kvnorm_g=mla_kvnorm_g, mla_w_uq=mla_w_uq, mla_w_ukv=mla_w_ukv,
        gla_w_gate=gla_w_gate, gla_b_gate=gla_b_gate, gla_norm_g=gla_norm_g,
        ml_conv_w=ml_conv_w, ml_conv_b=ml_conv_b, ml_gate_b=ml_gate_b, ml_norm_g=ml_norm_g,
        w_branch=w_branch, w_out=w_out, ln1_g=ln1_g, ln1_b=ln1_b,
        x_w_q=x_w_q, x_w_kv=x_w_kv, x_w_o=x_w_o, ln2_g=ln2_g, ln2_b=ln2_b,
        w_router=w_router, router_bias=router_bias,
        moe_w_gate=moe_w_gate, moe_w_up=moe_w_up, moe_w_down=moe_w_down, ln3_g=ln3_g, ln3_b=ln3_b))
```

```python
import functools
import math

import jax
import jax.numpy as jnp
from jax import lax
from jax.experimental import pallas as pl
from jax.experimental.pallas import tpu as pltpu

F32 = jnp.float32
BF16 = jnp.bfloat16

D_MODEL = 2048
DEPTH = 4
EPS = 1e-5
ALPHA = (2.0 * DEPTH) ** 0.25

SG_CHUNK = 128
N_SG = 4
D_SG = 512
N_MLA = 4
MLA_Q_RANK = 384
MLA_KV_RANK = 256
MLA_NOPE = 128
MLA_ROPE = 64
MLA_V = 128
MLA_QK = 256
ROPE_BASE = 10000.0
N_GLA = 4
GLA_DK = 64
GLA_DV = 128
GLA_RANK = 16
GLA_TAU = 16.0
GLA_CHUNK = 64
N_ML = 4
ML_DK = 64
ML_DV = 128
ML_CHUNK = 128
N_X = 4
X_HEAD = 128
N_EXPERTS = 16
D_EXPERT = 512

LANES = 128
SUBLANES = 8
VMEM_LIMIT = 48 * 1024 * 1024
VMEM_LIMIT_BIG = 56 * 1024 * 1024

_NT = (((1,), (1,)), ((), ()))
_TN = (((0,), (0,)), ((), ()))


def _cparams(*sem, vmem=VMEM_LIMIT):
    return pltpu.CompilerParams(dimension_semantics=sem, vmem_limit_bytes=vmem)


def _dot(a, b):
    return jnp.dot(a, b, preferred_element_type=F32)


def _dotg(a, b, dims):
    return lax.dot_general(a, b, dims, preferred_element_type=F32)


def _split3(a):
    a1 = a.astype(BF16)
    r1 = a - a1.astype(F32)
    a2 = r1.astype(BF16)
    a3 = (r1 - a2.astype(F32)).astype(BF16)
    return a1, a2, a3


def _dot_exact_rhs(a, ones_bf16):
    a1, a2, a3 = _split3(a)
    return _dot(a1, ones_bf16) + _dot(a2, ones_bf16) + _dot(a3, ones_bf16)


def _dot_exact_lhs(ones_bf16, a):
    a1, a2, a3 = _split3(a)
    return _dot(ones_bf16, a1) + _dot(ones_bf16, a2) + _dot(ones_bf16, a3)


def _dot_hi(a, b):
    a1 = a.astype(BF16)
    a2 = (a - a1.astype(F32)).astype(BF16)
    b1 = b.astype(BF16)
    b2 = (b - b1.astype(F32)).astype(BF16)
    return _dot(a1, b1) + _dot(a2, b1) + _dot(a1, b2)


def _sigmoid(x):
    return 1.0 / (1.0 + jnp.exp(-x))


def _log_sigmoid(x):
    return jnp.minimum(x, 0.0) - jnp.log(1.0 + jnp.exp(-jnp.abs(x)))


def _layer_norm(t, g, b):
    mu = jnp.mean(t, axis=-1, keepdims=True)
    c = t - mu
    var = jnp.mean(c * c, axis=-1, keepdims=True)
    return c * lax.rsqrt(var + EPS) * g + b


def _rms_norm(t, g):
    return t * lax.rsqrt(jnp.mean(t * t, axis=-1, keepdims=True) + EPS) * g


def _ln_kernel(x_ref, g_ref, b_ref, of_ref, ob_ref):
    y = _layer_norm(x_ref[...], g_ref[...], b_ref[...])
    of_ref[...] = y
    ob_ref[...] = y.astype(BF16)


def _ln(x, g, b, tm=256):
    n, d = x.shape
    row = pl.BlockSpec((tm, d), lambda i: (i, 0))
    par = pl.BlockSpec((1, d), lambda i: (0, 0))
    return pl.pallas_call(
        _ln_kernel, grid=(n // tm,), in_specs=[row, par, par], out_specs=[row, row],
        out_shape=[jax.ShapeDtypeStruct((n, d), F32), jax.ShapeDtypeStruct((n, d), BF16)],
        compiler_params=_cparams("parallel"), name="ln_in")(x, g.reshape(1, d), b.reshape(1, d))


def _mm_kernel(x_ref, w_ref, o_ref, *, act):
    acc = _dot(x_ref[...], w_ref[...])
    if act == "sigmoid":
        acc = _sigmoid(acc)
    o_ref[...] = acc.astype(o_ref.dtype)


def _mm(x, w, out_dtype, act=None, tm=512, tn=None, name="mm"):
    n, k = x.shape
    m = w.shape[1]
    tn = m if tn is None else tn
    tm = min(tm, n)
    assert n % tm == 0 and m % tn == 0
    return pl.pallas_call(
        functools.partial(_mm_kernel, act=act), grid=(n // tm, m // tn),
        in_specs=[pl.BlockSpec((tm, k), lambda i, j: (i, 0)),
                  pl.BlockSpec((k, tn), lambda i, j: (0, j))],
        out_specs=pl.BlockSpec((tm, tn), lambda i, j: (i, j)),
        out_shape=jax.ShapeDtypeStruct((n, m), out_dtype),
        compiler_params=_cparams("parallel", "parallel"), name=name)(x, w)


def _mm_wt_kernel(x_ref, w_ref, o_ref, ws, *, act):
    @pl.when(pl.program_id(1) == 0)
    def _():
        ws[...] = w_ref[0].astype(BF16)

    acc = _dotg(x_ref[...], ws[...], _NT)
    if act == "sigmoid":
        acc = _sigmoid(acc)
    o_ref[...] = acc.astype(o_ref.dtype)


def _mm_wt(x, wt_all, l, row0, nblk, tn, out_dtype, act=None, tm=1024, name="mm_wt"):
    n, k = x.shape
    assert row0 % SUBLANES == 0
    wmode = dict(pipeline_mode=pl.Buffered(1)) if nblk == 1 else {}
    return pl.pallas_call(
        functools.partial(_mm_wt_kernel, act=act), grid=(nblk, n // tm),
        in_specs=[pl.BlockSpec((tm, k), lambda j, i: (i, 0)),
                  pl.BlockSpec((pl.Element(1), pl.Element(tn), pl.Element(k)),
                               lambda j, i: (l, pl.multiple_of(row0 + j * tn, SUBLANES), 0), **wmode)],
        out_specs=pl.BlockSpec((tm, tn), lambda j, i: (i, j)),
        out_shape=jax.ShapeDtypeStruct((n, nblk * tn), out_dtype),
        scratch_shapes=[pltpu.VMEM((tn, k), BF16)],
        compiler_params=_cparams("arbitrary", "arbitrary"), name=name)(x, wt_all)


def _project_tile(x_ref, w_ref, wbf):
    @pl.when(pl.program_id(0) == 0)
    def _():
        wbf[...] = w_ref[0].astype(BF16)

    return _dotg(x_ref[...], wbf[...], _NT)


def _proj_specs(d, l, row0, width):
    wspec = pl.BlockSpec((pl.Element(1), pl.Element(width), pl.Element(d)), lambda i: (l, row0, 0),
                         pipeline_mode=pl.Buffered(1))
    return wspec, pltpu.VMEM((width, d), BF16)


def _sg_kernel(x_ref, w_ref, vg_ref, vb_ref, ws_ref, bst_ref, o_ref, wbf, *, nchunk):
    z = _project_tile(x_ref, w_ref, wbf)
    z = 0.5 * z * (1.0 + jnp.tanh(math.sqrt(2.0 / math.pi) * (z + 0.044715 * (z * z * z))))
    u = z[:, :D_SG]
    vn = _layer_norm(z[:, D_SG:], vg_ref[...], vb_ref[...]).astype(BF16)
    r = lax.broadcasted_iota(jnp.int32, (SG_CHUNK, SG_CHUNK), 0)
    c = lax.broadcasted_iota(jnp.int32, (SG_CHUNK, SG_CHUNK), 1)
    causal = c <= r
    gw = SG_CHUNK
    for g in range(N_SG):
        w = jnp.where(causal, ws_ref[g], 0.0).astype(BF16)
        bias = bst_ref[:, g:g + 1]
        for ci in range(nchunk):
            rs = slice(ci * SG_CHUNK, (ci + 1) * SG_CHUNK)
            cs = slice(g * gw, (g + 1) * gw)
            mixed = _dot(w, vn[rs, cs]) + bias
            o_ref[rs, cs] = (u[rs, cs] * mixed).astype(BF16)


def _sg(hb, wt, l, row0, vg, vb, ws, bs, tm=512):
    n, d = hb.shape
    wspec, wscratch = _proj_specs(d, l, row0, 2 * D_SG)
    return pl.pallas_call(
        functools.partial(_sg_kernel, nchunk=tm // SG_CHUNK), grid=(n // tm,),
        in_specs=[pl.BlockSpec((tm, d), lambda i: (i, 0)), wspec,
                  pl.BlockSpec((1, D_SG), lambda i: (0, 0)),
                  pl.BlockSpec((1, D_SG), lambda i: (0, 0)),
                  pl.BlockSpec((N_SG, SG_CHUNK, SG_CHUNK), lambda i: (0, 0, 0)),
                  pl.BlockSpec((SG_CHUNK, N_SG), lambda i: (0, 0))],
        out_specs=pl.BlockSpec((tm, D_SG), lambda i: (i, 0)),
        out_shape=jax.ShapeDtypeStruct((n, D_SG), BF16),
        scratch_shapes=[wscratch],
        compiler_params=_cparams("arbitrary"), name="sg")(
            hb, wt, vg.reshape(1, D_SG), vb.reshape(1, D_SG), ws, bs.T)


def _rope_kernel(pos_ref, cos_ref, sin_ref):
    half = MLA_ROPE // 2
    lane = lax.broadcasted_iota(jnp.int32, (1, LANES), 1)
    idx = jnp.bitwise_and(lane, half - 1).astype(F32)
    freq = jnp.exp(idx * (-math.log(ROPE_BASE) / half))
    ang = pos_ref[...] * freq
    c = jnp.cos(ang)
    s = jnp.sin(ang)
    cos_ref[...] = jnp.where(lane < MLA_ROPE, c, 0.0)
    sin_ref[...] = jnp.where(lane < half, -s, jnp.where(lane < MLA_ROPE, s, 0.0))


def _rope_tables(posb, tm=512):
    n = posb.shape[0]
    row = pl.BlockSpec((tm, LANES), lambda i: (i, 0))
    return pl.pallas_call(
        _rope_kernel, grid=(n // tm,), in_specs=[row], out_specs=[row, row],
        out_shape=[jax.ShapeDtypeStruct((n, LANES), F32)] * 2,
        compiler_params=_cparams("parallel"), name="rope_tables")(posb)


def _mla_prep_kernel(x_ref, w_ref, cos_ref, sin_ref, qg_ref, kvg_ref, wuq_ref, wuk_ref, wuv_ref,
                     q_ref, k_ref, v_ref, wbf):
    z = _project_tile(x_ref, w_ref, wbf)
    cq =_rms_norm(z[:, :MLA_Q_RANK], qg_ref[...]).astype(BF16)
    o1 = MLA_Q_RANK + MLA_KV_RANK
    ckv = _rms_norm(z[:, MLA_Q_RANK:o1], kvg_ref[...]).astype(BF16)
    cos_p = cos_ref[...]
    sin_p = sin_ref[...]
    kr = z[:, o1:o1 + LANES]
    half = MLA_ROPE // 2
    lane = lax.broadcasted_iota(jnp.int32, (1, LANES), 1)
    kr_swapped = jnp.where(lane < half, pltpu.roll(kr, LANES - half, 1), pltpu.roll(kr, half, 1))
    k_tail = (kr * cos_p + kr_swapped * sin_p).astype(BF16)
    qa = _dot(cq, wuq_ref[...])
    kn = _dot(ckv, wuk_ref[...])
    v_ref[...] = _dot(ckv, wuv_ref[...]).astype(BF16)
    scale = (MLA_NOPE + MLA_ROPE) ** -0.5
    for h in range(N_MLA):
        b0 = h * 3 * LANES
        q_tail = qa[:, b0 + LANES:b0 + 2 * LANES] * cos_p + qa[:, b0 + 2 * LANES:b0 + 3 * LANES] * sin_p
        q_ref[:, h * MLA_QK:h * MLA_QK + LANES] = (qa[:, b0:b0 + LANES] * scale).astype(BF16)
        q_ref[:, h * MLA_QK + LANES:(h + 1) * MLA_QK] = (q_tail * scale).astype(BF16)
        k_ref[:, h * MLA_QK:h * MLA_QK + LANES] = kn[:, h * LANES:(h + 1) * LANES].astype(BF16)
        k_ref[:, h * MLA_QK + LANES:(h + 1) * MLA_QK] = k_tail


def _mla_prep(hb, wt, l, row0, cos_p, sin_p, qg, kvg, wuq, wuk, wuv, tm=512):
    n, d = hb.shape
    row = lambda w: pl.BlockSpec((tm, w), lambda i: (i, 0))
    full = lambda a: pl.BlockSpec(a.shape, lambda i: (0,) * a.ndim)
    qg = qg.reshape(1, -1)
    kvg = kvg.reshape(1, -1)
    wspec, wscratch = _proj_specs(d, l, row0, MLA_Q_RANK + MLA_KV_RANK + LANES)
    return pl.pallas_call(
        _mla_prep_kernel, grid=(n // tm,),
        in_specs=[row(d), wspec, row(LANES), row(LANES), full(qg), full(kvg), full(wuq), full(wuk), full(wuv)],
        out_specs=[row(N_MLA * MLA_QK), row(N_MLA * MLA_QK), row(N_MLA * MLA_V)],
        out_shape=[jax.ShapeDtypeStruct((n, N_MLA * MLA_QK), BF16),
                   jax.ShapeDtypeStruct((n, N_MLA * MLA_QK), BF16),
                   jax.ShapeDtypeStruct((n, N_MLA * MLA_V), BF16)],
        scratch_shapes=[wscratch],
        compiler_params=_cparams("arbitrary"), name="mla_prep")(
            hb, wt, cos_p, sin_p, qg, kvg, wuq, wuk, wuv)


def _mla_attn_kernel(q_ref, k_ref, v_ref, o_ref, *, t, nblk):
    krow = lax.broadcasted_iota(jnp.int32, (t, t), 0)
    qcol = lax.broadcasted_iota(jnp.int32, (t, t), 1)
    visible = krow <= qcol
    for i in range(nblk):
        q = q_ref[i * t:(i + 1) * t, :]
        m = jnp.full((1, t), -jnp.inf, F32)
        l = jnp.zeros((1, t), F32)
        acc = jnp.zeros((MLA_V, t), F32)
        for j in range(i + 1):
            ks = slice(j * t, (j + 1) * t)
            s = _dotg(k_ref[ks, :], q, _NT)
            if j == i:
                s = jnp.where(visible, s, -jnp.inf)
            m_new = jnp.maximum(m, jnp.max(s, axis=0, keepdims=True))
            p = jnp.exp(s - m_new)
            a = jnp.exp(m - m_new)
            l = a * l + jnp.sum(p, axis=0, keepdims=True)
            acc = a * acc + _dotg(v_ref[ks, :], p.astype(BF16), _TN)
            m = m_new
        o_ref[i * t:(i + 1) * t, :] = (acc / l).T.astype(BF16)


def _mla_attn(q, k, v, batch, seq, t=512):
    n = q.shape[0]
    return pl.pallas_call(
        functools.partial(_mla_attn_kernel, t=t, nblk=seq // t), grid=(batch, N_MLA),
        in_specs=[pl.BlockSpec((seq, MLA_QK), lambda b, h: (b, h)),
                  pl.BlockSpec((seq, MLA_QK), lambda b, h: (b, h)),
                  pl.BlockSpec((seq, MLA_V), lambda b, h: (b, h))],
        out_specs=pl.BlockSpec((seq, MLA_V), lambda b, h: (b, h)),
        out_shape=jax.ShapeDtypeStruct((n, N_MLA * MLA_V), BF16),
        compiler_params=_cparams("parallel", "parallel"), name="mla_attn")(q, k, v)


def _gla_tile(z, o_ref, row0, st, wg_ref, bg_ref, ng_ref, nchunk):
    L = GLA_CHUNK
    qkw = N_GLA * GLA_DK
    vw = N_GLA * GLA_DV
    c_k, c_v, c_o, c_lr = qkw, 2 * qkw, 2 * qkw + vw, 2 * qkw + 2 * vw
    logits = _dot_hi(z[:, c_lr:c_lr + LANES], wg_ref[...]) + bg_ref[...]
    log_a = _log_sigmoid(logits) * (1.0 / GLA_TAU)
    lane = lax.broadcasted_iota(jnp.int32, (1, qkw), 1)
    masks = [((lane >= h * GLA_DK) & (lane < (h + 1) * GLA_DK)).astype(F32) for h in range(N_GLA)]
    r = lax.broadcasted_iota(jnp.int32, (L, L), 0)
    c = lax.broadcasted_iota(jnp.int32, (L, L), 1)
    causal = c <= r
    tril = jnp.where(causal, 1.0, 0.0).astype(BF16)
    ng = ng_ref[...]
    for ci in range(nchunk):
        rs = slice(ci * L, (ci + 1) * L)
        ro = slice(row0 + ci * L, row0 + (ci + 1) * L)
        b = _dot_exact_lhs(tril, log_a[rs])
        b_last = b[L - 1:L, :]
        q = z[rs, 0:qkw] * (GLA_DK ** -0.5)
        k = z[rs, c_k:c_k + qkw]
        qt = q * jnp.exp(b)
        kt = (k * jnp.exp(-b)).astype(BF16)
        kd = (k * jnp.exp(b_last - b)).astype(BF16)
        qstack = jnp.concatenate([qt * masks[h] for h in range(N_GLA)], axis=0).astype(BF16)
        att = _dotg(qstack, kt, _NT)
        inter = _dotg(qstack, st.astype(BF16), _NT)
        vb = z[rs, c_v:c_v + vw].astype(BF16)
        for h in range(N_GLA):
            hs = slice(h * L, (h + 1) * L)
            vs = slice(h * GLA_DV, (h + 1) * GLA_DV)
            a_h = jnp.where(causal, att[hs], 0.0).astype(BF16)
            o_h = _rms_norm(_dot(a_h, vb[:, vs]) + inter[hs], ng)
            g = z[rs, c_o + h * GLA_DV:c_o + (h + 1) * GLA_DV]
            o_ref[ro, vs] = (o_h * (g * _sigmoid(g))).astype(BF16)
        upd = _dotg(vb, kd, _TN)
        new = st * jnp.exp(b_last)
        for h in range(N_GLA):
            new = new + upd[h * GLA_DV:(h + 1) * GLA_DV] * masks[h]
        st = new
    return st


def _gla_kernel(x_ref, xn_ref, w_ref, wg_ref, bg_ref, ng_ref, o_ref, ws, za, zb, st_ref, *, tm):
    first = pl.program_id(1) == 0

    @pl.when((pl.program_id(0) == 0) & first)
    def _():
        ws[...] = w_ref[0].astype(BF16)

    @pl.when(first)
    def _():
        st_ref[...] = jnp.zeros_like(st_ref)
        za[...] = _dotg(x_ref[0:tm, :], ws[...], _NT)

    nchunk = tm // GLA_CHUNK
    zb[...] = _dotg(x_ref[tm:2 * tm, :], ws[...], _NT)
    st = _gla_tile(za, o_ref, 0, st_ref[...], wg_ref, bg_ref, ng_ref, nchunk)
    za[...] = _dotg(xn_ref[...], ws[...], _NT)
    st_ref[...] = _gla_tile(zb, o_ref, tm, st, wg_ref, bg_ref, ng_ref, nchunk)


def _gla(hb, wt, l, row0, wg, bg, ng, batch, seq, tm=256):
    n, d = hb.shape
    ns2 = seq // (2 * tm)
    qkw = N_GLA * GLA_DK
    vw = N_GLA * GLA_DV
    zw = 2 * qkw + 2 * vw + LANES
    full = lambda a: pl.BlockSpec(a.shape, lambda b, s: (0,) * a.ndim)
    bg = bg.reshape(1, qkw)
    ng = ng.reshape(1, GLA_DV)
    last = 2 * ns2 - 1
    return pl.pallas_call(
        functools.partial(_gla_kernel, tm=tm), grid=(batch, ns2),
        in_specs=[pl.BlockSpec((2 * tm, d), lambda b, s: (b * ns2 + s, 0)),
                  pl.BlockSpec((tm, d), lambda b, s: (b * 2 * ns2 + jnp.minimum(2 * s + 2, last), 0)),
                  pl.BlockSpec((pl.Element(1), pl.Element(zw), pl.Element(d)), lambda b, s: (l, row0, 0),
                               pipeline_mode=pl.Buffered(1)),
                  full(wg), full(bg), full(ng)],
        out_specs=pl.BlockSpec((2 * tm, vw), lambda b, s: (b * ns2 + s, 0)),
        out_shape=jax.ShapeDtypeStruct((n, vw), BF16),
        scratch_shapes=[pltpu.VMEM((zw, d), BF16), pltpu.VMEM((tm, zw), F32), pltpu.VMEM((tm, zw), F32),
                        pltpu.VMEM((GLA_DV, qkw), F32)],
        compiler_params=_cparams("arbitrary", "arbitrary"), name="gla")(hb, hb, wt, wg, bg, ng)


def _mlstm_tile(z, o_ref, row0, carry, cw_ref, cb_ref, gb_ref, ng_ref, nchunk):
    L = ML_CHUNK
    qkw = N_ML * ML_DK
    vw = N_ML * ML_DV
    tm = nchunk * L
    c_v, c_o, c_if = 2 * qkw, 2 * qkw + vw, 2 * qkw + 2 * vw
    ct, nrow, mwide, mk, tail = carry

    x = z[:, 0:2 * qkw]
    row8 = lax.broadcasted_iota(jnp.int32, (SUBLANES, 2 * qkw), 0)
    acc = x * cw_ref[3:4, :] + cb_ref[...]
    for j in range(1, 4):
        rx = pltpu.roll(x, j, 0)
        fix = jnp.where(row8 < j, pltpu.roll(tail, j, 0), rx[0:SUBLANES])
        acc = acc + jnp.concatenate([fix, rx[SUBLANES:]], axis=0) * cw_ref[3 - j:4 - j, :]
    tail = x[tm - SUBLANES:tm]
    y = acc * _sigmoid(acc)
    q = y[:, :qkw] * (ML_DK ** -0.5)
    k = y[:, qkw:]

    gates = z[:, c_if:c_if + LANES] + gb_ref[...]
    fc = _log_sigmoid(gates)
    gt = gates.T[0:2 * N_ML]
    fct = _log_sigmoid(gt)

    lane = lax.broadcasted_iota(jnp.int32, (1, qkw), 1)
    masks = [((lane >= h * ML_DK) & (lane < (h + 1) * ML_DK)).astype(F32) for h in range(N_ML)]
    r = lax.broadcasted_iota(jnp.int32, (L, L), 0)
    c = lax.broadcasted_iota(jnp.int32, (L, L), 1)
    causal = c <= r
    tril = jnp.where(causal, 1.0, 0.0).astype(BF16)
    triu = jnp.where(r <= c, 1.0, 0.0).astype(BF16)

    def selector(width, block, first):
        rr = lax.broadcasted_iota(jnp.int32, (LANES, width), 0)
        cc = lax.broadcasted_iota(jnp.int32, (LANES, width), 1)
        return jnp.where(rr == (cc >> int(math.log2(block))) + first, 1.0, 0.0).astype(BF16)

    wide = N_ML * L
    fcb = _dot_exact_rhs(fc, selector(wide, L, N_ML))
    icb = _dot_exact_rhs(gates, selector(wide, L, 0))
    fck = _dot_exact_rhs(fc, selector(qkw, ML_DK, N_ML))
    ick = _dot_exact_rhs(gates, selector(qkw, ML_DK, 0))

    ng = ng_ref[...]

    for ci in range(nchunk):
        rs = slice(ci * L, (ci + 1) * L)
        ro = slice(row0 + ci * L, row0 + (ci + 1) * L)
        bb = _dot_exact_lhs(tril, fcb[rs])
        bk = _dot_exact_lhs(tril, fck[rs])
        brow = _dot_exact_rhs(fct[:, rs], triu)
        rowterm = gt[:, rs] - pltpu.roll(brow, N_ML, 0)
        qc = q[rs]
        kc = k[rs]
        log_d = jnp.concatenate(
            [jnp.where(causal, bb[:, h * L:(h + 1) * L] + rowterm[h:h + 1, :], -jnp.inf) for h in range(N_ML)],
            axis=0)
        log_inter = jnp.concatenate(
            [bb[:, h * L:(h + 1) * L] + mwide[:, h * L:(h + 1) * L] for h in range(N_ML)], axis=0)
        m_t = jnp.maximum(log_inter, jnp.max(log_d, axis=-1, keepdims=True))
        w_inter = jnp.exp(log_inter - m_t)
        qst = jnp.concatenate([qc * masks[h] for h in range(N_ML)], axis=0)
        qsb = qst.astype(BF16)
        s_all = _dotg(qsb, kc.astype(BF16), _NT) * jnp.exp(log_d - m_t)
        sb = s_all.astype(BF16)
        vb = z[rs, c_v:c_v + vw].astype(BF16)
        num = jnp.concatenate(
            [_dot(sb[h * L:(h + 1) * L], vb[:, h * ML_DV:(h + 1) * ML_DV]) for h in range(N_ML)], axis=0)
        num = num + w_inter * _dotg(qsb, ct.astype(BF16), _NT)
        den = jnp.sum(s_all, axis=-1, keepdims=True) + w_inter * jnp.sum(qst * nrow, axis=-1, keepdims=True)
        hh = _rms_norm(num / jnp.maximum(jnp.abs(den), jnp.exp(-m_t)), ng)
        for h in range(N_ML):
            vs = slice(h * ML_DV, (h + 1) * ML_DV)
            gate_o = _sigmoid(z[rs, c_o + h * ML_DV:c_o + (h + 1) * ML_DV])
            o_ref[ro, vs] = (hh[h * L:(h + 1) * L] * gate_o).astype(BF16)
        bl_w = bb[L - 1:L, :]
        lw_w = bl_w - bb + icb[rs]
        mwide_new = jnp.maximum(bl_w + mwide, jnp.max(lw_w, axis=0, keepdims=True))
        bl_k = bk[L - 1:L, :]
        lw_k = bl_k - bk + ick[rs]
        mk_new = jnp.maximum(bl_k + mk, jnp.max(lw_k, axis=0, keepdims=True))
        decay = jnp.exp(bl_k + mk - mk_new)
        kw = kc * jnp.exp(lw_k - mk_new)
        upd = _dotg(vb, kw.astype(BF16), _TN)
        ct = ct * decay
        for h in range(N_ML):
            ct = ct + upd[h * ML_DV:(h + 1) * ML_DV] * masks[h]
        nrow = nrow * decay + jnp.sum(kw, axis=0, keepdims=True)
        mwide = mwide_new
        mk = mk_new

    return ct, nrow, mwide, mk, tail


def _mlstm_kernel(x_ref, xn_ref, w_ref, cw_ref, cb_ref, gb_ref, ng_ref, o_ref,
                  ws, za, zb, ct_ref, n_ref, mw_ref, mk_ref, tail_ref, *, tm):
    first = pl.program_id(1) == 0

    @pl.when((pl.program_id(0) == 0) & first)
    def _():
        ws[...] = w_ref[0].astype(BF16)

    @pl.when(first)
    def _():
        for ref in (ct_ref, n_ref, mw_ref, mk_ref, tail_ref):
            ref[...] = jnp.zeros_like(ref)
        za[...] = _dotg(x_ref[0:tm, :], ws[...], _NT)

    nchunk = tm // ML_CHUNK
    consts = (cw_ref, cb_ref, gb_ref, ng_ref, nchunk)
    zb[...] = _dotg(x_ref[tm:2 * tm, :], ws[...], _NT)
    carry = (ct_ref[...], n_ref[...], mw_ref[...], mk_ref[...], tail_ref[...])
    carry = _mlstm_tile(za, o_ref, 0, carry, *consts)
    za[...] = _dotg(xn_ref[...], ws[...], _NT)
    carry = _mlstm_tile(zb, o_ref, tm, carry, *consts)
    for ref, val in zip((ct_ref, n_ref, mw_ref, mk_ref, tail_ref), carry):
        ref[...] = val


def _mlstm(hb, wt, l, row0, cw, cb, gb, ng, batch, seq, tm=512):
    n, d = hb.shape
    ns2 = seq // (2 * tm)
    qkw = N_ML * ML_DK
    vw = N_ML * ML_DV
    zw = 2 * qkw + 2 * vw + LANES
    full = lambda a: pl.BlockSpec(a.shape, lambda b, s: (0,) * a.ndim)
    cb = cb.reshape(1, 2 * qkw)
    gbp = jnp.pad(gb, (0, LANES - 2 * N_ML)).reshape(1, LANES)
    ng = ng.reshape(1, ML_DV)
    last = 2 * ns2 - 1
    return pl.pallas_call(
        functools.partial(_mlstm_kernel, tm=tm), grid=(batch, ns2),
        in_specs=[pl.BlockSpec((2 * tm, d), lambda b, s: (b * ns2 + s, 0)),
                  pl.BlockSpec((tm, d), lambda b, s: (b * 2 * ns2 + jnp.minimum(2 * s + 2, last), 0)),
                  pl.BlockSpec((pl.Element(1), pl.Element(zw), pl.Element(d)), lambda b, s: (l, row0, 0),
                               pipeline_mode=pl.Buffered(1)),
                  full(cw), full(cb), full(gbp), full(ng)],
        out_specs=pl.BlockSpec((2 * tm, vw), lambda b, s: (b * ns2 + s, 0)),
        out_shape=jax.ShapeDtypeStruct((n, vw), BF16),
        scratch_shapes=[pltpu.VMEM((zw, d), BF16), pltpu.VMEM((tm, zw), F32), pltpu.VMEM((tm, zw), F32),
                        pltpu.VMEM((ML_DV, qkw), F32), pltpu.VMEM((1, qkw), F32),
                        pltpu.VMEM((1, N_ML * ML_CHUNK), F32), pltpu.VMEM((1, qkw), F32),
                        pltpu.VMEM((SUBLANES, 2 * qkw), F32)],
        compiler_params=_cparams("arbitrary", "arbitrary", vmem=VMEM_LIMIT_BIG), name="mlstm")(
            hb, hb, wt, cw, cb, gbp, ng)


def _merge_out_kernel(ya_ref, yb_ref, yc_ref, yd_ref, g_ref, wb_ref, wo_ref, h_ref, lg_ref, lb_ref,
                      of_ref, ob_ref):
    acc = None
    for i, y_ref in enumerate((ya_ref, yb_ref, yc_ref, yd_ref)):
        p = _dot(y_ref[...], wb_ref[i])
        t = g_ref[:, i * D_MODEL:(i + 1) * D_MODEL].astype(F32) * p
        acc = t if acc is None else acc + t
    y = _dot(acc.astype(BF16), wo_ref[...])
    o = _layer_norm(ALPHA * h_ref[...] + y, lg_ref[...], lb_ref[...])
    of_ref[...] = o
    ob_ref[...] = o.astype(BF16)


def _merge_out(ya, yb, yc, yd, gates, wb_all, wo_all, l, hf, lg, lb, tm=256):
    n, bw = ya.shape
    d = hf.shape[1]
    row = pl.BlockSpec((tm, bw), lambda i: (i, 0))
    hrow = pl.BlockSpec((tm, d), lambda i: (i, 0))
    par = pl.BlockSpec((1, d), lambda i: (0, 0))
    once = dict(pipeline_mode=pl.Buffered(1))
    return pl.pallas_call(
        _merge_out_kernel, grid=(n // tm,),
        in_specs=[row, row, row, row, pl.BlockSpec((tm, 4 * d), lambda i: (i, 0)),
                  pl.BlockSpec((None,) + wb_all.shape[1:], lambda i: (l, 0, 0, 0), **once),
                  pl.BlockSpec((None,) + wo_all.shape[1:], lambda i: (l, 0, 0), **once), hrow, par, par],
        out_specs=[hrow, hrow],
        out_shape=[jax.ShapeDtypeStruct((n, d), F32), jax.ShapeDtypeStruct((n, d), BF16)],
        compiler_params=_cparams("parallel"), name="merge_out_ln1")(
            ya, yb, yc, yd, gates, wb_all, wo_all, hf, lg.reshape(1, d), lb.reshape(1, d))


def _xattn_kernel(hb_ref, hf_ref, wq_ref, k_ref, v_ref, wo_ref, g_ref, b_ref, wrt_ref, rbc_ref,
                  of_ref, pay_ref, dest_ref, cnt_ref, carry_ref, *, region):
    q = (_dot(hb_ref[...], wq_ref[...]) * (X_HEAD ** -0.5)).astype(BF16)
    outs = []
    for h in range(N_X):
        hs = slice(h * X_HEAD, (h + 1) * X_HEAD)
        s = _dotg(q[:, hs], k_ref[:, hs], _NT)
        p = jnp.exp(s - jnp.max(s, axis=-1, keepdims=True))
        l = jnp.sum(p, axis=-1, keepdims=True)
        outs.append((_dot(p.astype(BF16), v_ref[:, hs]) / l).astype(BF16))
    y = _dot(jnp.concatenate(outs, axis=-1), wo_ref[...])
    o = _layer_norm(ALPHA * hf_ref[...] + y, g_ref[...], b_ref[...])
    of_ref[...] = o
    is_first = (pl.program_id(0) == 0) & (pl.program_id(1) == 0)
    _route_tile(o, is_first, wrt_ref, rbc_ref, pay_ref, dest_ref, cnt_ref, carry_ref, region)


def _xattn_route(hb, hf, wq_all, kv, wo_all, l, g, b, wrt, rbc, region, batch, seq, mem_len, tm=512):
    n, d = hf.shape
    ns = seq // tm
    xw = N_X * X_HEAD
    tile = lambda bb, s: bb * ns + s
    row = pl.BlockSpec((tm, d), lambda bb, s: (tile(bb, s), 0))
    par = pl.BlockSpec((1, d), lambda bb, s: (0, 0))
    return pl.pallas_call(
        functools.partial(_xattn_kernel, region=region), grid=(batch, ns),
        in_specs=[row, row, pl.BlockSpec((None, d, xw), lambda bb, s: (l, 0, 0)),
                  pl.BlockSpec((mem_len, xw), lambda bb, s: (bb, 0)),
                  pl.BlockSpec((mem_len, xw), lambda bb, s: (bb, 1)),
                  pl.BlockSpec((None, xw, d), lambda bb, s: (l, 0, 0)), par, par,
                  pl.BlockSpec((LANES, d), lambda bb, s: (0, 0)),
                  pl.BlockSpec((N_EXPERTS, 1), lambda bb, s: (0, 0))],
        out_specs=[row, pl.BlockSpec((tm, PAY_W), lambda bb, s: (tile(bb, s), 0)),
                   pl.BlockSpec((1, tm), lambda bb, s: (0, tile(bb, s))),
                   pl.BlockSpec((SUBLANES, LANES), lambda bb, s: (0, 0))],
        out_shape=[jax.ShapeDtypeStruct((n, d), F32), jax.ShapeDtypeStruct((n, PAY_W), U32),
                   jax.ShapeDtypeStruct((1, n), jnp.int32), jax.ShapeDtypeStruct((SUBLANES, LANES), jnp.int32)],
        scratch_shapes=[pltpu.VMEM((SUBLANES, LANES), F32)],
        compiler_params=_cparams("arbitrary", "arbitrary"), name="xattn_route")(
            hb, hf, wq_all, kv, kv, wo_all, g.reshape(1, d), b.reshape(1, d), wrt, rbc)


N_GROUPS = 4
PER_GROUP = N_EXPERTS // N_GROUPS
MOE_TILE = 512
MOE_DOWN_TN = 1024
PAY_X = D_MODEL // 2
PAY_W = PAY_X + LANES
U32 = jnp.uint32
_HI16 = 0xFFFF0000


def _pack_bf16_pairs(x):
    u = lax.bitcast_convert_type(x.astype(BF16).astype(F32), U32)
    w = x.shape[1] // 2
    return (u[:, :w] >> 16) | (u[:, w:] & jnp.uint32(_HI16))


def _unpack_pairs_f32(words):
    lo = lax.bitcast_convert_type(words << 16, F32)
    hi = lax.bitcast_convert_type(words & jnp.uint32(_HI16), F32)
    return jnp.concatenate([lo, hi], axis=1)


def _unpack_bf16_pairs(words):
    return _unpack_pairs_f32(words).astype(BF16)


def _route_tile(h, is_first, wrt_ref, rbc_ref, pay_ref, dest_ref, cnt_ref, carry_ref, region):
    tm = h.shape[0]

    @pl.when(is_first)
    def _():
        carry_ref[...] = jnp.zeros_like(carry_ref)

    w = wrt_ref[...]
    w1 = w.astype(BF16)
    w2 = (w - w1.astype(F32)).astype(BF16)
    h1 = h.astype(BF16)
    h2 = (h - h1.astype(F32)).astype(BF16)
    logits = _dotg(w1, h1, _NT) + _dotg(w2, h1, _NT) + _dotg(w1, h2, _NT)
    aff = _sigmoid(logits[0:N_EXPERTS])
    biased = aff + rbc_ref[...]
    row = lax.broadcasted_iota(jnp.int32, (N_EXPERTS, 1), 0).astype(F32)
    big = float(LANES)
    best = e1 = e2 = None
    for g in range(N_GROUPS):
        x = jnp.where((row >= g * PER_GROUP) & (row < (g + 1) * PER_GROUP), biased, -jnp.inf)
        m1 = jnp.max(x, axis=0, keepdims=True)
        i1 = jnp.min(jnp.where(x == m1, row, big), axis=0, keepdims=True)
        x2 = jnp.where(row == i1, -jnp.inf, x)
        m2 = jnp.max(x2, axis=0, keepdims=True)
        i2 = jnp.min(jnp.where(x2 == m2, row, big), axis=0, keepdims=True)
        score = m1 + m2
        if g == 0:
            best, e1, e2 = score, i1, i2
        else:
            better = score > best
            best = jnp.where(better, score, best)
            e1 = jnp.where(better, i1, e1)
            e2 = jnp.where(better, i2, e2)
    s1 = jnp.sum(jnp.where(row == e1, aff, 0.0), axis=0, keepdims=True)
    s2 = jnp.sum(jnp.where(row == e2, aff, 0.0), axis=0, keepdims=True)
    tot = s1 + s2
    gates_t = jnp.where(row == e1, s1 / tot, 0.0) + jnp.where(row == e2, s2 / tot, 0.0)

    grp = jnp.zeros_like(e1)
    for g in range(1, N_GROUPS):
        grp = grp + jnp.where(e1 >= g * PER_GROUP, 1.0, 0.0)
    row8 = lax.broadcasted_iota(jnp.int32, (SUBLANES, 1), 0).astype(F32)
    onehot = jnp.where(row8 == grp, 1.0, 0.0)
    r = lax.broadcasted_iota(jnp.int32, (tm, tm), 0)
    c = lax.broadcasted_iota(jnp.int32, (tm, tm), 1)
    earlier = jnp.where(r < c, 1.0, 0.0).astype(BF16)
    rank_in = _dot(onehot.astype(BF16), earlier)
    carry = carry_ref[...]
    rank = jnp.sum(onehot * (rank_in + carry[:, 0:1]), axis=0, keepdims=True)
    dest_ref[...] = (grp * float(region) + rank).astype(jnp.int32)
    carry = carry + jnp.sum(onehot, axis=1, keepdims=True)
    carry_ref[...] = carry
    cnt_ref[...] = carry.astype(jnp.int32)

    gates = jnp.concatenate([gates_t, jnp.zeros((LANES - N_EXPERTS, tm), F32)], axis=0).T
    pay_ref[:, :PAY_X] = _pack_bf16_pairs(h)
    pay_ref[:, PAY_X:] = lax.bitcast_convert_type(gates, U32)


def _scatter_kernel(dest_ref, cnt_ref, pay_ref, out_ref, buf, zbuf, sem, zsem, *, region, nsteps):
    tm = pay_ref.shape[0]
    i = pl.program_id(0)
    slot = i % 2

    def wait_rows(s):
        pltpu.make_async_copy(buf.at[s], out_ref.at[pl.ds(0, tm)], sem.at[s]).wait()

    @pl.when(i == 0)
    def _():
        zbuf[...] = jnp.zeros_like(zbuf)
        copies = []
        for g in range(N_GROUPS):
            start = pl.multiple_of(g * region + (cnt_ref[g] // MOE_TILE) * MOE_TILE, MOE_TILE)
            copies.append(pltpu.make_async_copy(zbuf, out_ref.at[pl.ds(start, MOE_TILE)], zsem))
        for cp in copies:
            cp.start()
        for cp in copies:
            cp.wait()

    @pl.when(i >= 2)
    def _():
        wait_rows(slot)

    buf[slot] = pay_ref[...]

    base = i * tm
    for r in range(tm):
        d = dest_ref[base + r]
        pltpu.make_async_copy(buf.at[slot, pl.ds(r, 1)], out_ref.at[pl.ds(d, 1)],
                              sem.at[slot]).start(priority=r % 2)

    @pl.when(i == nsteps - 1)
    def _():
        wait_rows(slot)
        if nsteps >= 2:
            wait_rows(1 - slot)


def _scatter(dest, cnt, pay, region, tm=512):
    n = pay.shape[0]
    nsteps = n // tm
    return pl.pallas_call(
        functools.partial(_scatter_kernel, region=region, nsteps=nsteps),
        grid_spec=pltpu.PrefetchScalarGridSpec(
            num_scalar_prefetch=2, grid=(nsteps,),
            in_specs=[pl.BlockSpec((tm, PAY_W), lambda i, d, c: (i, 0))],
            out_specs=pl.BlockSpec(memory_space=pl.ANY),
            scratch_shapes=[pltpu.VMEM((2, tm, PAY_W), U32), pltpu.VMEM((MOE_TILE, PAY_W), U32),
                            pltpu.SemaphoreType.DMA((2,)), pltpu.SemaphoreType.DMA(())]),
        out_shape=jax.ShapeDtypeStruct((N_GROUPS * region, PAY_W), U32),
        compiler_params=_cparams("arbitrary"), name="moe_scatter")(dest, cnt, pay)


def _tile_tables(cnt, region, ntiles):
    nt = (cnt + MOE_TILE - 1) // MOE_TILE
    ends = jnp.cumsum(nt)
    starts = ends - nt
    total = ends[-1]
    i = jnp.minimum(jnp.arange(ntiles, dtype=jnp.int32), total - 1)
    g = jnp.sum((i[:, None] >= ends[None, :]).astype(jnp.int32), axis=1)
    blk = g * (region // MOE_TILE) + i - starts[g]
    return g.astype(jnp.int32), blk.astype(jnp.int32), total.reshape(1).astype(jnp.int32)


def _moe_up_kernel(tg_ref, tb_ref, nt_ref, x_ref, wg_ref, wu_ref, o_ref, wgs, wus):
    e = pl.program_id(0)
    i = pl.program_id(1)
    grp = tg_ref[i]
    changed = (i == 0) | (grp != tg_ref[jnp.maximum(i - 1, 0)])

    @pl.when(changed)
    def _():
        wgs[...] = wg_ref[...].astype(BF16)
        wus[...] = wu_ref[...].astype(BF16)

    @pl.when(i < nt_ref[0])
    def _():
        x = _unpack_bf16_pairs(x_ref[:, :PAY_X])
        gates = lax.bitcast_convert_type(x_ref[:, PAY_X:], F32)
        lane = lax.broadcasted_iota(jnp.int32, (1, LANES), 1)
        ge = jnp.sum(jnp.where(lane == grp * PER_GROUP + e, gates, 0.0), axis=-1, keepdims=True)
        a = _dot(x, wgs[...])
        u = _dot(x, wus[...])
        o_ref[...] = (a * _sigmoid(a) * u * ge).astype(BF16)


def _moe_up(tg, tb, nt, xs, wg, wu, l):
    rows = xs.shape[0]
    ntiles = tg.shape[0]
    d, de = wg.shape[2], wg.shape[3]
    wspec = pl.BlockSpec((None, None, d, de), lambda e, i, tg, tb, nt: (l, tg[i] * PER_GROUP + e, 0, 0))
    return pl.pallas_call(
        _moe_up_kernel,
        grid_spec=pltpu.PrefetchScalarGridSpec(
            num_scalar_prefetch=3, grid=(PER_GROUP, ntiles),
            in_specs=[pl.BlockSpec((MOE_TILE, PAY_W), lambda e, i, tg, tb, nt: (tb[i], 0)), wspec, wspec],
            out_specs=pl.BlockSpec((MOE_TILE, de), lambda e, i, tg, tb, nt: (tb[i], e)),
            scratch_shapes=[pltpu.VMEM((d, de), BF16), pltpu.VMEM((d, de), BF16)]),
        out_shape=jax.ShapeDtypeStruct((rows, PER_GROUP * de), BF16),
        compiler_params=_cparams("arbitrary", "arbitrary"), name="moe_up")(tg, tb, nt, xs, wg, wu)


def _moe_down_kernel(tg_ref, tb_ref, nt_ref, x_ref, w_ref, o_ref, ws):
    i = pl.program_id(1)
    changed = (i == 0) | (tg_ref[i] != tg_ref[jnp.maximum(i - 1, 0)])

    @pl.when(changed)
    def _():
        ws[...] = w_ref[...].astype(BF16)

    @pl.when(i < nt_ref[0])
    def _():
        o_ref[...] = _pack_bf16_pairs(_dot(x_ref[...], ws[...]))


def _moe_down(tg, tb, nt, hid, wd, l, tn=MOE_DOWN_TN):
    rows, k = hid.shape
    ntiles = tg.shape[0]
    d = wd.shape[3]
    return pl.pallas_call(
        _moe_down_kernel,
        grid_spec=pltpu.PrefetchScalarGridSpec(
            num_scalar_prefetch=3, grid=(d // tn, ntiles),
            in_specs=[pl.BlockSpec((MOE_TILE, k), lambda c, i, tg, tb, nt: (tb[i], 0)),
                      pl.BlockSpec((None, None, k, tn), lambda c, i, tg, tb, nt: (l, tg[i], 0, c))],
            out_specs=pl.BlockSpec((MOE_TILE, tn // 2), lambda c, i, tg, tb, nt: (tb[i], c)),
            scratch_shapes=[pltpu.VMEM((k, tn), BF16)]),
        out_shape=jax.ShapeDtypeStruct((rows, d // 2), U32),
        compiler_params=_cparams("arbitrary", "arbitrary"), name="moe_down")(tg, tb, nt, hid, wd)


def _gather_ln_kernel(dest_ref, y_ref, h_ref, g_ref, b_ref, of_ref, ob_ref, buf, sem, *, nsteps):
    tm = h_ref.shape[0]
    i = pl.program_id(0)

    def issue(step, slot):
        base = step * tm
        for r in range(tm):
            d = dest_ref[base + r]
            pltpu.make_async_copy(y_ref.at[pl.ds(d, 1)], buf.at[slot, pl.ds(r, 1)],
                                  sem.at[slot]).start(priority=r % 2)

    @pl.when(i == 0)
    def _():
        issue(0, 0)

    @pl.when(i + 1 < nsteps)
    def _():
        issue(i + 1, (i + 1) % 2)

    slot = i % 2
    pltpu.make_async_copy(y_ref.at[pl.ds(0, tm)], buf.at[slot], sem.at[slot]).wait()
    half = MOE_DOWN_TN // 2
    words = buf[slot]
    y = jnp.concatenate([_unpack_pairs_f32(words[:, c * half:(c + 1) * half])
                         for c in range(words.shape[1] // half)], axis=1)
    o = _layer_norm(ALPHA * h_ref[...] + y, g_ref[...], b_ref[...])
    of_ref[...] = o
    ob_ref[...] = o.astype(BF16)


def _gather_ln(dest, ys, hf, g, b, tm=512):
    n, d = hf.shape
    nsteps = n // tm
    row = pl.BlockSpec((tm, d), lambda i, dref: (i, 0))
    par = pl.BlockSpec((1, d), lambda i, dref: (0, 0))
    return pl.pallas_call(
        functools.partial(_gather_ln_kernel, nsteps=nsteps),
        grid_spec=pltpu.PrefetchScalarGridSpec(
            num_scalar_prefetch=1, grid=(nsteps,),
            in_specs=[pl.BlockSpec(memory_space=pl.ANY), row, par, par],
            out_specs=[row, row],
            scratch_shapes=[pltpu.VMEM((2, tm, d // 2), U32), pltpu.SemaphoreType.DMA((2,))]),
        out_shape=[jax.ShapeDtypeStruct((n, d), F32), jax.ShapeDtypeStruct((n, d), BF16)],
        compiler_params=_cparams("arbitrary"), name="moe_gather_ln3")(
            dest, ys, hf, g.reshape(1, d), b.reshape(1, d))


def _moe_region(n):
    return n + MOE_TILE


def _moe(hf, pay, dest, cnt, p, w, l):
    n = hf.shape[0]
    region = _moe_region(n)
    ntiles = n // MOE_TILE + N_GROUPS
    dest = dest.reshape(n)
    cnt = cnt[:N_GROUPS, 0]
    tg, tb, nt = _tile_tables(cnt, region, ntiles)
    xs = _scatter(dest, cnt, pay, region)
    hid = _moe_up(tg, tb, nt, xs, p["moe_w_gate"], p["moe_w_up"], l)
    ys = _moe_down(tg, tb, nt, hid, w["moe_wd"], l)
    return _gather_ln(dest, ys, hf, p["ln3_g"][l], p["ln3_b"][l])


def _prep_params(p):
    L = p["w_in"].shape[0]
    half = MLA_ROPE // 2
    pad_r = LANES - MLA_ROPE

    wq = p["mla_w_uq"].reshape(L, MLA_Q_RANK, N_MLA, MLA_NOPE + MLA_ROPE)
    rq = wq[..., MLA_NOPE:]
    zq = jnp.zeros((L, MLA_Q_RANK, N_MLA, pad_r), F32)
    wuq = jnp.concatenate([wq[..., :MLA_NOPE], rq, zq, rq[..., half:], rq[..., :half], zq], axis=-1)
    wkv = p["mla_w_ukv"].reshape(L, MLA_KV_RANK, N_MLA, MLA_NOPE + MLA_V)

    bf = lambda a: a.astype(BF16)
    ne = p["moe_w_down"].shape[1]
    return dict(
        wt_in=jnp.swapaxes(p["w_in"], 1, 2),
        wuq=bf(wuq.reshape(L, MLA_Q_RANK, N_MLA * 3 * LANES)),
        wuk=bf(wkv[..., :MLA_NOPE].reshape(L, MLA_KV_RANK, N_MLA * MLA_NOPE)),
        wuv=bf(wkv[..., MLA_NOPE:].reshape(L, MLA_KV_RANK, N_MLA * MLA_V)),
        gla_wg=jnp.pad(p["gla_w_gate"], ((0, 0), (0, LANES - GLA_RANK), (0, 0))),
        w_branch=bf(p["w_branch"]), w_out=bf(p["w_out"]),
        x_w_q=bf(p["x_w_q"]), x_w_kv=bf(p["x_w_kv"]), x_w_o=bf(p["x_w_o"]),
        w_router_t=jnp.pad(p["w_router"].T, ((0, LANES - N_EXPERTS), (0, 0))),
        router_bias_c=p["router_bias"].reshape(N_EXPERTS, 1),
        moe_wd=p["moe_w_down"].reshape(L, N_GROUPS, (ne // N_GROUPS) * D_EXPERT, D_MODEL),
    )


def _mixer(hf, hb, cos_p, sin_p, p, w, l, batch, seq):
    wt = w["wt_in"]
    o_mla = 2 * D_SG
    o_gla = o_mla + MLA_Q_RANK + MLA_KV_RANK + MLA_ROPE
    o_ml = o_gla + 2 * N_GLA * GLA_DK + 2 * N_GLA * GLA_DV + GLA_RANK
    o_gate = o_ml + 2 * N_ML * ML_DK + 2 * N_ML * ML_DV + 2 * N_ML
    gates = _mm_wt(hb, wt, l, o_gate, 4 * D_MODEL // 1024, 1024, BF16, act="sigmoid", name="mm_gate")

    y_a = _sg(hb, wt, l, 0, p["sg_vnorm_g"][l], p["sg_vnorm_b"][l], p["sg_w_s"][l], p["sg_b_s"][l])
    q, k, v = _mla_prep(hb, wt, l, o_mla, cos_p, sin_p, p["mla_qnorm_g"][l], p["mla_kvnorm_g"][l],
                        w["wuq"][l], w["wuk"][l], w["wuv"][l])
    y_b = _mla_attn(q, k, v, batch, seq)
    y_c = _gla(hb, wt, l, o_gla, w["gla_wg"][l], p["gla_b_gate"][l], p["gla_norm_g"][l], batch, seq)
    y_d = _mlstm(hb, wt, l, o_ml, p["ml_conv_w"][l], p["ml_conv_b"][l], p["ml_gate_b"][l], p["ml_norm_g"][l],
                 batch, seq)
    return _merge_out(y_a, y_b, y_c, y_d, gates, w["w_branch"], w["w_out"], l, hf,
                      p["ln1_g"][l], p["ln1_b"][l])


def _forward(p):
    x = p["x"]
    batch, seq, d = x.shape
    n = batch * seq
    mem = p["mem"]
    mem_len = mem.shape[1]
    w = _prep_params(p)
    posb = jnp.broadcast_to(p["positions"].reshape(n, 1).astype(F32), (n, LANES))
    cos_p, sin_p = _rope_tables(posb)
    memb = mem.reshape(batch * mem_len, d).astype(BF16)
    hf, hb = _ln(x.reshape(n, d), p["ln_in_g"], p["ln_in_b"])
    for l in range(p["w_in"].shape[0]):
        hf, hb = _mixer(hf, hb, cos_p, sin_p, p, w, l, batch, seq)
        kv = _mm(memb, w["x_w_kv"][l], BF16, tm=512, tn=1024, name="mm_xkv")
        hf, pay, dest, cnt = _xattn_route(hb, hf, w["x_w_q"], kv, w["x_w_o"], l, p["ln2_g"][l], p["ln2_b"][l],
                                          w["w_router_t"], w["router_bias_c"], _moe_region(n),
                                          batch, seq, mem_len)
        hf, hb = _moe(hf, pay, dest, cnt, p, w, l)
    return hf.reshape(batch, seq, d)


def kernel(x, mem, positions, ln_in_g, ln_in_b, w_in, sg_vnorm_g, sg_vnorm_b, sg_w_s, sg_b_s, mla_qnorm_g, mla_kvnorm_g, mla_w_uq, mla_w_ukv, gla_w_gate, gla_b_gate, gla_norm_g, ml_conv_w, ml_conv_b, ml_gate_b, ml_norm_g, w_branch, w_out, ln1_g, ln1_b, x_w_q, x_w_kv, x_w_o, ln2_g, ln2_b, w_router, router_bias, moe_w_gate, moe_w_up, moe_w_down, ln3_g, ln3_b):
    return _forward(dict(
        x=x, mem=mem, positions=positions, ln_in_g=ln_in_g, ln_in_b=ln_in_b, w_in=w_in,
        sg_vnorm_g=sg_vnorm_g, sg_vnorm_b=sg_vnorm_b, sg_w_s=sg_w_s, sg_b_s=sg_b_s,
        mla_qnorm_g=mla_qnorm_g, mla_kvnorm_g=mla_kvnorm_g, mla_w_uq=mla_w_uq, mla_w_ukv=mla_w_ukv,
        gla_w_gate=gla_w_gate, gla_b_gate=gla_b_gate, gla_norm_g=gla_norm_g,
        ml_conv_w=ml_conv_w, ml_conv_b=ml_conv_b, ml_gate_b=ml_gate_b, ml_norm_g=ml_norm_g,
        w_branch=w_branch, w_out=w_out, ln1_g=ln1_g, ln1_b=ln1_b,
        x_w_q=x_w_q, x_w_kv=x_w_kv, x_w_o=x_w_o, ln2_g=ln2_g, ln2_b=ln2_b,
        w_router=w_router, router_bias=router_bias,
        moe_w_gate=moe_w_gate, moe_w_up=moe_w_up, moe_w_down=moe_w_down, ln3_g=ln3_g, ln3_b=ln3_b))
```

```python
import functools
import math

import jax
import jax.numpy as jnp
from jax import lax
from jax.experimental import pallas as pl
from jax.experimental.pallas import tpu as pltpu

F32 = jnp.float32
BF16 = jnp.bfloat16

D_MODEL = 2048
DEPTH = 4
EPS = 1e-5
ALPHA = (2.0 * DEPTH) ** 0.25

SG_CHUNK = 128
N_SG = 4
D_SG = 512
N_MLA = 4
MLA_Q_RANK = 384
MLA_KV_RANK = 256
MLA_NOPE = 128
MLA_ROPE = 64
MLA_V = 128
MLA_QK = 256
ROPE_BASE = 10000.0
N_GLA = 4
GLA_DK = 64
GLA_DV = 128
GLA_RANK = 16
GLA_TAU = 16.0
GLA_CHUNK = 64
N_ML = 4
ML_DK = 64
ML_DV = 128
ML_CHUNK = 128
N_X = 4
X_HEAD = 128
N_EXPERTS = 16
D_EXPERT = 512

LANES = 128
SUBLANES = 8
VMEM_LIMIT = 48 * 1024 * 1024

_NT = (((1,), (1,)), ((), ()))
_TN = (((0,), (0,)), ((), ()))


def _cparams(*sem, vmem=VMEM_LIMIT):
    return pltpu.CompilerParams(dimension_semantics=sem, vmem_limit_bytes=vmem)


def _dot(a, b):
    return jnp.dot(a, b, preferred_element_type=F32)


def _dotg(a, b, dims):
    return lax.dot_general(a, b, dims, preferred_element_type=F32)


def _split3(a):
    a1 = a.astype(BF16)
    r1 = a - a1.astype(F32)
    a2 = r1.astype(BF16)
    a3 = (r1 - a2.astype(F32)).astype(BF16)
    return a1, a2, a3


def _dot_exact_rhs(a, ones_bf16):
    a1, a2, a3 = _split3(a)
    return _dot(a1, ones_bf16) + _dot(a2, ones_bf16) + _dot(a3, ones_bf16)


def _dot_exact_lhs(ones_bf16, a):
    a1, a2, a3 = _split3(a)
    return _dot(ones_bf16, a1) + _dot(ones_bf16, a2) + _dot(ones_bf16, a3)


def _dot_hi(a, b):
    a1 = a.astype(BF16)
    a2 = (a - a1.astype(F32)).astype(BF16)
    b1 = b.astype(BF16)
    b2 = (b - b1.astype(F32)).astype(BF16)
    return _dot(a1, b1) + _dot(a2, b1) + _dot(a1, b2)


def _sigmoid(x):
    return 1.0 / (1.0 + jnp.exp(-x))


def _log_sigmoid(x):
    return jnp.minimum(x, 0.0) - jnp.log(1.0 + jnp.exp(-jnp.abs(x)))


def _layer_norm(t, g, b):
    mu = jnp.mean(t, axis=-1, keepdims=True)
    c = t - mu
    var = jnp.mean(c * c, axis=-1, keepdims=True)
    return c * lax.rsqrt(var + EPS) * g + b


def _rms_norm(t, g):
    return t * lax.rsqrt(jnp.mean(t * t, axis=-1, keepdims=True) + EPS) * g


def _ln_kernel(x_ref, g_ref, b_ref, of_ref, ob_ref):
    y = _layer_norm(x_ref[...], g_ref[...], b_ref[...])
    of_ref[...] = y
    ob_ref[...] = y.astype(BF16)


def _ln(x, g, b, tm=256):
    n, d = x.shape
    row = pl.BlockSpec((tm, d), lambda i: (i, 0))
    par = pl.BlockSpec((1, d), lambda i: (0, 0))
    return pl.pallas_call(
        _ln_kernel, grid=(n // tm,), in_specs=[row, par, par], out_specs=[row, row],
        out_shape=[jax.ShapeDtypeStruct((n, d), F32), jax.ShapeDtypeStruct((n, d), BF16)],
        compiler_params=_cparams("parallel"), name="ln_in")(x, g.reshape(1, d), b.reshape(1, d))


def _mm_kernel(x_ref, w_ref, o_ref, *, act):
    acc = _dot(x_ref[...], w_ref[...])
    if act == "sigmoid":
        acc = _sigmoid(acc)
    o_ref[...] = acc.astype(o_ref.dtype)


def _mm(x, w, out_dtype, act=None, tm=512, tn=None, name="mm"):
    n, k = x.shape
    m = w.shape[1]
    tn = m if tn is None else tn
    tm = min(tm, n)
    assert n % tm == 0 and m % tn == 0
    return pl.pallas_call(
        functools.partial(_mm_kernel, act=act), grid=(n // tm, m // tn),
        in_specs=[pl.BlockSpec((tm, k), lambda i, j: (i, 0)),
                  pl.BlockSpec((k, tn), lambda i, j: (0, j))],
        out_specs=pl.BlockSpec((tm, tn), lambda i, j: (i, j)),
        out_shape=jax.ShapeDtypeStruct((n, m), out_dtype),
        compiler_params=_cparams("parallel", "parallel"), name=name)(x, w)


def _mm_wt_kernel(x_ref, w_ref, o_ref, ws, *, act):
    @pl.when(pl.program_id(1) == 0)
    def _():
        ws[...] = w_ref[0].astype(BF16)

    acc = _dotg(x_ref[...], ws[...], _NT)
    if act == "sigmoid":
        acc = _sigmoid(acc)
    o_ref[...] = acc.astype(o_ref.dtype)


def _mm_wt(x, wt_all, l, row0, nblk, tn, out_dtype, act=None, tm=1024, name="mm_wt"):
    n, k = x.shape
    assert row0 % SUBLANES == 0
    wmode = dict(pipeline_mode=pl.Buffered(1)) if nblk == 1 else {}
    return pl.pallas_call(
        functools.partial(_mm_wt_kernel, act=act), grid=(nblk, n // tm),
        in_specs=[pl.BlockSpec((tm, k), lambda j, i: (i, 0)),
                  pl.BlockSpec((pl.Element(1), pl.Element(tn), pl.Element(k)),
                               lambda j, i: (l, pl.multiple_of(row0 + j * tn, SUBLANES), 0), **wmode)],
        out_specs=pl.BlockSpec((tm, tn), lambda j, i: (i, j)),
        out_shape=jax.ShapeDtypeStruct((n, nblk * tn), out_dtype),
        scratch_shapes=[pltpu.VMEM((tn, k), BF16)],
        compiler_params=_cparams("arbitrary", "arbitrary"), name=name)(x, wt_all)


def _project_tile(x_ref, w_ref, wbf):
    @pl.when(pl.program_id(0) == 0)
    def _():
        wbf[...] = w_ref[0].astype(BF16)

    return _dotg(x_ref[...], wbf[...], _NT)


def _proj_specs(d, l, row0, width):
    wspec = pl.BlockSpec((pl.Element(1), pl.Element(width), pl.Element(d)), lambda i: (l, row0, 0),
                         pipeline_mode=pl.Buffered(1))
    return wspec, pltpu.VMEM((width, d), BF16)


def _sg_kernel(x_ref, w_ref, vg_ref, vb_ref, ws_ref, bst_ref, o_ref, wbf, *, nchunk):
    z = _project_tile(x_ref, w_ref, wbf)
    z = 0.5 * z * (1.0 + jnp.tanh(math.sqrt(2.0 / math.pi) * (z + 0.044715 * (z * z * z))))
    u = z[:, :D_SG]
    vn = _layer_norm(z[:, D_SG:], vg_ref[...], vb_ref[...]).astype(BF16)
    r = lax.broadcasted_iota(jnp.int32, (SG_CHUNK, SG_CHUNK), 0)
    c = lax.broadcasted_iota(jnp.int32, (SG_CHUNK, SG_CHUNK), 1)
    causal = c <= r
    gw = SG_CHUNK
    for g in range(N_SG):
        w = jnp.where(causal, ws_ref[g], 0.0).astype(BF16)
        bias = bst_ref[:, g:g + 1]
        for ci in range(nchunk):
            rs = slice(ci * SG_CHUNK, (ci + 1) * SG_CHUNK)
            cs = slice(g * gw, (g + 1) * gw)
            mixed = _dot(w, vn[rs, cs]) + bias
            o_ref[rs, cs] = (u[rs, cs] * mixed).astype(BF16)


def _sg(hb, wt, l, row0, vg, vb, ws, bs, tm=512):
    n, d = hb.shape
    wspec, wscratch = _proj_specs(d, l, row0, 2 * D_SG)
    return pl.pallas_call(
        functools.partial(_sg_kernel, nchunk=tm // SG_CHUNK), grid=(n // tm,),
        in_specs=[pl.BlockSpec((tm, d), lambda i: (i, 0)), wspec,
                  pl.BlockSpec((1, D_SG), lambda i: (0, 0)),
                  pl.BlockSpec((1, D_SG), lambda i: (0, 0)),
                  pl.BlockSpec((N_SG, SG_CHUNK, SG_CHUNK), lambda i: (0, 0, 0)),
                  pl.BlockSpec((SG_CHUNK, N_SG), lambda i: (0, 0))],
        out_specs=pl.BlockSpec((tm, D_SG), lambda i: (i, 0)),
        out_shape=jax.ShapeDtypeStruct((n, D_SG), BF16),
        scratch_shapes=[wscratch],
        compiler_params=_cparams("arbitrary"), name="sg")(
            hb, wt, vg.reshape(1, D_SG), vb.reshape(1, D_SG), ws, bs.T)


def _rope_kernel(pos_ref, cos_ref, sin_ref):
    half = MLA_ROPE // 2
    lane = lax.broadcasted_iota(jnp.int32, (1, LANES), 1)
    idx = jnp.bitwise_and(lane, half - 1).astype(F32)
    freq = ROPE_BASE ** (-idx / half)
    ang = pos_ref[...] * freq
    c = jnp.cos(ang)
    s = jnp.sin(ang)
    cos_ref[...] = jnp.where(lane < MLA_ROPE, c, 0.0)
    sin_ref[...] = jnp.where(lane < half, -s, jnp.where(lane < MLA_ROPE, s, 0.0))


def _rope_tables(posb, tm=512):
    n = posb.shape[0]
    row = pl.BlockSpec((tm, LANES), lambda i: (i, 0))
    return pl.pallas_call(
        _rope_kernel, grid=(n // tm,), in_specs=[row], out_specs=[row, row],
        out_shape=[jax.ShapeDtypeStruct((n, LANES), F32)] * 2,
        compiler_params=_cparams("parallel"), name="rope_tables")(posb)


def _mla_prep_kernel(x_ref, w_ref, cos_ref, sin_ref, qg_ref, kvg_ref, wuq_ref, wuk_ref, wuv_ref,
                     q_ref, k_ref, v_ref, wbf):
    z = _project_tile(x_ref, w_ref, wbf)
    cq =_rms_norm(z[:, :MLA_Q_RANK], qg_ref[...]).astype(BF16)
    o1 = MLA_Q_RANK + MLA_KV_RANK
    ckv = _rms_norm(z[:, MLA_Q_RANK:o1], kvg_ref[...]).astype(BF16)
    cos_p = cos_ref[...]
    sin_p = sin_ref[...]
    kr = z[:, o1:o1 + LANES]
    half = MLA_ROPE // 2
    lane = lax.broadcasted_iota(jnp.int32, (1, LANES), 1)
    kr_swapped = jnp.where(lane < half, pltpu.roll(kr, LANES - half, 1), pltpu.roll(kr, half, 1))
    k_tail = (kr * cos_p + kr_swapped * sin_p).astype(BF16)
    qa = _dot(cq, wuq_ref[...])
    kn = _dot(ckv, wuk_ref[...])
    v_ref[...] = _dot(ckv, wuv_ref[...]).astype(BF16)
    scale = (MLA_NOPE + MLA_ROPE) ** -0.5
    for h in range(N_MLA):
        b0 = h * 3 * LANES
        q_tail = qa[:, b0 + LANES:b0 + 2 * LANES] * cos_p + qa[:, b0 + 2 * LANES:b0 + 3 * LANES] * sin_p
        q_ref[:, h * MLA_QK:h * MLA_QK + LANES] = (qa[:, b0:b0 + LANES] * scale).astype(BF16)
        q_ref[:, h * MLA_QK + LANES:(h + 1) * MLA_QK] = (q_tail * scale).astype(BF16)
        k_ref[:, h * MLA_QK:h * MLA_QK + LANES] = kn[:, h * LANES:(h + 1) * LANES].astype(BF16)
        k_ref[:, h * MLA_QK + LANES:(h + 1) * MLA_QK] = k_tail


def _mla_prep(hb, wt, l, row0, cos_p, sin_p, qg, kvg, wuq, wuk, wuv, tm=512):
    n, d = hb.shape
    row = lambda w: pl.BlockSpec((tm, w), lambda i: (i, 0))
    full = lambda a: pl.BlockSpec(a.shape, lambda i: (0,) * a.ndim)
    qg = qg.reshape(1, -1)
    kvg = kvg.reshape(1, -1)
    wspec, wscratch = _proj_specs(d, l, row0, MLA_Q_RANK + MLA_KV_RANK + LANES)
    return pl.pallas_call(
        _mla_prep_kernel, grid=(n // tm,),
        in_specs=[row(d), wspec, row(LANES), row(LANES), full(qg), full(kvg), full(wuq), full(wuk), full(wuv)],
        out_specs=[row(N_MLA * MLA_QK), row(N_MLA * MLA_QK), row(N_MLA * MLA_V)],
        out_shape=[jax.ShapeDtypeStruct((n, N_MLA * MLA_QK), BF16),
                   jax.ShapeDtypeStruct((n, N_MLA * MLA_QK), BF16),
                   jax.ShapeDtypeStruct((n, N_MLA * MLA_V), BF16)],
        scratch_shapes=[wscratch],
        compiler_params=_cparams("arbitrary"), name="mla_prep")(
            hb, wt, cos_p, sin_p, qg, kvg, wuq, wuk, wuv)


def _mla_attn_kernel(q_ref, k_ref, v_ref, o_ref, *, t, nblk):
    krow = lax.broadcasted_iota(jnp.int32, (t, t), 0)
    qcol = lax.broadcasted_iota(jnp.int32, (t, t), 1)
    visible = krow <= qcol
    for i in range(nblk):
        q = q_ref[i * t:(i + 1) * t, :]
        m = jnp.full((1, t), -jnp.inf, F32)
        l = jnp.zeros((1, t), F32)
        acc = jnp.zeros((MLA_V, t), F32)
        for j in range(i + 1):
            ks = slice(j * t, (j + 1) * t)
            s = _dotg(k_ref[ks, :], q, _NT)
            if j == i:
                s = jnp.where(visible, s, -jnp.inf)
            m_new = jnp.maximum(m, jnp.max(s, axis=0, keepdims=True))
            p = jnp.exp(s - m_new)
            a = jnp.exp(m - m_new)
            l = a * l + jnp.sum(p, axis=0, keepdims=True)
            acc = a * acc + _dotg(v_ref[ks, :], p.astype(BF16), _TN)
            m = m_new
        o_ref[i * t:(i + 1) * t, :] = (acc / l).T.astype(BF16)


def _mla_attn(q, k, v, batch, seq, t=512):
    n = q.shape[0]
    return pl.pallas_call(
        functools.partial(_mla_attn_kernel, t=t, nblk=seq // t), grid=(batch, N_MLA),
        in_specs=[pl.BlockSpec((seq, MLA_QK), lambda b, h: (b, h)),
                  pl.BlockSpec((seq, MLA_QK), lambda b, h: (b, h)),
                  pl.BlockSpec((seq, MLA_V), lambda b, h: (b, h))],
        out_specs=pl.BlockSpec((seq, MLA_V), lambda b, h: (b, h)),
        out_shape=jax.ShapeDtypeStruct((n, N_MLA * MLA_V), BF16),
        compiler_params=_cparams("parallel", "parallel"), name="mla_attn")(q, k, v)


def _gla_tile(z, o_ref, row0, st, wg_ref, bg_ref, ng_ref, nchunk):
    L = GLA_CHUNK
    qkw = N_GLA * GLA_DK
    vw = N_GLA * GLA_DV
    c_k, c_v, c_o, c_lr = qkw, 2 * qkw, 2 * qkw + vw, 2 * qkw + 2 * vw
    logits = _dot_hi(z[:, c_lr:c_lr + LANES], wg_ref[...]) + bg_ref[...]
    log_a = _log_sigmoid(logits) * (1.0 / GLA_TAU)
    lane = lax.broadcasted_iota(jnp.int32, (1, qkw), 1)
    masks = [((lane >= h * GLA_DK) & (lane < (h + 1) * GLA_DK)).astype(F32) for h in range(N_GLA)]
    r = lax.broadcasted_iota(jnp.int32, (L, L), 0)
    c = lax.broadcasted_iota(jnp.int32, (L, L), 1)
    causal = c <= r
    tril = jnp.where(causal, 1.0, 0.0).astype(BF16)
    ng = ng_ref[...]
    for ci in range(nchunk):
        rs = slice(ci * L, (ci + 1) * L)
        ro = slice(row0 + ci * L, row0 + (ci + 1) * L)
        b = _dot_exact_lhs(tril, log_a[rs])
        b_last = b[L - 1:L, :]
        q = z[rs, 0:qkw] * (GLA_DK ** -0.5)
        k = z[rs, c_k:c_k + qkw]
        qt = q * jnp.exp(b)
        kt = (k * jnp.exp(-b)).astype(BF16)
        kd = (k * jnp.exp(b_last - b)).astype(BF16)
        qstack = jnp.concatenate([qt * masks[h] for h in range(N_GLA)], axis=0).astype(BF16)
        att = _dotg(qstack, kt, _NT)
        inter = _dotg(qstack, st.astype(BF16), _NT)
        vb = z[rs, c_v:c_v + vw].astype(BF16)
        for h in range(N_GLA):
            hs = slice(h * L, (h + 1) * L)
            vs = slice(h * GLA_DV, (h + 1) * GLA_DV)
            a_h = jnp.where(causal, att[hs], 0.0).astype(BF16)
            o_h = _rms_norm(_dot(a_h, vb[:, vs]) + inter[hs], ng)
            g = z[rs, c_o + h * GLA_DV:c_o + (h + 1) * GLA_DV]
            o_ref[ro, vs] = (o_h * (g * _sigmoid(g))).astype(BF16)
        upd = _dotg(vb, kd, _TN)
        new = st * jnp.exp(b_last)
        for h in range(N_GLA):
            new = new + upd[h * GLA_DV:(h + 1) * GLA_DV] * masks[h]
        st = new
    return st


def _gla_kernel(x_ref, xn_ref, w_ref, wg_ref, bg_ref, ng_ref, o_ref, ws, za, zb, st_ref, *, tm):
    first = pl.program_id(1) == 0

    @pl.when((pl.program_id(0) == 0) & first)
    def _():
        ws[...] = w_ref[0].astype(BF16)

    @pl.when(first)
    def _():
        st_ref[...] = jnp.zeros_like(st_ref)
        za[...] = _dotg(x_ref[0:tm, :], ws[...], _NT)

    nchunk = tm // GLA_CHUNK
    zb[...] = _dotg(x_ref[tm:2 * tm, :], ws[...], _NT)
    st = _gla_tile(za, o_ref, 0, st_ref[...], wg_ref, bg_ref, ng_ref, nchunk)
    za[...] = _dotg(xn_ref[...], ws[...], _NT)
    st_ref[...] = _gla_tile(zb, o_ref, tm, st, wg_ref, bg_ref, ng_ref, nchunk)


def _gla(hb, wt, l, row0, wg, bg, ng, batch, seq, tm=256):
    n, d = hb.shape
    ns2 = seq // (2 * tm)
    qkw = N_GLA * GLA_DK
    vw = N_GLA * GLA_DV
    zw = 2 * qkw + 2 * vw + LANES
    full = lambda a: pl.BlockSpec(a.shape, lambda b, s: (0,) * a.ndim)
    bg = bg.reshape(1, qkw)
    ng = ng.reshape(1, GLA_DV)
    last = 2 * ns2 - 1
    return pl.pallas_call(
        functools.partial(_gla_kernel, tm=tm), grid=(batch, ns2),
        in_specs=[pl.BlockSpec((2 * tm, d), lambda b, s: (b * ns2 + s, 0)),
                  pl.BlockSpec((tm, d), lambda b, s: (b * 2 * ns2 + jnp.minimum(2 * s + 2, last), 0)),
                  pl.BlockSpec((pl.Element(1), pl.Element(zw), pl.Element(d)), lambda b, s: (l, row0, 0),
                               pipeline_mode=pl.Buffered(1)),
                  full(wg), full(bg), full(ng)],
        out_specs=pl.BlockSpec((2 * tm, vw), lambda b, s: (b * ns2 + s, 0)),
        out_shape=jax.ShapeDtypeStruct((n, vw), BF16),
        scratch_shapes=[pltpu.VMEM((zw, d), BF16), pltpu.VMEM((tm, zw), F32), pltpu.VMEM((tm, zw), F32),
                        pltpu.VMEM((GLA_DV, qkw), F32)],
        compiler_params=_cparams("arbitrary", "arbitrary"), name="gla")(hb, hb, wt, wg, bg, ng)


def _mlstm_tile(z, o_ref, row0, carry, cw_ref, cb_ref, gb_ref, ng_ref, nchunk):
    L = ML_CHUNK
    qkw = N_ML * ML_DK
    vw = N_ML * ML_DV
    tm = nchunk * L
    c_v, c_o, c_if = 2 * qkw, 2 * qkw + vw, 2 * qkw + 2 * vw
    ct, nrow, mwide, mk, tail = carry

    x = z[:, 0:2 * qkw]
    row8 = lax.broadcasted_iota(jnp.int32, (SUBLANES, 2 * qkw), 0)
    acc = x * cw_ref[3:4, :] + cb_ref[...]
    for j in range(1, 4):
        rx = pltpu.roll(x, j, 0)
        fix = jnp.where(row8 < j, pltpu.roll(tail, j, 0), rx[0:SUBLANES])
        acc = acc + jnp.concatenate([fix, rx[SUBLANES:]], axis=0) * cw_ref[3 - j:4 - j, :]
    tail = x[tm - SUBLANES:tm]
    y = acc * _sigmoid(acc)
    q = y[:, :qkw] * (ML_DK ** -0.5)
    k = y[:, qkw:]

    gates = z[:, c_if:c_if + LANES] + gb_ref[...]
    fc = _log_sigmoid(gates)
    gt = gates.T[0:2 * N_ML]
    fct = _log_sigmoid(gt)

    lane = lax.broadcasted_iota(jnp.int32, (1, qkw), 1)
    masks = [((lane >= h * ML_DK) & (lane < (h + 1) * ML_DK)).astype(F32) for h in range(N_ML)]
    r = lax.broadcasted_iota(jnp.int32, (L, L), 0)
    c = lax.broadcasted_iota(jnp.int32, (L, L), 1)
    causal = c <= r
    tril = jnp.where(causal, 1.0, 0.0).astype(BF16)
    triu = jnp.where(r <= c, 1.0, 0.0).astype(BF16)

    def selector(width, block, first):
        rr = lax.broadcasted_iota(jnp.int32, (LANES, width), 0)
        cc = lax.broadcasted_iota(jnp.int32, (LANES, width), 1)
        return jnp.where(rr == (cc >> int(math.log2(block))) + first, 1.0, 0.0).astype(BF16)

    wide = N_ML * L
    fcb = _dot_exact_rhs(fc, selector(wide, L, N_ML))
    icb = _dot_exact_rhs(gates, selector(wide, L, 0))
    fck = _dot_exact_rhs(fc, selector(qkw, ML_DK, N_ML))
    ick = _dot_exact_rhs(gates, selector(qkw, ML_DK, 0))

    ng = ng_ref[...]

    for ci in range(nchunk):
        rs = slice(ci * L, (ci + 1) * L)
        ro = slice(row0 + ci * L, row0 + (ci + 1) * L)
        bb = _dot_exact_lhs(tril, fcb[rs])
        bk = _dot_exact_lhs(tril, fck[rs])
        brow = _dot_exact_rhs(fct[:, rs], triu)
        rowterm = gt[:, rs] - pltpu.roll(brow, N_ML, 0)
        qc = q[rs]
        kc = k[rs]
        log_d = jnp.concatenate(
            [jnp.where(causal, bb[:, h * L:(h + 1) * L] + rowterm[h:h + 1, :], -jnp.inf) for h in range(N_ML)],
            axis=0)
        log_inter = jnp.concatenate(
            [bb[:, h * L:(h + 1) * L] + mwide[:, h * L:(h + 1) * L] for h in range(N_ML)], axis=0)
        m_t = jnp.maximum(log_inter, jnp.max(log_d, axis=-1, keepdims=True))
        w_inter = jnp.exp(log_inter - m_t)
        qst = jnp.concatenate([qc * masks[h] for h in range(N_ML)], axis=0)
        qsb = qst.astype(BF16)
        s_all = _dotg(qsb, kc.astype(BF16), _NT) * jnp.exp(log_d - m_t)
        sb = s_all.astype(BF16)
        vb = z[rs, c_v:c_v + vw].astype(BF16)
        num = jnp.concatenate(
            [_dot(sb[h * L:(h + 1) * L], vb[:, h * ML_DV:(h + 1) * ML_DV]) for h in range(N_ML)], axis=0)
        num = num + w_inter * _dotg(qsb, ct.astype(BF16), _NT)
        den = jnp.sum(s_all, axis=-1, keepdims=True) + w_inter * jnp.sum(qst * nrow, axis=-1, keepdims=True)
        hh = _rms_norm(num / jnp.maximum(jnp.abs(den), jnp.exp(-m_t)), ng)
        for h in range(N_ML):
            vs = slice(h * ML_DV, (h + 1) * ML_DV)
            gate_o = _sigmoid(z[rs, c_o + h * ML_DV:c_o + (h + 1) * ML_DV])
            o_ref[ro, vs] = (hh[h * L:(h + 1) * L] * gate_o).astype(BF16)
        bl_w = bb[L - 1:L, :]
        lw_w = bl_w - bb + icb[rs]
        mwide_new = jnp.maximum(bl_w + mwide, jnp.max(lw_w, axis=0, keepdims=True))
        bl_k = bk[L - 1:L, :]
        lw_k = bl_k - bk + ick[rs]
        mk_new = jnp.maximum(bl_k + mk, jnp.max(lw_k, axis=0, keepdims=True))
        decay = jnp.exp(bl_k + mk - mk_new)
        kw = kc * jnp.exp(lw_k - mk_new)
        upd = _dotg(vb, kw.astype(BF16), _TN)
        ct = ct * decay
        for h in range(N_ML):
            ct = ct + upd[h * ML_DV:(h + 1) * ML_DV] * masks[h]
        nrow = nrow * decay + jnp.sum(kw, axis=0, keepdims=True)
        mwide = mwide_new
        mk = mk_new

    return ct, nrow, mwide, mk, tail


def _mlstm_kernel(z_ref, cw_ref, cb_ref, gb_ref, ng_ref, o_ref, ct_ref, n_ref, mw_ref, mk_ref, tail_ref):
    state = (ct_ref, n_ref, mw_ref, mk_ref, tail_ref)

    @pl.when(pl.program_id(1) == 0)
    def _():
        for ref in state:
            ref[...] = jnp.zeros_like(ref)

    carry = _mlstm_tile(z_ref, o_ref, 0, tuple(ref[...] for ref in state),
                        cw_ref, cb_ref, gb_ref, ng_ref, z_ref.shape[0] // ML_CHUNK)
    for ref, val in zip(state, carry):
        ref[...] = val


ML_ZW = 2 * N_ML * ML_DK + 2 * N_ML * ML_DV + LANES


def _mlstm(z, cw, cb, gb, ng, batch, seq, tm=512):
    n = z.shape[0]
    ns = seq // tm
    qkw = N_ML * ML_DK
    vw = N_ML * ML_DV
    full = lambda a: pl.BlockSpec(a.shape, lambda b, s: (0,) * a.ndim)
    cb = cb.reshape(1, 2 * qkw)
    gbp = jnp.pad(gb, (0, LANES - 2 * N_ML)).reshape(1, LANES)
    ng = ng.reshape(1, ML_DV)
    return pl.pallas_call(
        _mlstm_kernel, grid=(batch, ns),
        in_specs=[pl.BlockSpec((tm, ML_ZW), lambda b, s: (b * ns + s, 0)),
                  full(cw), full(cb), full(gbp), full(ng)],
        out_specs=pl.BlockSpec((tm, vw), lambda b, s: (b * ns + s, 0)),
        out_shape=jax.ShapeDtypeStruct((n, vw), BF16),
        scratch_shapes=[pltpu.VMEM((ML_DV, qkw), F32), pltpu.VMEM((1, qkw), F32),
                        pltpu.VMEM((1, N_ML * ML_CHUNK), F32), pltpu.VMEM((1, qkw), F32),
                        pltpu.VMEM((SUBLANES, 2 * qkw), F32)],
        compiler_params=_cparams("parallel", "arbitrary"), name="mlstm")(z, cw, cb, gbp, ng)


def _merge_out_kernel(ya_ref, yb_ref, yc_ref, yd_ref, g_ref, wb_ref, wo_ref, h_ref, lg_ref, lb_ref,
                      of_ref, ob_ref):
    acc = None
    for i, y_ref in enumerate((ya_ref, yb_ref, yc_ref, yd_ref)):
        p = _dot(y_ref[...], wb_ref[i])
        t = g_ref[:, i * D_MODEL:(i + 1) * D_MODEL].astype(F32) * p
        acc = t if acc is None else acc + t
    y = _dot(acc.astype(BF16), wo_ref[...])
    o = _layer_norm(ALPHA * h_ref[...] + y, lg_ref[...], lb_ref[...])
    of_ref[...] = o
    ob_ref[...] = o.astype(BF16)


def _merge_out(ya, yb, yc, yd, gates, wb_all, wo_all, l, hf, lg, lb, tm=256):
    n, bw = ya.shape
    d = hf.shape[1]
    row = pl.BlockSpec((tm, bw), lambda i: (i, 0))
    hrow = pl.BlockSpec((tm, d), lambda i: (i, 0))
    par = pl.BlockSpec((1, d), lambda i: (0, 0))
    once = dict(pipeline_mode=pl.Buffered(1))
    return pl.pallas_call(
        _merge_out_kernel, grid=(n // tm,),
        in_specs=[row, row, row, row, pl.BlockSpec((tm, 4 * d), lambda i: (i, 0)),
                  pl.BlockSpec((None,) + wb_all.shape[1:], lambda i: (l, 0, 0, 0), **once),
                  pl.BlockSpec((None,) + wo_all.shape[1:], lambda i: (l, 0, 0), **once), hrow, par, par],
        out_specs=[hrow, hrow],
        out_shape=[jax.ShapeDtypeStruct((n, d), F32), jax.ShapeDtypeStruct((n, d), BF16)],
        compiler_params=_cparams("parallel"), name="merge_out_ln1")(
            ya, yb, yc, yd, gates, wb_all, wo_all, hf, lg.reshape(1, d), lb.reshape(1, d))


def _xattn_kernel(hb_ref, hf_ref, wq_ref, k_ref, v_ref, wo_ref, g_ref, b_ref, wrt_ref, rbc_ref,
                  of_ref, pay_ref, dest_ref, cnt_ref, carry_ref, *, region):
    q = (_dot(hb_ref[...], wq_ref[...]) * (X_HEAD ** -0.5)).astype(BF16)
    outs = []
    for h in range(N_X):
        hs = slice(h * X_HEAD, (h + 1) * X_HEAD)
        s = _dotg(q[:, hs], k_ref[:, hs], _NT)
        p = jnp.exp(s - jnp.max(s, axis=-1, keepdims=True))
        l = jnp.sum(p, axis=-1, keepdims=True)
        outs.append((_dot(p.astype(BF16), v_ref[:, hs]) / l).astype(BF16))
    y = _dot(jnp.concatenate(outs, axis=-1), wo_ref[...])
    o = _layer_norm(ALPHA * hf_ref[...] + y, g_ref[...], b_ref[...])
    of_ref[...] = o
    is_first = (pl.program_id(0) == 0) & (pl.program_id(1) == 0)
    _route_tile(o, is_first, wrt_ref, rbc_ref, pay_ref, dest_ref, cnt_ref, carry_ref, region)


def _xattn_route(hb, hf, wq_all, kv, wo_all, l, g, b, wrt, rbc, region, batch, seq, mem_len, tm=512):
    n, d = hf.shape
    ns = seq // tm
    xw = N_X * X_HEAD
    tile = lambda bb, s: bb * ns + s
    row = pl.BlockSpec((tm, d), lambda bb, s: (tile(bb, s), 0))
    par = pl.BlockSpec((1, d), lambda bb, s: (0, 0))
    return pl.pallas_call(
        functools.partial(_xattn_kernel, region=region), grid=(batch, ns),
        in_specs=[row, row, pl.BlockSpec((None, d, xw), lambda bb, s: (l, 0, 0)),
                  pl.BlockSpec((mem_len, xw), lambda bb, s: (bb, 0)),
                  pl.BlockSpec((mem_len, xw), lambda bb, s: (bb, 1)),
                  pl.BlockSpec((None, xw, d), lambda bb, s: (l, 0, 0)), par, par,
                  pl.BlockSpec((LANES, d), lambda bb, s: (0, 0)),
                  pl.BlockSpec((N_EXPERTS, 1), lambda bb, s: (0, 0))],
        out_specs=[row, pl.BlockSpec((tm, PAY_W), lambda bb, s: (tile(bb, s), 0)),
                   pl.BlockSpec((1, tm), lambda bb, s: (0, tile(bb, s))),
                   pl.BlockSpec((SUBLANES, LANES), lambda bb, s: (0, 0))],
        out_shape=[jax.ShapeDtypeStruct((n, d), F32), jax.ShapeDtypeStruct((n, PAY_W), F32),
                   jax.ShapeDtypeStruct((1, n), jnp.int32), jax.ShapeDtypeStruct((SUBLANES, LANES), jnp.int32)],
        scratch_shapes=[pltpu.VMEM((SUBLANES, LANES), F32)],
        compiler_params=_cparams("arbitrary", "arbitrary"), name="xattn_route")(
            hb, hf, wq_all, kv, kv, wo_all, g.reshape(1, d), b.reshape(1, d), wrt, rbc)


N_GROUPS = 4
PER_GROUP = N_EXPERTS // N_GROUPS
MOE_TILE = 512
PAY_W = D_MODEL + LANES

def _route_tile(h, is_first, wrt_ref, rbc_ref, pay_ref, dest_ref, cnt_ref, carry_ref, region):
    tm = h.shape[0]

    @pl.when(is_first)
    def _():
        carry_ref[...] = jnp.zeros_like(carry_ref)

    w = wrt_ref[...]
    w1 = w.astype(BF16)
    w2 = (w - w1.astype(F32)).astype(BF16)
    h1 = h.astype(BF16)
    h2 = (h - h1.astype(F32)).astype(BF16)
    logits = _dotg(w1, h1, _NT) + _dotg(w2, h1, _NT) + _dotg(w1, h2, _NT)
    aff = _sigmoid(logits[0:N_EXPERTS])
    biased = aff + rbc_ref[...]
    row = lax.broadcasted_iota(jnp.int32, (N_EXPERTS, 1), 0).astype(F32)
    big = float(LANES)
    best = e1 = e2 = None
    for g in range(N_GROUPS):
        x = jnp.where((row >= g * PER_GROUP) & (row < (g + 1) * PER_GROUP), biased, -jnp.inf)
        m1 = jnp.max(x, axis=0, keepdims=True)
        i1 = jnp.min(jnp.where(x == m1, row, big), axis=0, keepdims=True)
        x2 = jnp.where(row == i1, -jnp.inf, x)
        m2 = jnp.max(x2, axis=0, keepdims=True)
        i2 = jnp.min(jnp.where(x2 == m2, row, big), axis=0, keepdims=True)
        score = m1 + m2
        if g == 0:
            best, e1, e2 = score, i1, i2
        else:
            better = score > best
            best = jnp.where(better, score, best)
            e1 = jnp.where(better, i1, e1)
            e2 = jnp.where(better, i2, e2)
    s1 = jnp.sum(jnp.where(row == e1, aff, 0.0), axis=0, keepdims=True)
    s2 = jnp.sum(jnp.where(row == e2, aff, 0.0), axis=0, keepdims=True)
    tot = s1 + s2
    gates_t = jnp.where(row == e1, s1 / tot, 0.0) + jnp.where(row == e2, s2 / tot, 0.0)

    grp = jnp.zeros_like(e1)
    for g in range(1, N_GROUPS):
        grp = grp + jnp.where(e1 >= g * PER_GROUP, 1.0, 0.0)
    row8 = lax.broadcasted_iota(jnp.int32, (SUBLANES, 1), 0).astype(F32)
    onehot = jnp.where(row8 == grp, 1.0, 0.0)
    r = lax.broadcasted_iota(jnp.int32, (tm, tm), 0)
    c = lax.broadcasted_iota(jnp.int32, (tm, tm), 1)
    earlier = jnp.where(r < c, 1.0, 0.0).astype(BF16)
    rank_in = _dot(onehot.astype(BF16), earlier)
    carry = carry_ref[...]
    rank = jnp.sum(onehot * (rank_in + carry[:, 0:1]), axis=0, keepdims=True)
    dest_ref[...] = (grp * float(region) + rank).astype(jnp.int32)
    carry = carry + jnp.sum(onehot, axis=1, keepdims=True)
    carry_ref[...] = carry
    cnt_ref[...] = carry.astype(jnp.int32)

    gates = jnp.concatenate([gates_t, jnp.zeros((LANES - N_EXPERTS, tm), F32)], axis=0).T
    pay_ref[:, :D_MODEL] = h
    pay_ref[:, D_MODEL:] = gates


def _scatter_kernel(dest_ref, cnt_ref, pay_ref, out_ref, buf, zbuf, sem, zsem, *, region, nsteps):
    tm = pay_ref.shape[0]
    i = pl.program_id(0)
    slot = i % 2

    def wait_rows(s):
        pltpu.make_async_copy(buf.at[s], out_ref.at[pl.ds(0, tm)], sem.at[s]).wait()

    @pl.when(i == 0)
    def _():
        zbuf[...] = jnp.zeros_like(zbuf)
        copies = []
        for g in range(N_GROUPS):
            start = pl.multiple_of(g * region + (cnt_ref[g] // MOE_TILE) * MOE_TILE, MOE_TILE)
            copies.append(pltpu.make_async_copy(zbuf, out_ref.at[pl.ds(start, MOE_TILE)], zsem))
        for cp in copies:
            cp.start()
        for cp in copies:
            cp.wait()

    @pl.when(i >= 2)
    def _():
        wait_rows(slot)

    buf[slot] = pay_ref[...]

    base = i * tm
    for r in range(tm):
        d = dest_ref[base + r]
        pltpu.make_async_copy(buf.at[slot, pl.ds(r, 1)], out_ref.at[pl.ds(d, 1)],
                              sem.at[slot]).start(priority=r % 2)

    @pl.when(i == nsteps - 1)
    def _():
        wait_rows(slot)
        if nsteps >= 2:
            wait_rows(1 - slot)


def _scatter(dest, cnt, pay, region, tm=512):
    n = pay.shape[0]
    nsteps = n // tm
    return pl.pallas_call(
        functools.partial(_scatter_kernel, region=region, nsteps=nsteps),
        grid_spec=pltpu.PrefetchScalarGridSpec(
            num_scalar_prefetch=2, grid=(nsteps,),
            in_specs=[pl.BlockSpec((tm, PAY_W), lambda i, d, c: (i, 0))],
            out_specs=pl.BlockSpec(memory_space=pl.ANY),
            scratch_shapes=[pltpu.VMEM((2, tm, PAY_W), F32), pltpu.VMEM((MOE_TILE, PAY_W), F32),
                            pltpu.SemaphoreType.DMA((2,)), pltpu.SemaphoreType.DMA(())]),
        out_shape=jax.ShapeDtypeStruct((N_GROUPS * region, PAY_W), F32),
        compiler_params=_cparams("arbitrary"), name="moe_scatter")(dest, cnt, pay)


def _tile_tables(cnt, region, ntiles):
    nt = (cnt + MOE_TILE - 1) // MOE_TILE
    ends = jnp.cumsum(nt)
    starts = ends - nt
    total = ends[-1]
    i = jnp.minimum(jnp.arange(ntiles, dtype=jnp.int32), total - 1)
    g = jnp.sum((i[:, None] >= ends[None, :]).astype(jnp.int32), axis=1)
    blk = g * (region // MOE_TILE) + i - starts[g]
    return g.astype(jnp.int32), blk.astype(jnp.int32), total.reshape(1).astype(jnp.int32)


def _moe_up_kernel(tg_ref, tb_ref, nt_ref, x_ref, wg_ref, wu_ref, o_ref, wgs, wus):
    e = pl.program_id(0)
    i = pl.program_id(1)
    grp = tg_ref[i]
    changed = (i == 0) | (grp != tg_ref[jnp.maximum(i - 1, 0)])

    @pl.when(changed)
    def _():
        wgs[...] = wg_ref[...].astype(BF16)
        wus[...] = wu_ref[...].astype(BF16)

    @pl.when(i < nt_ref[0])
    def _():
        x = x_ref[:, :D_MODEL].astype(BF16)
        gates = x_ref[:, D_MODEL:]
        lane = lax.broadcasted_iota(jnp.int32, (1, LANES), 1)
        ge = jnp.sum(jnp.where(lane == grp * PER_GROUP + e, gates, 0.0), axis=-1, keepdims=True)
        a = _dot(x, wgs[...])
        u = _dot(x, wus[...])
        o_ref[...] = (a * _sigmoid(a) * u * ge).astype(BF16)


def _moe_up(tg, tb, nt, xs, wg, wu, l):
    rows = xs.shape[0]
    ntiles = tg.shape[0]
    d, de = wg.shape[2], wg.shape[3]
    wspec = pl.BlockSpec((None, None, d, de), lambda e, i, tg, tb, nt: (l, tg[i] * PER_GROUP + e, 0, 0))
    return pl.pallas_call(
        _moe_up_kernel,
        grid_spec=pltpu.PrefetchScalarGridSpec(
            num_scalar_prefetch=3, grid=(PER_GROUP, ntiles),
            in_specs=[pl.BlockSpec((MOE_TILE, PAY_W), lambda e, i, tg, tb, nt: (tb[i], 0)), wspec, wspec],
            out_specs=pl.BlockSpec((MOE_TILE, de), lambda e, i, tg, tb, nt: (tb[i], e)),
            scratch_shapes=[pltpu.VMEM((d, de), BF16), pltpu.VMEM((d, de), BF16)]),
        out_shape=jax.ShapeDtypeStruct((rows, PER_GROUP * de), BF16),
        compiler_params=_cparams("arbitrary", "arbitrary"), name="moe_up")(tg, tb, nt, xs, wg, wu)


def _moe_down_kernel(tg_ref, tb_ref, nt_ref, x_ref, w_ref, o_ref, ws):
    i = pl.program_id(1)
    changed = (i == 0) | (tg_ref[i] != tg_ref[jnp.maximum(i - 1, 0)])

    @pl.when(changed)
    def _():
        ws[...] = w_ref[...].astype(BF16)

    @pl.when(i < nt_ref[0])
    def _():
        o_ref[...] = _dot(x_ref[...], ws[...])


def _moe_down(tg, tb, nt, hid, wd, l, tn=1024):
    rows, k = hid.shape
    ntiles = tg.shape[0]
    d = wd.shape[3]
    return pl.pallas_call(
        _moe_down_kernel,
        grid_spec=pltpu.PrefetchScalarGridSpec(
            num_scalar_prefetch=3, grid=(d // tn, ntiles),
            in_specs=[pl.BlockSpec((MOE_TILE, k), lambda c, i, tg, tb, nt: (tb[i], 0)),
                      pl.BlockSpec((None, None, k, tn), lambda c, i, tg, tb, nt: (l, tg[i], 0, c))],
            out_specs=pl.BlockSpec((MOE_TILE, tn), lambda c, i, tg, tb, nt: (tb[i], c)),
            scratch_shapes=[pltpu.VMEM((k, tn), BF16)]),
        out_shape=jax.ShapeDtypeStruct((rows, d), F32),
        compiler_params=_cparams("arbitrary", "arbitrary"), name="moe_down")(tg, tb, nt, hid, wd)


def _gather_ln_kernel(dest_ref, y_ref, h_ref, g_ref, b_ref, of_ref, ob_ref, buf, sem, *, nsteps):
    tm = h_ref.shape[0]
    i = pl.program_id(0)

    def issue(step, slot):
        base = step * tm
        for r in range(tm):
            d = dest_ref[base + r]
            pltpu.make_async_copy(y_ref.at[pl.ds(d, 1)], buf.at[slot, pl.ds(r, 1)],
                                  sem.at[slot]).start(priority=r % 2)

    @pl.when(i == 0)
    def _():
        issue(0, 0)

    @pl.when(i + 1 < nsteps)
    def _():
        issue(i + 1, (i + 1) % 2)

    slot = i % 2
    pltpu.make_async_copy(y_ref.at[pl.ds(0, tm)], buf.at[slot], sem.at[slot]).wait()
    o = _layer_norm(ALPHA * h_ref[...] + buf[slot], g_ref[...], b_ref[...])
    of_ref[...] = o
    ob_ref[...] = o.astype(BF16)


def _gather_ln(dest, ys, hf, g, b, tm=512):
    n, d = hf.shape
    nsteps = n // tm
    row = pl.BlockSpec((tm, d), lambda i, dref: (i, 0))
    par = pl.BlockSpec((1, d), lambda i, dref: (0, 0))
    return pl.pallas_call(
        functools.partial(_gather_ln_kernel, nsteps=nsteps),
        grid_spec=pltpu.PrefetchScalarGridSpec(
            num_scalar_prefetch=1, grid=(nsteps,),
            in_specs=[pl.BlockSpec(memory_space=pl.ANY), row, par, par],
            out_specs=[row, row],
            scratch_shapes=[pltpu.VMEM((2, tm, d), F32), pltpu.SemaphoreType.DMA((2,))]),
        out_shape=[jax.ShapeDtypeStruct((n, d), F32), jax.ShapeDtypeStruct((n, d), BF16)],
        compiler_params=_cparams("arbitrary"), name="moe_gather_ln3")(
            dest, ys, hf, g.reshape(1, d), b.reshape(1, d))


def _moe_region(n):
    return n + MOE_TILE


def _moe(hf, pay, dest, cnt, p, w, l):
    n = hf.shape[0]
    region = _moe_region(n)
    ntiles = n // MOE_TILE + N_GROUPS
    dest = dest.reshape(n)
    cnt = cnt[:N_GROUPS, 0]
    tg, tb, nt = _tile_tables(cnt, region, ntiles)
    xs = _scatter(dest, cnt, pay, region)
    hid = _moe_up(tg, tb, nt, xs, p["moe_w_gate"], p["moe_w_up"], l)
    ys = _moe_down(tg, tb, nt, hid, w["moe_wd"], l)
    return _gather_ln(dest, ys, hf, p["ln3_g"][l], p["ln3_b"][l])


def _prep_params(p):
    L = p["w_in"].shape[0]
    half = MLA_ROPE // 2
    pad_r = LANES - MLA_ROPE

    wq = p["mla_w_uq"].reshape(L, MLA_Q_RANK, N_MLA, MLA_NOPE + MLA_ROPE)
    rq = wq[..., MLA_NOPE:]
    zq = jnp.zeros((L, MLA_Q_RANK, N_MLA, pad_r), F32)
    wuq = jnp.concatenate([wq[..., :MLA_NOPE], rq, zq, rq[..., half:], rq[..., :half], zq], axis=-1)
    wkv = p["mla_w_ukv"].reshape(L, MLA_KV_RANK, N_MLA, MLA_NOPE + MLA_V)

    bf = lambda a: a.astype(BF16)
    ne = p["moe_w_down"].shape[1]
    return dict(
        wt_in=jnp.swapaxes(p["w_in"], 1, 2),
        wuq=bf(wuq.reshape(L, MLA_Q_RANK, N_MLA * 3 * LANES)),
        wuk=bf(wkv[..., :MLA_NOPE].reshape(L, MLA_KV_RANK, N_MLA * MLA_NOPE)),
        wuv=bf(wkv[..., MLA_NOPE:].reshape(L, MLA_KV_RANK, N_MLA * MLA_V)),
        gla_wg=jnp.pad(p["gla_w_gate"], ((0, 0), (0, LANES - GLA_RANK), (0, 0))),
        w_branch=bf(p["w_branch"]), w_out=bf(p["w_out"]),
        x_w_q=bf(p["x_w_q"]), x_w_kv=bf(p["x_w_kv"]), x_w_o=bf(p["x_w_o"]),
        w_router_t=jnp.pad(p["w_router"].T, ((0, LANES - N_EXPERTS), (0, 0))),
        router_bias_c=p["router_bias"].reshape(N_EXPERTS, 1),
        moe_wd=p["moe_w_down"].reshape(L, N_GROUPS, (ne // N_GROUPS) * D_EXPERT, D_MODEL),
    )


def _mixer(hf, hb, cos_p, sin_p, p, w, l, batch, seq):
    wt = w["wt_in"]
    o_mla = 2 * D_SG
    o_gla = o_mla + MLA_Q_RANK + MLA_KV_RANK + MLA_ROPE
    o_ml = o_gla + 2 * N_GLA * GLA_DK + 2 * N_GLA * GLA_DV + GLA_RANK
    o_gate = o_ml + 2 * N_ML * ML_DK + 2 * N_ML * ML_DV + 2 * N_ML
    gates = _mm_wt(hb, wt, l, o_gate, 4 * D_MODEL // 1024, 1024, BF16, act="sigmoid", name="mm_gate")

    y_a = _sg(hb, wt, l, 0, p["sg_vnorm_g"][l], p["sg_vnorm_b"][l], p["sg_w_s"][l], p["sg_b_s"][l])
    q, k, v = _mla_prep(hb, wt, l, o_mla, cos_p, sin_p, p["mla_qnorm_g"][l], p["mla_kvnorm_g"][l],
                        w["wuq"][l], w["wuk"][l], w["wuv"][l])
    y_b = _mla_attn(q, k, v, batch, seq)
    y_c = _gla(hb, wt, l, o_gla, w["gla_wg"][l], p["gla_b_gate"][l], p["gla_norm_g"][l], batch, seq)
    z_ml = _mm_wt(hb, wt, l, o_ml, 1, ML_ZW, F32, name="mm_ml")
    y_d = _mlstm(z_ml, p["ml_conv_w"][l], p["ml_conv_b"][l], p["ml_gate_b"][l], p["ml_norm_g"][l], batch, seq)
    return _merge_out(y_a, y_b, y_c, y_d, gates, w["w_branch"], w["w_out"], l, hf,
                      p["ln1_g"][l], p["ln1_b"][l])


def _forward(p):
    x = p["x"]
    batch, seq, d = x.shape
    n = batch * seq
    mem = p["mem"]
    mem_len = mem.shape[1]
    w = _prep_params(p)
    posb = jnp.broadcast_to(p["positions"].reshape(n, 1).astype(F32), (n, LANES))
    cos_p, sin_p = _rope_tables(posb)
    memb = mem.reshape(batch * mem_len, d).astype(BF16)
    hf, hb = _ln(x.reshape(n, d), p["ln_in_g"], p["ln_in_b"])
    for l in range(p["w_in"].shape[0]):
        hf, hb = _mixer(hf, hb, cos_p, sin_p, p, w, l, batch, seq)
        kv = _mm(memb, w["x_w_kv"][l], BF16, tm=512, tn=1024, name="mm_xkv")
        hf, pay, dest, cnt = _xattn_route(hb, hf, w["x_w_q"], kv, w["x_w_o"], l, p["ln2_g"][l], p["ln2_b"][l],
                                          w["w_router_t"], w["router_bias_c"], _moe_region(n),
                                          batch, seq, mem_len)
        hf, hb = _moe(hf, pay, dest, cnt, p, w, l)
    return hf.reshape(batch, seq, d)


def kernel(x, mem, positions, ln_in_g, ln_in_b, w_in, sg_vnorm_g, sg_vnorm_b, sg_w_s, sg_b_s, mla_qnorm_g, mla_kvnorm_g, mla_w_uq, mla_w_ukv, gla_w_gate, gla_b_gate, gla_norm_g, ml_conv_w, ml_conv_b, ml_gate_b, ml_norm_g, w_branch, w_out, ln1_g, ln1_b, x_w_q, x_w_kv, x_w_o, ln2_g, ln2_b, w_router, router_bias, moe_w_gate, moe_w_up, moe_w_down, ln3_g, ln3_b):
    return _forward(dict(
        x=x, mem=mem, positions=positions, ln_in_g=ln_in_g, ln_in_b=ln_in_b, w_in=w_in,
        sg_vnorm_g=sg_vnorm_g, sg_vnorm_b=sg_vnorm_b, sg_w_s=sg_w_s, sg_b_s=sg_b_s,
        mla_qnorm_g=mla_qnorm_g, mla_kvnorm_g=mla_kvnorm_g, mla_w_uq=mla_w_uq, mla_w_ukv=mla_w_ukv,
        gla_w_gate=gla_w_gate, gla_b_gate=gla_b_gate, gla_norm_g=gla_norm_g,
        ml_conv_w=ml_conv_w, ml_conv_b=ml_conv_b, ml_gate_b=ml_gate_b, ml_norm_g=ml_norm_g,
        w_branch=w_branch, w_out=w_out, ln1_g=ln1_g, ln1_b=ln1_b,
        x_w_q=x_w_q, x_w_kv=x_w_kv, x_w_o=x_w_o, ln2_g=ln2_g, ln2_b=ln2_b,
        w_router=w_router, router_bias=router_bias,
        moe_w_gate=moe_w_gate, moe_w_up=moe_w_up, moe_w_down=moe_w_down, ln3_g=ln3_g, ln3_b=ln3_b))
```
